```python
import math
import jax, jax.numpy as jnp
from jax import lax
import numpy as np

D_MODEL = 1024
BATCH = 8
SEQ = 2048
DEPTH = 1
DEC_BATCH = 32
DEC_SEQ = 1
PAST_LEN = 16384
PAGE_SIZE = 128

FOX_HEADS = 8
FOX_HEAD_DIM = 64
FOX_WIDTH = FOX_HEADS * FOX_HEAD_DIM
Q_BLOCK = 128
FORGET_BIAS_INIT = 3.0
GDN_HEADS = 4
GDN_HEAD_DIM = 128
GDN_WIDTH = GDN_HEADS * GDN_HEAD_DIM
CONV_WIDTH = 4
CONV_CH = 3 * GDN_WIDTH
CHUNK = 64
D_FF = 2816
EPS = 1e-6
SPLIT_SIZES = (FOX_WIDTH, FOX_WIDTH, FOX_WIDTH, FOX_HEADS, CONV_CH, GDN_HEADS, GDN_HEADS, GDN_WIDTH, D_MODEL, D_MODEL)
P_IN = 3 * FOX_WIDTH + FOX_HEADS + CONV_CH + 2 * GDN_HEADS + GDN_WIDTH + 2 * D_MODEL

kernel_name = "fox_gdn_parallel_macaron_step"


def rmsnorm(x, g):
    xf = x.astype(jnp.float32)
    y = xf * lax.rsqrt(jnp.mean(xf * xf, axis=-1, keepdims=True) + EPS)
    return (y * g.astype(jnp.float32)).astype(x.dtype)


def l2norm(x):
    xf = x.astype(jnp.float32)
    return xf * lax.rsqrt(jnp.sum(xf * xf, axis=-1, keepdims=True) + EPS)


def swiglu_half(x, g_pre, w_gate, w_up, w_down, g_post):
    h = rmsnorm(x, g_pre)
    y = (jax.nn.silu(h @ w_gate) * (h @ w_up)) @ w_down
    return x + 0.5 * rmsnorm(y, g_post)


def causal_conv(u, buf, w):
    T = u.shape[1]
    full = jnp.concatenate([buf.astype(u.dtype), u], axis=1)
    out = full[:, 0:T] * w[0]
    for i in range(1, CONV_WIDTH):
        out = out + full[:, i:i + T] * w[i]
    return jax.nn.silu(out), full[:, T:]


def mixer_inputs(h, conv_buf, w_in, b_f, conv_w, dt_bias, a_log):
    B, T, _ = h.shape
    p = h @ w_in
    idx = np.cumsum(SPLIT_SIZES)[:-1].tolist()
    qa, ka, va, fa, qkv_g, a_g, b_g, z, gate_a, gate_b = jnp.split(p, idx, axis=-1)
    qa = qa.reshape(B, T, FOX_HEADS, FOX_HEAD_DIM)
    ka = ka.reshape(B, T, FOX_HEADS, FOX_HEAD_DIM)
    va = va.reshape(B, T, FOX_HEADS, FOX_HEAD_DIM)
    logf = jax.nn.log_sigmoid(fa.astype(jnp.float32) + b_f.astype(jnp.float32))
    conv_out, new_buf = causal_conv(qkv_g, conv_buf, conv_w)
    qg, kg, vg = jnp.split(conv_out, 3, axis=-1)
    qg = l2norm(qg.reshape(B, T, GDN_HEADS, GDN_HEAD_DIM))
    kg = l2norm(kg.reshape(B, T, GDN_HEADS, GDN_HEAD_DIM))
    vg = vg.reshape(B, T, GDN_HEADS, GDN_HEAD_DIM).astype(jnp.float32)
    g = -jnp.exp(a_log.astype(jnp.float32)) * jax.nn.softplus(a_g.astype(jnp.float32) + dt_bias.astype(jnp.float32))
    beta = jax.nn.sigmoid(b_g.astype(jnp.float32))
    return qa, ka, va, logf, qg, kg, vg, g, beta, z, gate_a, gate_b, new_buf


def fox_block(q, cq, pos_q, k, v, ck, pos_k):
    s = jnp.einsum('bqhd,bkhd->bhqk', q, k).astype(jnp.float32) * (FOX_HEAD_DIM ** -0.5)
    s = s + jnp.swapaxes(cq, 1, 2)[..., :, None] - jnp.swapaxes(ck, 1, 2)[..., None, :]
    s = jnp.where(pos_k[None, :] <= pos_q[:, None], s, -jnp.inf)
    p = jax.nn.softmax(s, axis=-1)
    return jnp.einsum('bhqk,bkhd->bqhd', p.astype(v.dtype), v)


def fox_prompt(q, k, v, logf):
    B, S = q.shape[:2]
    c = jnp.cumsum(logf, axis=1)
    nb = S // Q_BLOCK
    qb = jnp.swapaxes(q.reshape(B, nb, Q_BLOCK, FOX_HEADS, FOX_HEAD_DIM), 0, 1)
    cb = jnp.swapaxes(c.reshape(B, nb, Q_BLOCK, FOX_HEADS), 0, 1)
    pos = jnp.arange(S)

    def one_block(args):
        i, q_i, c_i = args
        return fox_block(q_i, c_i, i * Q_BLOCK + jnp.arange(Q_BLOCK), k, v, c, pos)

    o = lax.map(one_block, (jnp.arange(nb), qb, cb))
    return jnp.swapaxes(o, 0, 1).reshape(B, S, FOX_HEADS, FOX_HEAD_DIM)


def fox_sample(q, k, v, logf, cache_k, cache_v, cache_logf, page_table):
    DB, T = q.shape[:2]
    past_k = cache_k[page_table].reshape(DB, -1, FOX_HEADS, FOX_HEAD_DIM)
    past_v = cache_v[page_table].reshape(DB, -1, FOX_HEADS, FOX_HEAD_DIM)
    past_lf = cache_logf[page_table].reshape(DB, -1, FOX_HEADS).astype(jnp.float32)
    P = past_k.shape[1]
    k_all = jnp.concatenate([past_k.astype(k.dtype), k], axis=1)
    v_all = jnp.concatenate([past_v.astype(v.dtype), v], axis=1)
    c = jnp.cumsum(jnp.concatenate([past_lf, logf], axis=1), axis=1)
    return fox_block(q, c[:, P:], P + jnp.arange(T), k_all, v_all, c, jnp.arange(P + T))


def gdn_chunked(q, k, v, g, beta, S0):
    B, T, H, dk = q.shape
    dv = v.shape[-1]
    n = T // CHUNK

    def to_chunks(a):
        a = jnp.moveaxis(a.astype(jnp.float32), 2, 1)
        return a.reshape((B, H, n, CHUNK) + a.shape[3:])

    qc = to_chunks(q) * (dk ** -0.5)
    kc, vc, bc = to_chunks(k), to_chunks(v), to_chunks(beta)
    gc = jnp.cumsum(to_chunks(g), axis=-1)
    tri = jnp.tril(jnp.ones((CHUNK, CHUNK), bool))
    diff = gc[..., :, None] - gc[..., None, :]
    decay = jnp.where(tri, jnp.exp(jnp.where(tri, diff, 0.0)), 0.0)
    kb = kc * bc[..., None]
    lower = jnp.einsum('bhncd,bhnsd->bhncs', kb, kc) * decay
    rhs = jnp.concatenate([vc * bc[..., None], kb * jnp.exp(gc)[..., None]], axis=-1)
    sol = lax.linalg.triangular_solve(lower, rhs, left_side=True, lower=True, unit_diagonal=True)
    u, w = sol[..., :dv], sol[..., dv:]
    attn = jnp.einsum('bhncd,bhnsd->bhncs', qc, kc) * decay

    def step(S, xs):
        q_i, k_i, u_i, w_i, g_i, a_i = xs
        v_new = u_i - jnp.einsum('bhcd,bhde->bhce', w_i, S)
        o = jnp.einsum('bhcd,bhde->bhce', q_i * jnp.exp(g_i)[..., None], S) + jnp.einsum('bhcs,bhse->bhce', a_i, v_new)
        g_last = g_i[..., -1]
        S = S * jnp.exp(g_last)[..., None, None] + jnp.einsum('bhcd,bhce->bhde', k_i * jnp.exp(g_last[..., None] - g_i)[..., None], v_new)
        return S, o

    xs = tuple(jnp.moveaxis(a, 2, 0) for a in (qc, kc, u, w, gc, attn))
    S, o = lax.scan(step, S0.astype(jnp.float32), xs)
    return jnp.transpose(o, (1, 0, 3, 2, 4)).reshape(B, T, H, dv), S


def gdn_recurrent(q, k, v, g, beta, S0):
    xs = tuple(jnp.moveaxis(a.astype(jnp.float32), 1, 0) for a in (q * (GDN_HEAD_DIM ** -0.5), k, v, g, beta))

    def step(S, xs_t):
        q_t, k_t, v_t, g_t, b_t = xs_t
        S = S * jnp.exp(g_t)[..., None, None]
        delta = (v_t - jnp.einsum('bhd,bhde->bhe', k_t, S)) * b_t[..., None]
        S = S + jnp.einsum('bhd,bhe->bhde', k_t, delta)
        return S, jnp.einsum('bhd,bhde->bhe', q_t, S)

    S, o = lax.scan(step, S0.astype(jnp.float32), xs)
    return jnp.moveaxis(o, 0, 1), S


def token_mixing(x, fox_cache, page_table, conv_buf, S0, mix):
    (g_pre, w_in, b_f, conv_w, dt_bias, a_log, g_gdn, w_branch_fox, w_branch_gdn, w_out, g_post) = mix
    B, T, _ = x.shape
    h = rmsnorm(x, g_pre)
    qa, ka, va, logf, qg, kg, vg, g, beta, z, gate_a, gate_b, new_buf = mixer_inputs(h, conv_buf, w_in, b_f, conv_w, dt_bias, a_log)
    if fox_cache is None:
        o_a = fox_prompt(qa, ka, va, logf)
        o_g, S = gdn_chunked(qg, kg, vg, g, beta, S0)
    else:
        cache_k, cache_v, cache_logf = fox_cache
        o_a = fox_sample(qa, ka, va, logf, cache_k, cache_v, cache_logf, page_table)
        o_g, S = gdn_recurrent(qg, kg, vg, g, beta, S0)
    o_g = rmsnorm(o_g.astype(x.dtype), g_gdn) * jax.nn.silu(z.reshape(B, T, GDN_HEADS, GDN_HEAD_DIM))
    y_a = o_a.reshape(B, T, FOX_WIDTH) @ w_branch_fox
    y_g = o_g.reshape(B, T, GDN_WIDTH) @ w_branch_gdn
    y = (jax.nn.sigmoid(gate_a) * y_a + jax.nn.sigmoid(gate_b) * y_g) @ w_out
    x = x + rmsnorm(y, g_post)
    return x, (ka, va, logf.astype(x.dtype), S.astype(S0.dtype), new_buf)


def setup_inputs(seed: int = 0) -> dict:
    key = jax.random.key(seed)
    keys = iter(jax.random.split(key, 40))
    f32 = jnp.float32
    n_pages = PAST_LEN // PAGE_SIZE
    n_phys = (DEC_BATCH * n_pages * 5) // 4

    def nrm(shape, scale=1.0):
        return scale * jax.random.normal(next(keys), shape, f32)

    def gain(n):
        return 1.0 + 0.02 * nrm((DEPTH, n))

    perm = jax.random.permutation(next(keys), n_phys)[: DEC_BATCH * n_pages]
    page_table = perm.reshape(DEC_BATCH, n_pages).astype(jnp.int32)
    dt = jnp.exp(jax.random.uniform(next(keys), (DEPTH, GDN_HEADS), f32, math.log(1e-3), math.log(1e-1)))
    dt_bias = dt + jnp.log(-jnp.expm1(-dt))
    a_log = jnp.log(jax.random.uniform(next(keys), (DEPTH, GDN_HEADS), f32, 1.0, 16.0))
    return {
        "x_prompt": nrm((BATCH, SEQ, D_MODEL)),
        "x_sample": nrm((DEC_BATCH, DEC_SEQ, D_MODEL)),
        "cache_k": nrm((DEPTH, n_phys, PAGE_SIZE, FOX_HEADS, FOX_HEAD_DIM)),
        "cache_v": nrm((DEPTH, n_phys, PAGE_SIZE, FOX_HEADS, FOX_HEAD_DIM)),
        "cache_logf": jax.nn.log_sigmoid(FORGET_BIAS_INIT + nrm((DEPTH, n_phys, PAGE_SIZE, FOX_HEADS))),
        "state_gdn": nrm((DEPTH, DEC_BATCH, GDN_HEADS, GDN_HEAD_DIM, GDN_HEAD_DIM), GDN_HEAD_DIM ** -0.5),
        "state_conv": nrm((DEPTH, DEC_BATCH, CONV_WIDTH - 1, CONV_CH)),
        "page_table": page_table,
        "ffn1_norm_pre": gain(D_MODEL),
        "ffn1_w_gate": nrm((DEPTH, D_MODEL, D_FF), D_MODEL ** -0.5),
        "ffn1_w_up": nrm((DEPTH, D_MODEL, D_FF), D_MODEL ** -0.5),
        "ffn1_w_down": nrm((DEPTH, D_FF, D_MODEL), D_FF ** -0.5),
        "ffn1_norm_post": gain(D_MODEL),
        "mix_norm_pre": gain(D_MODEL),
        "w_in": nrm((DEPTH, D_MODEL, P_IN), D_MODEL ** -0.5),
        "fox_forget_bias": FORGET_BIAS_INIT + nrm((DEPTH, FOX_HEADS), 0.1),
        "gdn_conv_w": nrm((DEPTH, CONV_WIDTH, CONV_CH), CONV_WIDTH ** -0.5),
        "gdn_dt_bias": dt_bias,
        "gdn_a_log": a_log,
        "gdn_out_norm": gain(GDN_HEAD_DIM),
        "w_branch_fox": nrm((DEPTH, FOX_WIDTH, D_MODEL), FOX_WIDTH ** -0.5),
        "w_branch_gdn": nrm((DEPTH, GDN_WIDTH, D_MODEL), GDN_WIDTH ** -0.5),
        "w_out": nrm((DEPTH, D_MODEL, D_MODEL), D_MODEL ** -0.5),
        "mix_norm_post": gain(D_MODEL),
        "ffn2_norm_pre": gain(D_MODEL),
        "ffn2_w_gate": nrm((DEPTH, D_MODEL, D_FF), D_MODEL ** -0.5),
        "ffn2_w_up": nrm((DEPTH, D_MODEL, D_FF), D_MODEL ** -0.5),
        "ffn2_w_down": nrm((DEPTH, D_FF, D_MODEL), D_FF ** -0.5),
        "ffn2_norm_post": gain(D_MODEL),
    }


def stack_field(states, i):
    return jnp.stack([s[i] for s in states], axis=0)


def reference(x_prompt, x_sample, cache_k, cache_v, cache_logf, state_gdn, state_conv, page_table,
              ffn1_norm_pre, ffn1_w_gate, ffn1_w_up, ffn1_w_down, ffn1_norm_post,
              mix_norm_pre, w_in, fox_forget_bias, gdn_conv_w, gdn_dt_bias, gdn_a_log, gdn_out_norm,
              w_branch_fox, w_branch_gdn, w_out, mix_norm_post,
              ffn2_norm_pre, ffn2_w_gate, ffn2_w_up, ffn2_w_down, ffn2_norm_post):
    xp, xs = x_prompt, x_sample
    st_p, st_s = [], []
    for l in range(DEPTH):
        ffn1 = (ffn1_norm_pre[l], ffn1_w_gate[l], ffn1_w_up[l], ffn1_w_down[l], ffn1_norm_post[l])
        ffn2 = (ffn2_norm_pre[l], ffn2_w_gate[l], ffn2_w_up[l], ffn2_w_down[l], ffn2_norm_post[l])
        mix = (mix_norm_pre[l], w_in[l], fox_forget_bias[l], gdn_conv_w[l], gdn_dt_bias[l], gdn_a_log[l],
               gdn_out_norm[l], w_branch_fox[l], w_branch_gdn[l], w_out[l], mix_norm_post[l])
        xp = swiglu_half(xp, *ffn1)
        buf0 = jnp.zeros((xp.shape[0], CONV_WIDTH - 1, CONV_CH), xp.dtype)
        s0 = jnp.zeros((xp.shape[0], GDN_HEADS, GDN_HEAD_DIM, GDN_HEAD_DIM), xp.dtype)
        xp, sp = token_mixing(xp, None, None, buf0, s0, mix)
        xp = swiglu_half(xp, *ffn2)
        xs = swiglu_half(xs, *ffn1)
        xs, ss = token_mixing(xs, (cache_k[l], cache_v[l], cache_logf[l]), page_table, state_conv[l], state_gdn[l], mix)
        xs = swiglu_half(xs, *ffn2)
        st_p.append(sp)
        st_s.append(ss)
    k_p, v_p, lf_p, S_p, buf_p = (stack_field(st_p, 0), stack_field(st_p, 1), stack_field(st_p, 2), stack_field(st_p, 3), stack_field(st_p, 4))
    k_s, v_s, lf_s, S_s, buf_s = (stack_field(st_s, 0), stack_field(st_s, 1), stack_field(st_s, 2), stack_field(st_s, 3), stack_field(st_s, 4))
    return (xp, xs, k_p, v_p, lf_p, S_p, buf_p, k_s, v_s, lf_s, S_s, buf_s)
```

```python
import functools

import jax
import jax.numpy as jnp
from jax import lax
from jax.experimental import pallas as pl
from jax.experimental.pallas import tpu as pltpu

F32 = jnp.float32
BF16 = jnp.bfloat16
EPS = 1e-6
LANES = 128
VMEM_LIMIT = 56 * 1024 * 1024

FOX_HEADS = 8
FOX_HEAD_DIM = 64
FOX_WIDTH = FOX_HEADS * FOX_HEAD_DIM
GDN_HEADS = 4
GDN_HEAD_DIM = 128
GDN_WIDTH = GDN_HEADS * GDN_HEAD_DIM
CONV_WIDTH = 4
CONV_CH = 3 * GDN_WIDTH
CHUNK = 64
FF_CHUNK = 256


def _rms(x, g):
    return x * lax.rsqrt(jnp.mean(x * x, axis=-1, keepdims=True) + EPS) * g


def _sigmoid(x):
    return 1.0 / (1.0 + jnp.exp(-x))


def _silu(x):
    return x * _sigmoid(x)


def _const_spec(shape):
    n = len(shape)
    return pl.BlockSpec(shape, lambda *_: (0,) * n, pipeline_mode=pl.Buffered(1))


def _ffn_body(x_ref, gpre_ref, wg_ref, wu_ref, wd_ref, gpost_ref, o_ref):
    x = x_ref[...]
    h = _rms(x, gpre_ref[...]).astype(BF16)
    n_chunks = wg_ref.shape[0]

    def chunk(j, acc):
        g = jnp.dot(h, wg_ref[j], preferred_element_type=F32)
        u = jnp.dot(h, wu_ref[j], preferred_element_type=F32)
        a = (_silu(g) * u).astype(BF16)
        return acc + jnp.dot(a, wd_ref[j], preferred_element_type=F32)

    y = lax.fori_loop(0, n_chunks, chunk, jnp.zeros(x.shape, F32))
    o_ref[...] = x + 0.5 * _rms(y, gpost_ref[...])


def _ffn(x, g_pre, wg, wu, wd, g_post, tm):
    n, d = x.shape
    row = pl.BlockSpec((tm, d), lambda i: (i, 0))
    return pl.pallas_call(
        _ffn_body,
        grid=(pl.cdiv(n, tm),),
        in_specs=[row, _const_spec((1, d)), _const_spec(wg.shape), _const_spec(wu.shape),
                  _const_spec(wd.shape), _const_spec((1, d))],
        out_specs=row,
        out_shape=jax.ShapeDtypeStruct((n, d), F32),
        compiler_params=pltpu.CompilerParams(dimension_semantics=("arbitrary",), vmem_limit_bytes=VMEM_LIMIT),
        name="ffn",
    )(x, g_pre, wg, wu, wd, g_post)


def _ffn_weights(w_gate, w_up, w_down):
    d, ff = w_gate.shape
    nc = ff // FF_CHUNK
    wg = w_gate.astype(BF16).reshape(d, nc, FF_CHUNK).transpose(1, 0, 2)
    wu = w_up.astype(BF16).reshape(d, nc, FF_CHUNK).transpose(1, 0, 2)
    wd = w_down.astype(BF16).reshape(nc, FF_CHUNK, d)
    return wg, wu, wd


def _split3(x):
    hi = x.astype(BF16)
    r = x - hi.astype(F32)
    mid = r.astype(BF16)
    lo = (r - mid.astype(F32)).astype(BF16)
    return hi, mid, lo


def _dot3(tri_bf16, x):
    hi, mid, lo = _split3(x)
    d = functools.partial(jnp.dot, tri_bf16, preferred_element_type=F32)
    return d(hi) + d(mid) + d(lo)


def _softplus(x):
    return jnp.maximum(x, 0.0) + jnp.log(1.0 + jnp.exp(-jnp.abs(x)))


def _small_cols(small, bias_ref, aneg_ref):
    lane = lax.broadcasted_iota(jnp.int32, small.shape, 1)
    pre = small + bias_ref[...]
    sp = _softplus(pre)
    logf = pre - sp
    g = aneg_ref[...] * sp
    beta = _sigmoid(small)
    gb = jnp.where(lane < FOX_HEADS + GDN_HEADS, g, beta)
    return logf, gb


def _l2n_heads(x):
    outs = []
    for h in range(x.shape[1] // GDN_HEAD_DIM):
        xh = x[:, h * GDN_HEAD_DIM:(h + 1) * GDN_HEAD_DIM]
        outs.append(xh * lax.rsqrt(jnp.sum(xh * xh, axis=-1, keepdims=True) + EPS))
    return jnp.concatenate(outs, axis=-1)


def _proj_prompt_body(x_ref, gpre_ref, wq_ref, wkT_ref, wvT_ref, ws_ref, wc_ref, wz_ref, wga_ref, wgb_ref,
                      bias_ref, aneg_ref, convw_ref, tri_ref,
                      q_ref, kT_ref, vT_ref, lfT_ref, cT_ref, gbT_ref, col_ref, qkv_ref, cst_ref,
                      z_ref, ga_ref, gb_ref, ubuf, ccarry, *, tiles_per_seq):
    i = pl.program_id(0)
    tm = x_ref.shape[0]

    @pl.when(i % tiles_per_seq == 0)
    def _():
        ubuf[0:8, :] = jnp.zeros((8, ubuf.shape[1]), F32)
        ccarry[...] = jnp.zeros(ccarry.shape, F32)

    h = _rms(x_ref[...], gpre_ref[...]).astype(BF16)
    dot = functools.partial(jnp.dot, h, preferred_element_type=F32)
    nt = (((1,), (1,)), ((), ()))

    q_ref[...] = (dot(wq_ref[...]) * (FOX_HEAD_DIM ** -0.5)).astype(BF16)
    kT_ref[0] = lax.dot_general(wkT_ref[...], h, nt, preferred_element_type=F32)
    vT_ref[0] = lax.dot_general(wvT_ref[...], h, nt, preferred_element_type=F32)
    z_ref[...] = dot(wz_ref[...])
    ga_ref[...] = _sigmoid(dot(wga_ref[...]))
    gb_ref[...] = _sigmoid(dot(wgb_ref[...]))

    logf, gb = _small_cols(dot(ws_ref[...]), bias_ref, aneg_ref)
    carry = ccarry[...]
    blocks = []
    for r in range(tm // LANES):
        cb = _dot3(tri_ref[...], logf[r * LANES:(r + 1) * LANES, :]) + carry
        carry = cb[LANES - 1:LANES, :]
        blocks.append(cb)
    ccarry[...] = carry
    c = jnp.concatenate(blocks, axis=0)
    lane = lax.broadcasted_iota(jnp.int32, c.shape, 1)
    col = jnp.where(lane < FOX_HEADS, c, gb)
    col_ref[...] = col
    colT = col.T
    lfT_ref[0] = logf.T[0:FOX_HEADS, :]
    cT_ref[0] = colT[0:FOX_HEADS, :]
    gbT_ref[0] = colT[FOX_HEADS:2 * FOX_HEADS, :]

    ubuf[8:8 + tm, :] = dot(wc_ref[...])
    base = 8 - (CONV_WIDTH - 1)
    conv = ubuf[base:base + tm, :] * convw_ref[0:1, :]
    for k in range(1, CONV_WIDTH):
        conv = conv + ubuf[base + k:base + k + tm, :] * convw_ref[k:k + 1, :]
    cst_ref[0] = ubuf[tm + base:tm + 8, :]
    ubuf[0:8, :] = ubuf[tm:tm + 8, :]
    conv = _silu(conv)
    qkv_ref[:, 0:2 * GDN_WIDTH] = _l2n_heads(conv[:, 0:2 * GDN_WIDTH])
    qkv_ref[:, 2 * GDN_WIDTH:] = conv[:, 2 * GDN_WIDTH:]


def _proj_prompt(x, g_pre, w, batch, seq, tm):
    n, d = x.shape
    tps = seq // tm
    row = lambda width: pl.BlockSpec((tm, width), lambda i: (i, 0))
    seqT = lambda rows: pl.BlockSpec((1, rows, tm), lambda i: (i // tps, 0, i % tps))
    consts = [g_pre, w["q"], w["kT"], w["vT"], w["small"], w["conv"], w["z"], w["ga"], w["gb"],
              w["bias"], w["aneg"], w["convw"], w["tri"]]
    out_shape = [
        jax.ShapeDtypeStruct((n, FOX_WIDTH), BF16),
        jax.ShapeDtypeStruct((batch, FOX_WIDTH, seq), F32),
        jax.ShapeDtypeStruct((batch, FOX_WIDTH, seq), F32),
        jax.ShapeDtypeStruct((batch, FOX_HEADS, seq), F32),
        jax.ShapeDtypeStruct((batch, FOX_HEADS, seq), F32),
        jax.ShapeDtypeStruct((batch, 2 * GDN_HEADS, seq), F32),
        jax.ShapeDtypeStruct((n, LANES), F32),
        jax.ShapeDtypeStruct((n, CONV_CH), F32),
        jax.ShapeDtypeStruct((batch, CONV_WIDTH - 1, CONV_CH), F32),
        jax.ShapeDtypeStruct((n, GDN_WIDTH), F32),
        jax.ShapeDtypeStruct((n, d), F32),
        jax.ShapeDtypeStruct((n, d), F32),
    ]
    out_specs = [row(FOX_WIDTH), seqT(FOX_WIDTH), seqT(FOX_WIDTH), seqT(FOX_HEADS), seqT(FOX_HEADS),
                 seqT(2 * GDN_HEADS), row(LANES), row(CONV_CH),
                 pl.BlockSpec((1, CONV_WIDTH - 1, CONV_CH), lambda i: (i // tps, 0, 0)),
                 row(GDN_WIDTH), row(d), row(d)]
    return pl.pallas_call(
        functools.partial(_proj_prompt_body, tiles_per_seq=tps),
        grid=(n // tm,),
        in_specs=[row(d)] + [_const_spec(a.shape) for a in consts],
        out_specs=out_specs,
        out_shape=out_shape,
        scratch_shapes=[pltpu.VMEM((tm + 8, CONV_CH), F32), pltpu.VMEM((1, LANES), F32)],
        compiler_params=pltpu.CompilerParams(dimension_semantics=("arbitrary",), vmem_limit_bytes=VMEM_LIMIT),
        name="proj_prompt",
    )(x, *consts)


def _mixer_weights(w_in, b_f, conv_w, dt_bias, a_log):
    sizes = (FOX_WIDTH, FOX_WIDTH, FOX_WIDTH, FOX_HEADS, CONV_CH, GDN_HEADS, GDN_HEADS, GDN_WIDTH)
    d = w_in.shape[0]
    offs = [0]
    for s in sizes:
        offs.append(offs[-1] + s)
    part = lambda k: w_in[:, offs[k]:offs[k + 1]]
    ga0 = offs[-1]
    n_small = FOX_HEADS + 2 * GDN_HEADS
    small = jnp.concatenate([part(3), part(5), part(6), jnp.zeros((d, LANES - n_small), w_in.dtype)], axis=1)
    pad = lambda v, lo: jnp.zeros((1, LANES), F32).at[0, lo:lo + v.shape[0]].set(v.astype(F32))
    r = lax.broadcasted_iota(jnp.int32, (LANES, LANES), 0)
    c = lax.broadcasted_iota(jnp.int32, (LANES, LANES), 1)
    return {
        "q": part(0).astype(BF16), "kT": part(1).T.astype(BF16), "vT": part(2).T.astype(BF16),
        "small": small.astype(BF16), "conv": part(4).astype(BF16), "z": part(7).astype(BF16),
        "ga": w_in[:, ga0:ga0 + d].astype(BF16), "gb": w_in[:, ga0 + d:ga0 + 2 * d].astype(BF16),
        "bias": pad(b_f, 0) + pad(dt_bias, FOX_HEADS),
        "aneg": pad(-jnp.exp(a_log.astype(F32)), FOX_HEADS),
        "convw": conv_w.astype(F32),
        "tri": (c <= r).astype(BF16),
    }


NEG_BIG = -1e30


def _fox_prompt_body(q_ref, col_ref, kT_ref, vT_ref, cT_ref, o_ref, kb_ref, vb_ref, *, tq):
    p = pl.program_id(1)
    qi = pl.program_id(2)

    @pl.when(qi == 0)
    def _():
        kb_ref[...] = kT_ref[0].astype(BF16)
        vb_ref[...] = vT_ref[0].astype(BF16)

    nt = (((1,), (1,)), ((), ()))
    col = col_ref[...]
    lane = lax.broadcasted_iota(jnp.int32, col.shape, 1)
    outs = []
    for e in range(2):
        head = 2 * p + e
        rows = slice(e * FOX_HEAD_DIM, (e + 1) * FOX_HEAD_DIM)
        q = q_ref[:, rows]
        cq = jnp.sum(jnp.where(lane == head, col, 0.0), axis=-1, keepdims=True)

        def scores(kb):
            ks = pl.ds(pl.multiple_of(kb * tq, tq), tq)
            s = jnp.dot(q, kb_ref[rows, ks], preferred_element_type=F32)
            return s + cq - cT_ref[0, pl.ds(head, 1), ks], ks

        def update(s, ks, carry):
            m, l, acc = carry
            m_new = jnp.maximum(m, jnp.max(s, axis=-1, keepdims=True))
            pr = jnp.exp(s - m_new)
            alpha = jnp.exp(m - m_new)
            l = alpha * l + jnp.sum(pr, axis=-1, keepdims=True)
            pv = lax.dot_general(pr.astype(BF16), vb_ref[rows, ks], nt, preferred_element_type=F32)
            return m_new, l, alpha * acc + pv

        def full_block(kb, carry):
            s, ks = scores(kb)
            return update(s, ks, carry)

        init = (jnp.full((tq, 1), NEG_BIG, F32), jnp.zeros((tq, 1), F32), jnp.zeros((tq, FOX_HEAD_DIM), F32))
        carry = lax.fori_loop(0, qi, full_block, init)
        s, ks = scores(qi)
        r = lax.broadcasted_iota(jnp.int32, s.shape, 0)
        c = lax.broadcasted_iota(jnp.int32, s.shape, 1)
        _, l, acc = update(jnp.where(c <= r, s, NEG_BIG), ks, carry)
        outs.append(acc / l)
    o_ref[...] = jnp.concatenate(outs, axis=-1).astype(o_ref.dtype)


def _proj_sample_body(x_ref, gpre_ref, wqT_ref, wkT_ref, wvT_ref, ws_ref, wc_ref, wcT_ref, wz_ref, wga_ref, wgb_ref,
                      bias_ref, aneg_ref, convw_ref, convwT_ref, buf_ref, bufT_ref,
                      qT_ref, kT_ref, vT_ref, small_ref, smallT_ref, u_ref, qkT_ref, vg_ref, z_ref, ga_ref, gb_ref):
    h = _rms(x_ref[...], gpre_ref[...]).astype(BF16)
    dot = functools.partial(jnp.dot, h, preferred_element_type=F32)
    dot_t = lambda w: lax.dot_general(w, h, (((1,), (1,)), ((), ())), preferred_element_type=F32)
    qT_ref[...] = dot_t(wqT_ref[...]) * (FOX_HEAD_DIM ** -0.5)
    kT_ref[...] = dot_t(wkT_ref[...])
    vT_ref[...] = dot_t(wvT_ref[...])
    z_ref[...] = dot(wz_ref[...])
    ga_ref[...] = _sigmoid(dot(wga_ref[...]))
    gb_ref[...] = _sigmoid(dot(wgb_ref[...]))

    logf, gb = _small_cols(dot(ws_ref[...]), bias_ref, aneg_ref)
    lane = lax.broadcasted_iota(jnp.int32, logf.shape, 1)
    small = jnp.where(lane < FOX_HEADS, logf, gb)
    small_ref[...] = small
    pad = jnp.concatenate([small, jnp.zeros((LANES - small.shape[0], LANES), F32)], axis=0)
    smallT_ref[...] = pad.T[:, 0:small.shape[0]]

    u = dot(wc_ref[...])
    u_ref[...] = u
    vs = slice(2 * GDN_WIDTH, CONV_CH)
    conv_v = u[:, vs] * convw_ref[CONV_WIDTH - 1:CONV_WIDTH, vs]
    conv_qk = dot_t(wcT_ref[...]) * convwT_ref[:, CONV_WIDTH - 1:CONV_WIDTH]
    for k in range(CONV_WIDTH - 1):
        conv_v = conv_v + buf_ref[k][:, vs] * convw_ref[k:k + 1, vs]
        conv_qk = conv_qk + bufT_ref[k] * convwT_ref[:, k:k + 1]
    vg_ref[...] = _silu(conv_v)
    conv_qk = _silu(conv_qk)
    for hd in range(2 * GDN_HEADS):
        rs = slice(hd * GDN_HEAD_DIM, (hd + 1) * GDN_HEAD_DIM)
        xh = conv_qk[rs, :]
        qkT_ref[rs, :] = xh * lax.rsqrt(jnp.sum(xh * xh, axis=0, keepdims=True) + EPS)


def _proj_sample(x, g_pre, w, buf, bufT):
    db, d = x.shape
    ins = [x, g_pre, w["q"].T, w["kT"], w["vT"], w["small"], w["conv"], w["conv"][:, :2 * GDN_WIDTH].T, w["z"],
           w["ga"], w["gb"], w["bias"], w["aneg"], w["convw"], w["convw"][:, :2 * GDN_WIDTH].T, buf, bufT]
    out_shape = [
        jax.ShapeDtypeStruct((FOX_WIDTH, db), F32), jax.ShapeDtypeStruct((FOX_WIDTH, db), F32),
        jax.ShapeDtypeStruct((FOX_WIDTH, db), F32),
        jax.ShapeDtypeStruct((db, LANES), F32), jax.ShapeDtypeStruct((LANES, db), F32),
        jax.ShapeDtypeStruct((db, CONV_CH), F32), jax.ShapeDtypeStruct((2 * GDN_WIDTH, db), F32),
        jax.ShapeDtypeStruct((db, GDN_WIDTH), F32), jax.ShapeDtypeStruct((db, GDN_WIDTH), F32),
        jax.ShapeDtypeStruct((db, d), F32), jax.ShapeDtypeStruct((db, d), F32),
    ]
    return pl.pallas_call(
        _proj_sample_body, out_shape=out_shape,
        compiler_params=pltpu.CompilerParams(vmem_limit_bytes=VMEM_LIMIT),
        name="proj_sample",
    )(*ins)


def _take_lane(x, idx):
    lane = lax.broadcasted_iota(jnp.int32, x.shape, 1)
    return jnp.sum(jnp.where(lane == idx, x, 0.0), axis=-1, keepdims=True)


PAGES_PER_STEP = 8


def _fox_decode_body(pt_ref, qT_ref, kTn_ref, vTn_ref, smallT_ref, *rest, page):
    g_pages = PAGES_PER_STEP
    k_refs = rest[0:g_pages]
    v_refs = rest[g_pages:2 * g_pages]
    lf_refs = rest[2 * g_pages:3 * g_pages]
    o_ref, qb_s, acc_s, m_s, l_s, sfx_s = rest[3 * g_pages:]
    b = pl.program_id(0)
    j = pl.program_id(1)
    hd = FOX_HEAD_DIM
    lane = lax.broadcasted_iota(jnp.int32, (FOX_HEADS, page), 1)

    def per_head_dot(a, bm):
        return jnp.concatenate(
            [jnp.sum(a[h * hd:(h + 1) * hd, :] * bm[h * hd:(h + 1) * hd, :], axis=0, keepdims=True)
             for h in range(FOX_HEADS)], axis=0)

    @pl.when(j == 0)
    def _():
        qb = jnp.broadcast_to(_take_lane(qT_ref[...], b), (FOX_WIDTH, page))
        kn = jnp.broadcast_to(_take_lane(kTn_ref[...], b), (FOX_WIDTH, page))
        vn = jnp.broadcast_to(_take_lane(vTn_ref[...], b), (FOX_WIDTH, page))
        qb_s[...] = qb
        m_s[...] = per_head_dot(qb, kn)
        l_s[...] = jnp.ones(l_s.shape, F32)
        lane_w = lax.broadcasted_iota(jnp.int32, (FOX_WIDTH, page), 1)
        acc_s[...] = jnp.where(lane_w == 0, vn, 0.0)
        sfx_s[...] = jnp.broadcast_to(_take_lane(smallT_ref[0:FOX_HEADS, :], b), sfx_s.shape)

    carry = sfx_s[...]
    scores = []
    for g in range(g_pages):
        lf = lf_refs[g][0]
        incl = lf
        sh = 1
        while sh < page:
            incl = incl + jnp.where(lane + sh < page, pltpu.roll(incl, page - sh, axis=1), 0.0)
            sh *= 2
        bias = carry + (incl - lf)
        carry = carry + incl[:, 0:1]
        kt = k_refs[g][0].reshape(FOX_WIDTH, page)
        scores.append(per_head_dot(kt, qb_s[...]) + bias)
    sfx_s[...] = carry

    m_old = m_s[...]
    m_new = m_old
    for s in scores:
        m_new = jnp.maximum(m_new, jnp.max(s, axis=-1, keepdims=True))
    alpha = jnp.exp(m_old - m_new)
    probs = [jnp.exp(s - m_new) for s in scores]
    l_new = alpha * l_s[...]
    for p in probs:
        l_new = l_new + jnp.sum(p, axis=-1, keepdims=True)
    m_s[...] = m_new
    l_s[...] = l_new
    for h in range(FOX_HEADS):
        rs = slice(h * hd, (h + 1) * hd)
        acc = acc_s[rs, :] * alpha[h:h + 1, 0:1]
        for g in range(g_pages):
            acc = acc + probs[g][h:h + 1, :] * v_refs[g][0, h]
        acc_s[rs, :] = acc

    @pl.when(j == pl.num_programs(1) - 1)
    def _():
        inv_l = 1.0 / l_s[...]
        for h in range(FOX_HEADS):
            rs = slice(h * hd, (h + 1) * hd)
            o_ref[0, rs, :] = jnp.sum(acc_s[rs, :], axis=-1, keepdims=True) * inv_l[h:h + 1, 0:1]


def _fox_decode(page_table, qT, kTn, vTn, smallT, cache_kT, cache_vT, cache_lfT):
    db, n_pages = page_table.shape
    page = cache_kT.shape[-1]
    g_pages = PAGES_PER_STEP
    n_steps = n_pages // g_pages
    const = lambda shape: pl.BlockSpec(shape, lambda b, j, pt: (0,) * len(shape))

    def page_idx(b, j, pt, g):
        return pt[b * n_pages + (n_steps - 1 - j) * g_pages + (g_pages - 1 - g)]

    kv_spec = lambda g: pl.BlockSpec((1, FOX_HEADS, FOX_HEAD_DIM, page),
                                     lambda b, j, pt: (page_idx(b, j, pt, g), 0, 0, 0))
    lf_spec = lambda g: pl.BlockSpec((1, FOX_HEADS, page), lambda b, j, pt: (page_idx(b, j, pt, g), 0, 0))
    in_specs = ([const(qT.shape), const(kTn.shape), const(vTn.shape), const(smallT.shape)]
                + [kv_spec(g) for g in range(g_pages)] + [kv_spec(g) for g in range(g_pages)]
                + [lf_spec(g) for g in range(g_pages)])
    out = pl.pallas_call(
        functools.partial(_fox_decode_body, page=page),
        grid_spec=pltpu.PrefetchScalarGridSpec(
            num_scalar_prefetch=1, grid=(db, n_steps), in_specs=in_specs,
            out_specs=pl.BlockSpec((1, FOX_WIDTH, 1), lambda b, j, pt: (b, 0, 0)),
            scratch_shapes=[pltpu.VMEM((FOX_WIDTH, page), F32), pltpu.VMEM((FOX_WIDTH, page), F32),
                            pltpu.VMEM((FOX_HEADS, page), F32), pltpu.VMEM((FOX_HEADS, page), F32),
                            pltpu.VMEM((FOX_HEADS, page), F32)]),
        out_shape=jax.ShapeDtypeStruct((db, FOX_WIDTH, 1), F32),
        compiler_params=pltpu.CompilerParams(dimension_semantics=("arbitrary",) * 2, vmem_limit_bytes=VMEM_LIMIT),
        name="fox_decode",
    )(page_table.reshape(-1), qT, kTn, vTn, smallT, *([cache_kT] * g_pages), *([cache_vT] * g_pages),
      *([cache_lfT] * g_pages))
    return out.reshape(db, FOX_WIDTH)


def _gdn_decode_body(qkT_ref, vg_ref, smallT_ref, s_ref, o_ref, snew_ref):
    b = pl.program_id(0)
    qk = _take_lane(qkT_ref[...], b)
    gbeta = _take_lane(smallT_ref[FOX_HEADS:FOX_HEADS + 2 * GDN_HEADS, :], b)
    for h in range(GDN_HEADS):
        rs = slice(h * GDN_HEAD_DIM, (h + 1) * GDN_HEAD_DIM)
        q = qk[rs, :] * (GDN_HEAD_DIM ** -0.5)
        k = qk[GDN_WIDTH + h * GDN_HEAD_DIM:GDN_WIDTH + (h + 1) * GDN_HEAD_DIM, :]
        v = vg_ref[0, :, rs]
        st = s_ref[0, h] * jnp.exp(gbeta[h:h + 1, :])
        delta = (v - jnp.sum(k * st, axis=0, keepdims=True)) * gbeta[GDN_HEADS + h:GDN_HEADS + h + 1, :]
        st = st + k * delta
        snew_ref[0, h] = st
        o_ref[0, :, rs] = jnp.sum(q * st, axis=0, keepdims=True)


def _gdn_decode(qkT, vg, smallT, state):
    db = vg.shape[0]
    const = lambda shape: pl.BlockSpec(shape, lambda b: (0,) * len(shape))
    st_spec = pl.BlockSpec((1, GDN_HEADS, GDN_HEAD_DIM, GDN_HEAD_DIM), lambda b: (b, 0, 0, 0))
    row_spec = pl.BlockSpec((1, 1, GDN_WIDTH), lambda b: (b, 0, 0))
    o, s_new = pl.pallas_call(
        _gdn_decode_body,
        grid=(db,),
        in_specs=[const(qkT.shape), row_spec, const(smallT.shape), st_spec],
        out_specs=[row_spec, st_spec],
        out_shape=[jax.ShapeDtypeStruct((db, 1, GDN_WIDTH), F32), jax.ShapeDtypeStruct(state.shape, F32)],
        compiler_params=pltpu.CompilerParams(dimension_semantics=("arbitrary",)),
        name="gdn_decode",
    )(qkT, vg.reshape(db, 1, GDN_WIDTH), smallT, state)
    return o.reshape(db, GDN_WIDTH), s_new


def _mix_out_body(x_ref, oa_ref, og_ref, z_ref, ga_ref, gb_ref, ggdn_ref, wbf_ref, wbg_ref, wout_ref, gpost_ref,
                  o_ref):
    og = og_ref[...]
    z = z_ref[...]
    parts = []
    for h in range(GDN_HEADS):
        ls = slice(h * GDN_HEAD_DIM, (h + 1) * GDN_HEAD_DIM)
        parts.append(_rms(og[:, ls], ggdn_ref[...]) * _silu(z[:, ls]))
    ogn = jnp.concatenate(parts, axis=-1).astype(BF16)
    y_a = jnp.dot(oa_ref[...], wbf_ref[...], preferred_element_type=F32)
    y_g = jnp.dot(ogn, wbg_ref[...], preferred_element_type=F32)
    y = (ga_ref[...] * y_a + gb_ref[...] * y_g).astype(BF16)
    y = jnp.dot(y, wout_ref[...], preferred_element_type=F32)
    o_ref[...] = x_ref[...] + _rms(y, gpost_ref[...])


def _mix_out(x, o_a, o_g, z, ga, gb, g_gdn, wbf, wbg, wout, g_post, tm):
    n, d = x.shape
    row = lambda width: pl.BlockSpec((tm, width), lambda i: (i, 0))
    consts = [g_gdn, wbf, wbg, wout, g_post]
    return pl.pallas_call(
        _mix_out_body,
        grid=(pl.cdiv(n, tm),),
        in_specs=[row(d), row(FOX_WIDTH), row(GDN_WIDTH), row(GDN_WIDTH), row(d), row(d)]
        + [_const_spec(a.shape) for a in consts],
        out_specs=row(d),
        out_shape=jax.ShapeDtypeStruct((n, d), F32),
        compiler_params=pltpu.CompilerParams(dimension_semantics=("arbitrary",), vmem_limit_bytes=VMEM_LIMIT),
        name="mix_out",
    )(x, o_a, o_g, z, ga, gb, *consts)


def _unit_lower_inverse(low, rr, cc):
    hdot = functools.partial(jnp.dot, precision=lax.Precision.HIGHEST, preferred_element_type=F32)
    eye = (rr == cc).astype(F32)
    dg = jnp.where((rr // 16) == (cc // 16), low, 0.0)
    off = low - dg
    d2 = hdot(dg, dg)
    d4 = hdot(d2, d2)
    d8 = hdot(d4, d4)
    pinv = eye - dg
    pinv = pinv + hdot(pinv, d2)
    pinv = pinv + hdot(pinv, d4)
    pinv = pinv + hdot(pinv, d8)
    nn = hdot(pinv, off)
    qm = eye - nn
    qm = qm + hdot(qm, hdot(nn, nn))
    return hdot(qm, pinv)


def _gdn_prompt_body(q_ref, k_ref, v_ref, col_ref, tri_ref, o_ref, s_ref,
                     u_s, w_s, qe_s, ke_s, at_s, eg_s, *, heads_per_step):
    hg = pl.program_id(1)
    seq = q_ref.shape[0]
    n_chunks = seq // CHUNK
    nt = (((1,), (1,)), ((), ()))
    tn = (((0,), (0,)), ((), ()))
    hdot = functools.partial(jnp.dot, precision=lax.Precision.HIGHEST, preferred_element_type=F32)
    rr = lax.broadcasted_iota(jnp.int32, (CHUNK, CHUNK), 0)
    cc = lax.broadcasted_iota(jnp.int32, (CHUNK, CHUNK), 1)
    lane = lax.broadcasted_iota(jnp.int32, (CHUNK, LANES), 1)

    def prepare(n, _):
        rows = pl.ds(pl.multiple_of(n * CHUNK, CHUNK), CHUNK)
        colc = col_ref[rows, :]
        cs = _dot3(tri_ref[...], colc)
        for e in range(heads_per_step):
            head = hg * heads_per_step + e
            ls = slice(e * GDN_HEAD_DIM, (e + 1) * GDN_HEAD_DIM)
            gc = jnp.sum(jnp.where(lane == FOX_HEADS + head, cs, 0.0), axis=-1, keepdims=True)
            beta = jnp.sum(jnp.where(lane == FOX_HEADS + GDN_HEADS + head, colc, 0.0), axis=-1, keepdims=True)
            q = q_ref[rows, ls] * (GDN_HEAD_DIM ** -0.5)
            k = k_ref[rows, ls]
            v = v_ref[rows, ls]
            cmat = jnp.broadcast_to(gc, (CHUNK, CHUNK))
            keep = cc <= rr
            decay = jnp.where(keep, jnp.exp(jnp.where(keep, cmat - cmat.T, 0.0)), 0.0)
            kb = k * beta
            kbf = k.astype(BF16)
            kk = lax.dot_general(kb.astype(BF16), kbf, nt, preferred_element_type=F32)
            tinv = _unit_lower_inverse(jnp.where(cc < rr, kk * decay, 0.0), rr, cc)
            egc = jnp.exp(gc)
            u_s[rows, ls] = hdot(tinv, v * beta)
            w_s[rows, ls] = hdot(tinv, kb * egc).astype(BF16)
            attn = lax.dot_general(q.astype(BF16), kbf, nt, preferred_element_type=F32) * decay
            at_s[e, rows, :] = attn.astype(BF16)
            g_last = gc[CHUNK - 1:CHUNK, :]
            qe_s[rows, ls] = (q * egc).astype(BF16)
            ke_s[rows, ls] = (k * jnp.exp(g_last - gc)).astype(BF16)
            eg_s[e, pl.ds(n, 1), :] = jnp.broadcast_to(jnp.exp(g_last), (1, LANES))
        return 0

    lax.fori_loop(0, n_chunks, prepare, 0)

    def scan(n, states):
        rows = pl.ds(pl.multiple_of(n * CHUNK, CHUNK), CHUNK)
        new_states = []
        for e in range(heads_per_step):
            ls = slice(e * GDN_HEAD_DIM, (e + 1) * GDN_HEAD_DIM)
            st = states[e]
            sb = st.astype(BF16)
            v_new = u_s[rows, ls] - jnp.dot(w_s[rows, ls], sb, preferred_element_type=F32)
            vb = v_new.astype(BF16)
            o_ref[rows, ls] = (jnp.dot(qe_s[rows, ls], sb, preferred_element_type=F32)
                               + jnp.dot(at_s[e, rows, :], vb, preferred_element_type=F32))
            new_states.append(st * eg_s[e, pl.ds(n, 1), :]
                              + lax.dot_general(ke_s[rows, ls], vb, tn, preferred_element_type=F32))
        return tuple(new_states)

    init = tuple(jnp.zeros((GDN_HEAD_DIM, GDN_HEAD_DIM), F32) for _ in range(heads_per_step))
    final = lax.fori_loop(0, n_chunks, scan, init)
    for e in range(heads_per_step):
        s_ref[0, e] = final[e]


def _gdn_prompt(qkv, col, tri64, batch, seq, heads_per_step=2):
    n = qkv.shape[0]
    wdt = heads_per_step * GDN_HEAD_DIM
    groups = GDN_HEADS // heads_per_step
    blk = lambda off: pl.BlockSpec((seq, wdt), lambda b, j: (b, off * groups + j))
    return pl.pallas_call(
        functools.partial(_gdn_prompt_body, heads_per_step=heads_per_step),
        grid=(batch, groups),
        in_specs=[blk(0), blk(1), blk(2),
                  pl.BlockSpec((seq, LANES), lambda b, j: (b, 0)),
                  pl.BlockSpec((CHUNK, CHUNK), lambda b, j: (0, 0))],
        out_specs=[pl.BlockSpec((seq, wdt), lambda b, j: (b, j)),
                   pl.BlockSpec((1, heads_per_step, GDN_HEAD_DIM, GDN_HEAD_DIM), lambda b, j: (b, j, 0, 0))],
        out_shape=[jax.ShapeDtypeStruct((n, GDN_WIDTH), F32),
                   jax.ShapeDtypeStruct((batch, GDN_HEADS, GDN_HEAD_DIM, GDN_HEAD_DIM), F32)],
        scratch_shapes=[pltpu.VMEM((seq, wdt), F32), pltpu.VMEM((seq, wdt), BF16), pltpu.VMEM((seq, wdt), BF16),
                        pltpu.VMEM((seq, wdt), BF16), pltpu.VMEM((heads_per_step, seq, CHUNK), BF16),
                        pltpu.VMEM((heads_per_step, seq // CHUNK, LANES), F32)],
        compiler_params=pltpu.CompilerParams(dimension_semantics=("arbitrary",) * 2, vmem_limit_bytes=VMEM_LIMIT),
        name="gdn_prompt",
    )(qkv, qkv, qkv, col, tri64)


def _fox_prompt(q, col, kT, vT, cT, tq):
    n = q.shape[0]
    batch, _, seq = kT.shape
    nq = seq // tq
    pair = 2 * FOX_HEAD_DIM
    return pl.pallas_call(
        functools.partial(_fox_prompt_body, tq=tq),
        grid=(batch, FOX_HEADS // 2, nq),
        in_specs=[
            pl.BlockSpec((tq, pair), lambda b, p, i: (b * nq + i, p)),
            pl.BlockSpec((tq, LANES), lambda b, p, i: (b * nq + i, 0)),
            pl.BlockSpec((1, pair, seq), lambda b, p, i: (b, p, 0)),
            pl.BlockSpec((1, pair, seq), lambda b, p, i: (b, p, 0)),
            pl.BlockSpec((1, FOX_HEADS, seq), lambda b, p, i: (b, 0, 0)),
        ],
        out_specs=pl.BlockSpec((tq, pair), lambda b, p, i: (b * nq + i, p)),
        out_shape=jax.ShapeDtypeStruct((n, FOX_WIDTH), BF16),
        scratch_shapes=[pltpu.VMEM((pair, seq), BF16), pltpu.VMEM((pair, seq), BF16)],
        compiler_params=pltpu.CompilerParams(dimension_semantics=("arbitrary",) * 3, vmem_limit_bytes=VMEM_LIMIT),
        name="fox_prompt",
    )(q, col, kT, vT, cT)


ROW_TILE = 512
FOX_Q_TILE = 256


def kernel(x_prompt, x_sample, cache_k, cache_v, cache_logf, state_gdn, state_conv, page_table,
           ffn1_norm_pre, ffn1_w_gate, ffn1_w_up, ffn1_w_down, ffn1_norm_post,
           mix_norm_pre, w_in, fox_forget_bias, gdn_conv_w, gdn_dt_bias, gdn_a_log, gdn_out_norm,
           w_branch_fox, w_branch_gdn, w_out, mix_norm_post,
           ffn2_norm_pre, ffn2_w_gate, ffn2_w_up, ffn2_w_down, ffn2_norm_post):
    batch, seq, d = x_prompt.shape
    db = x_sample.shape[0]
    assert x_sample.shape[1] == 1, "sample group carries one new token per sequence"
    depth = w_in.shape[0]
    xp = x_prompt.reshape(batch * seq, d)
    xs = x_sample.reshape(db, d)
    row = lambda v: v.reshape(1, -1).astype(F32)
    st_p, st_s = [], []
    for l in range(depth):
        ffn1 = _ffn_weights(ffn1_w_gate[l], ffn1_w_up[l], ffn1_w_down[l])
        ffn2 = _ffn_weights(ffn2_w_gate[l], ffn2_w_up[l], ffn2_w_down[l])
        w = _mixer_weights(w_in[l], fox_forget_bias[l], gdn_conv_w[l], gdn_dt_bias[l], gdn_a_log[l])
        out_w = (row(gdn_out_norm[l]), w_branch_fox[l].astype(BF16), w_branch_gdn[l].astype(BF16),
                 w_out[l].astype(BF16), row(mix_norm_post[l]))

        xp = _ffn(xp, row(ffn1_norm_pre[l]), *ffn1, row(ffn1_norm_post[l]), ROW_TILE)
        (q, kT, vT, lfT, cT, _, col, qkv, conv_p, z, ga, gb) = _proj_prompt(
            xp, row(mix_norm_pre[l]), w, batch, seq, ROW_TILE)
        o_a = _fox_prompt(q, col, kT, vT, cT, FOX_Q_TILE)
        o_g, s_p = _gdn_prompt(qkv, col, w["tri"][:CHUNK, :CHUNK], batch, seq)
        xp = _mix_out(xp, o_a, o_g, z, ga, gb, *out_w, ROW_TILE)
        xp = _ffn(xp, row(ffn2_norm_pre[l]), *ffn2, row(ffn2_norm_post[l]), ROW_TILE)
        to_heads = lambda t: t.reshape(batch, FOX_HEADS, FOX_HEAD_DIM, seq).transpose(0, 3, 1, 2)
        st_p.append((to_heads(kT), to_heads(vT), lfT.transpose(0, 2, 1), s_p, conv_p))

        xs = _ffn(xs, row(ffn1_norm_pre[l]), *ffn1, row(ffn1_norm_post[l]), db)
        buf = state_conv[l].transpose(1, 0, 2)
        bufT = state_conv[l][:, :, :2 * GDN_WIDTH].transpose(1, 2, 0)
        (qTs, kTs, vTs, small, smallT, u, qkT, vg, zs, gas, gbs) = _proj_sample(
            xs, row(mix_norm_pre[l]), w, buf, bufT)
        o_as = _fox_decode(page_table, qTs, kTs, vTs, smallT,
                           cache_k[l].transpose(0, 2, 3, 1), cache_v[l].transpose(0, 2, 3, 1),
                           cache_logf[l].transpose(0, 2, 1))
        o_gs, s_s = _gdn_decode(qkT, vg, smallT, state_gdn[l])
        xs = _mix_out(xs, o_as.astype(BF16), o_gs, zs, gas, gbs, *out_w, db)
        xs = _ffn(xs, row(ffn2_norm_pre[l]), *ffn2, row(ffn2_norm_post[l]), db)
        new_conv = jnp.concatenate([state_conv[l][:, 1:], u[:, None, :]], axis=1)
        st_s.append((kTs.T.reshape(db, 1, FOX_HEADS, FOX_HEAD_DIM), vTs.T.reshape(db, 1, FOX_HEADS, FOX_HEAD_DIM),
                     small[:, :FOX_HEADS].reshape(db, 1, FOX_HEADS), s_s, new_conv))

    stack = lambda states, i: jnp.stack([s[i] for s in states], axis=0)
    return (xp.reshape(batch, seq, d), xs.reshape(db, 1, d),
            *[stack(st_p, i) for i in range(5)], *[stack(st_s, i) for i in range(5)])
```

```python
import functools

import jax
import jax.numpy as jnp
from jax import lax
from jax.experimental import pallas as pl
from jax.experimental.pallas import tpu as pltpu

F32 = jnp.float32
BF16 = jnp.bfloat16
EPS = 1e-6
LANES = 128
VMEM_LIMIT = 56 * 1024 * 1024

FOX_HEADS = 8
FOX_HEAD_DIM = 64
FOX_WIDTH = FOX_HEADS * FOX_HEAD_DIM
GDN_HEADS = 4
GDN_HEAD_DIM = 128
GDN_WIDTH = GDN_HEADS * GDN_HEAD_DIM
CONV_WIDTH = 4
CONV_CH = 3 * GDN_WIDTH
CHUNK = 64
FF_CHUNK = 256


def _rms(x, g):
    return x * lax.rsqrt(jnp.mean(x * x, axis=-1, keepdims=True) + EPS) * g


def _sigmoid(x):
    return 1.0 / (1.0 + jnp.exp(-x))


def _silu(x):
    return x * _sigmoid(x)


def _const_spec(shape):
    n = len(shape)
    return pl.BlockSpec(shape, lambda *_: (0,) * n, pipeline_mode=pl.Buffered(1))


def _ffn_body(x_ref, gpre_ref, wg_ref, wu_ref, wd_ref, gpost_ref, o_ref):
    x = x_ref[...]
    h = _rms(x, gpre_ref[...]).astype(BF16)
    n_chunks = wg_ref.shape[0]

    def chunk(j, acc):
        g = jnp.dot(h, wg_ref[j], preferred_element_type=F32)
        u = jnp.dot(h, wu_ref[j], preferred_element_type=F32)
        a = (_silu(g) * u).astype(BF16)
        return acc + jnp.dot(a, wd_ref[j], preferred_element_type=F32)

    y = lax.fori_loop(0, n_chunks, chunk, jnp.zeros(x.shape, F32), unroll=True)
    o_ref[...] = x + 0.5 * _rms(y, gpost_ref[...])


def _ffn(x, g_pre, wg, wu, wd, g_post, tm):
    n, d = x.shape
    row = pl.BlockSpec((tm, d), lambda i: (i, 0))
    return pl.pallas_call(
        _ffn_body,
        grid=(pl.cdiv(n, tm),),
        in_specs=[row, _const_spec((1, d)), _const_spec(wg.shape), _const_spec(wu.shape),
                  _const_spec(wd.shape), _const_spec((1, d))],
        out_specs=row,
        out_shape=jax.ShapeDtypeStruct((n, d), F32),
        compiler_params=pltpu.CompilerParams(dimension_semantics=("arbitrary",), vmem_limit_bytes=VMEM_LIMIT),
        name="ffn",
    )(x, g_pre, wg, wu, wd, g_post)


def _ffn_weights(w_gate, w_up, w_down):
    d, ff = w_gate.shape
    nc = ff // FF_CHUNK
    wg = w_gate.astype(BF16).reshape(d, nc, FF_CHUNK).transpose(1, 0, 2)
    wu = w_up.astype(BF16).reshape(d, nc, FF_CHUNK).transpose(1, 0, 2)
    wd = w_down.astype(BF16).reshape(nc, FF_CHUNK, d)
    return wg, wu, wd


def _split3(x):
    hi = x.astype(BF16)
    r = x - hi.astype(F32)
    mid = r.astype(BF16)
    lo = (r - mid.astype(F32)).astype(BF16)
    return hi, mid, lo


def _dot3(tri_bf16, x):
    hi, mid, lo = _split3(x)
    d = functools.partial(jnp.dot, tri_bf16, preferred_element_type=F32)
    return d(hi) + d(mid) + d(lo)


def _softplus(x):
    return jnp.maximum(x, 0.0) + jnp.log(1.0 + jnp.exp(-jnp.abs(x)))


def _small_cols(small, bias_ref, aneg_ref):
    lane = lax.broadcasted_iota(jnp.int32, small.shape, 1)
    pre = small + bias_ref[...]
    sp = _softplus(pre)
    logf = pre - sp
    g = aneg_ref[...] * sp
    beta = _sigmoid(small)
    gb = jnp.where(lane < FOX_HEADS + GDN_HEADS, g, beta)
    return logf, gb


def _l2n_heads(x):
    outs = []
    for h in range(x.shape[1] // GDN_HEAD_DIM):
        xh = x[:, h * GDN_HEAD_DIM:(h + 1) * GDN_HEAD_DIM]
        outs.append(xh * lax.rsqrt(jnp.sum(xh * xh, axis=-1, keepdims=True) + EPS))
    return jnp.concatenate(outs, axis=-1)


def _proj_prompt_body(x_ref, gpre_ref, wq_ref, wkT_ref, wvT_ref, ws_ref, wc_ref, wz_ref, wga_ref, wgb_ref,
                      bias_ref, aneg_ref, convw_ref, tri_ref,
                      q_ref, kT_ref, vT_ref, lfT_ref, cT_ref, gbT_ref, col_ref, qkv_ref, cst_ref,
                      z_ref, ga_ref, gb_ref, ubuf, ccarry, *, tiles_per_seq):
    i = pl.program_id(0)
    tm = x_ref.shape[0]

    @pl.when(i % tiles_per_seq == 0)
    def _():
        ubuf[0:8, :] = jnp.zeros((8, ubuf.shape[1]), F32)
        ccarry[...] = jnp.zeros(ccarry.shape, F32)

    h = _rms(x_ref[...], gpre_ref[...]).astype(BF16)
    dot = functools.partial(jnp.dot, h, preferred_element_type=F32)
    nt = (((1,), (1,)), ((), ()))

    q_ref[...] = (dot(wq_ref[...]) * (FOX_HEAD_DIM ** -0.5)).astype(BF16)
    kT_ref[0] = lax.dot_general(wkT_ref[...], h, nt, preferred_element_type=F32)
    vT_ref[0] = lax.dot_general(wvT_ref[...], h, nt, preferred_element_type=F32)
    z_ref[...] = dot(wz_ref[...])
    ga_ref[...] = _sigmoid(dot(wga_ref[...]))
    gb_ref[...] = _sigmoid(dot(wgb_ref[...]))

    logf, gb = _small_cols(dot(ws_ref[...]), bias_ref, aneg_ref)
    carry = ccarry[...]
    blocks = []
    for r in range(tm // LANES):
        cb = _dot3(tri_ref[...], logf[r * LANES:(r + 1) * LANES, :]) + carry
        carry = cb[LANES - 1:LANES, :]
        blocks.append(cb)
    ccarry[...] = carry
    c = jnp.concatenate(blocks, axis=0)
    lane = lax.broadcasted_iota(jnp.int32, c.shape, 1)
    col = jnp.where(lane < FOX_HEADS, c, gb)
    col_ref[...] = col
    colT = col.T
    lfT_ref[0] = logf.T[0:FOX_HEADS, :]
    cT_ref[0] = colT[0:FOX_HEADS, :]
    gbT_ref[0] = colT[FOX_HEADS:2 * FOX_HEADS, :]

    ubuf[8:8 + tm, :] = dot(wc_ref[...])
    base = 8 - (CONV_WIDTH - 1)
    conv = ubuf[base:base + tm, :] * convw_ref[0:1, :]
    for k in range(1, CONV_WIDTH):
        conv = conv + ubuf[base + k:base + k + tm, :] * convw_ref[k:k + 1, :]
    cst_ref[0] = ubuf[tm + base:tm + 8, :]
    ubuf[0:8, :] = ubuf[tm:tm + 8, :]
    conv = _silu(conv)
    qkv_ref[:, 0:2 * GDN_WIDTH] = _l2n_heads(conv[:, 0:2 * GDN_WIDTH])
    qkv_ref[:, 2 * GDN_WIDTH:] = conv[:, 2 * GDN_WIDTH:]


def _proj_prompt(x, g_pre, w, batch, seq, tm):
    n, d = x.shape
    tps = seq // tm
    row = lambda width: pl.BlockSpec((tm, width), lambda i: (i, 0))
    seqT = lambda rows: pl.BlockSpec((1, rows, tm), lambda i: (i // tps, 0, i % tps))
    consts = [g_pre, w["q"], w["kT"], w["vT"], w["small"], w["conv"], w["z"], w["ga"], w["gb"],
              w["bias"], w["aneg"], w["convw"], w["tri"]]
    out_shape = [
        jax.ShapeDtypeStruct((n, FOX_WIDTH), BF16),
        jax.ShapeDtypeStruct((batch, FOX_WIDTH, seq), F32),
        jax.ShapeDtypeStruct((batch, FOX_WIDTH, seq), F32),
        jax.ShapeDtypeStruct((batch, FOX_HEADS, seq), F32),
        jax.ShapeDtypeStruct((batch, FOX_HEADS, seq), F32),
        jax.ShapeDtypeStruct((batch, 2 * GDN_HEADS, seq), F32),
        jax.ShapeDtypeStruct((n, LANES), F32),
        jax.ShapeDtypeStruct((n, CONV_CH), F32),
        jax.ShapeDtypeStruct((batch, CONV_WIDTH - 1, CONV_CH), F32),
        jax.ShapeDtypeStruct((n, GDN_WIDTH), F32),
        jax.ShapeDtypeStruct((n, d), F32),
        jax.ShapeDtypeStruct((n, d), F32),
    ]
    out_specs = [row(FOX_WIDTH), seqT(FOX_WIDTH), seqT(FOX_WIDTH), seqT(FOX_HEADS), seqT(FOX_HEADS),
                 seqT(2 * GDN_HEADS), row(LANES), row(CONV_CH),
                 pl.BlockSpec((1, CONV_WIDTH - 1, CONV_CH), lambda i: (i // tps, 0, 0)),
                 row(GDN_WIDTH), row(d), row(d)]
    return pl.pallas_call(
        functools.partial(_proj_prompt_body, tiles_per_seq=tps),
        grid=(n // tm,),
        in_specs=[row(d)] + [_const_spec(a.shape) for a in consts],
        out_specs=out_specs,
        out_shape=out_shape,
        scratch_shapes=[pltpu.VMEM((tm + 8, CONV_CH), F32), pltpu.VMEM((1, LANES), F32)],
        compiler_params=pltpu.CompilerParams(dimension_semantics=("arbitrary",), vmem_limit_bytes=VMEM_LIMIT),
        name="proj_prompt",
    )(x, *consts)


def _mixer_weights(w_in, b_f, conv_w, dt_bias, a_log):
    sizes = (FOX_WIDTH, FOX_WIDTH, FOX_WIDTH, FOX_HEADS, CONV_CH, GDN_HEADS, GDN_HEADS, GDN_WIDTH)
    d = w_in.shape[0]
    offs = [0]
    for s in sizes:
        offs.append(offs[-1] + s)
    part = lambda k: w_in[:, offs[k]:offs[k + 1]]
    ga0 = offs[-1]
    n_small = FOX_HEADS + 2 * GDN_HEADS
    small = jnp.concatenate([part(3), part(5), part(6), jnp.zeros((d, LANES - n_small), w_in.dtype)], axis=1)
    pad = lambda v, lo: jnp.zeros((1, LANES), F32).at[0, lo:lo + v.shape[0]].set(v.astype(F32))
    r = lax.broadcasted_iota(jnp.int32, (LANES, LANES), 0)
    c = lax.broadcasted_iota(jnp.int32, (LANES, LANES), 1)
    return {
        "q": part(0).astype(BF16), "kT": part(1).T.astype(BF16), "vT": part(2).T.astype(BF16),
        "small": small.astype(BF16), "conv": part(4).astype(BF16), "z": part(7).astype(BF16),
        "ga": w_in[:, ga0:ga0 + d].astype(BF16), "gb": w_in[:, ga0 + d:ga0 + 2 * d].astype(BF16),
        "bias": pad(b_f, 0) + pad(dt_bias, FOX_HEADS),
        "aneg": pad(-jnp.exp(a_log.astype(F32)), FOX_HEADS),
        "convw": conv_w.astype(F32),
        "tri": (c <= r).astype(BF16),
    }


NEG_BIG = -1e30


def _fox_prompt_body(q_ref, col_ref, kT_ref, vT_ref, cT_ref, o_ref, kb_ref, vb_ref, *, tq):
    p = pl.program_id(1)
    qi = pl.program_id(2)

    @pl.when(qi == 0)
    def _():
        kb_ref[...] = kT_ref[0].astype(BF16)
        vb_ref[...] = vT_ref[0].astype(BF16)

    nt = (((1,), (1,)), ((), ()))
    col = col_ref[...]
    lane = lax.broadcasted_iota(jnp.int32, col.shape, 1)
    hrows = [slice(e * FOX_HEAD_DIM, (e + 1) * FOX_HEAD_DIM) for e in range(2)]
    qs = [q_ref[:, r] for r in hrows]
    cqs = [jnp.sum(jnp.where(lane == 2 * p + e, col, 0.0), axis=-1, keepdims=True) for e in range(2)]

    def block(kb, carry, masked):
        ks = pl.ds(pl.multiple_of(kb * tq, tq), tq)
        out = []
        for e in range(2):
            m, l, acc = carry[e]
            s = jnp.dot(qs[e], kb_ref[hrows[e], ks], preferred_element_type=F32)
            s = s + cqs[e] - cT_ref[0, pl.ds(2 * p + e, 1), ks]
            if masked:
                r = lax.broadcasted_iota(jnp.int32, s.shape, 0)
                c = lax.broadcasted_iota(jnp.int32, s.shape, 1)
                s = jnp.where(c <= r, s, NEG_BIG)
            m_new = jnp.maximum(m, jnp.max(s, axis=-1, keepdims=True))
            pr = jnp.exp(s - m_new)
            alpha = jnp.exp(m - m_new)
            l = alpha * l + jnp.sum(pr, axis=-1, keepdims=True)
            pv = lax.dot_general(pr.astype(BF16), vb_ref[hrows[e], ks], nt, preferred_element_type=F32)
            out.append((m_new, l, alpha * acc + pv))
        return tuple(out)

    init = tuple((jnp.full((tq, 1), NEG_BIG, F32), jnp.zeros((tq, 1), F32), jnp.zeros((tq, FOX_HEAD_DIM), F32))
                 for _ in range(2))
    carry = lax.fori_loop(0, qi, lambda kb, c: block(kb, c, False), init)
    carry = block(qi, carry, True)
    o_ref[...] = jnp.concatenate([acc / l for (_, l, acc) in carry], axis=-1).astype(o_ref.dtype)


def _proj_sample_body(x_ref, gpre_ref, wqT_ref, wkT_ref, wvT_ref, ws_ref, wc_ref, wcT_ref, wz_ref, wga_ref, wgb_ref,
                      bias_ref, aneg_ref, convw_ref, convwT_ref, buf_ref, bufT_ref,
                      qT_ref, kT_ref, vT_ref, small_ref, smallT_ref, u_ref, qkT_ref, vg_ref, z_ref, ga_ref, gb_ref):
    h = _rms(x_ref[...], gpre_ref[...]).astype(BF16)
    dot = functools.partial(jnp.dot, h, preferred_element_type=F32)
    dot_t = lambda w: lax.dot_general(w, h, (((1,), (1,)), ((), ())), preferred_element_type=F32)
    qT_ref[...] = dot_t(wqT_ref[...]) * (FOX_HEAD_DIM ** -0.5)
    kT_ref[...] = dot_t(wkT_ref[...])
    vT_ref[...] = dot_t(wvT_ref[...])
    z_ref[...] = dot(wz_ref[...])
    ga_ref[...] = _sigmoid(dot(wga_ref[...]))
    gb_ref[...] = _sigmoid(dot(wgb_ref[...]))

    logf, gb = _small_cols(dot(ws_ref[...]), bias_ref, aneg_ref)
    lane = lax.broadcasted_iota(jnp.int32, logf.shape, 1)
    small = jnp.where(lane < FOX_HEADS, logf, gb)
    small_ref[...] = small
    pad = jnp.concatenate([small, jnp.zeros((LANES - small.shape[0], LANES), F32)], axis=0)
    smallT_ref[...] = pad.T[:, 0:small.shape[0]]

    u = dot(wc_ref[...])
    u_ref[...] = u
    vs = slice(2 * GDN_WIDTH, CONV_CH)
    conv_v = u[:, vs] * convw_ref[CONV_WIDTH - 1:CONV_WIDTH, vs]
    conv_qk = dot_t(wcT_ref[...]) * convwT_ref[:, CONV_WIDTH - 1:CONV_WIDTH]
    for k in range(CONV_WIDTH - 1):
        conv_v = conv_v + buf_ref[k][:, vs] * convw_ref[k:k + 1, vs]
        conv_qk = conv_qk + bufT_ref[k] * convwT_ref[:, k:k + 1]
    vg_ref[...] = _silu(conv_v)
    conv_qk = _silu(conv_qk)
    for hd in range(2 * GDN_HEADS):
        rs = slice(hd * GDN_HEAD_DIM, (hd + 1) * GDN_HEAD_DIM)
        xh = conv_qk[rs, :]
        qkT_ref[rs, :] = xh * lax.rsqrt(jnp.sum(xh * xh, axis=0, keepdims=True) + EPS)


def _proj_sample(x, g_pre, w, buf, bufT):
    db, d = x.shape
    ins = [x, g_pre, w["q"].T, w["kT"], w["vT"], w["small"], w["conv"], w["conv"][:, :2 * GDN_WIDTH].T, w["z"],
           w["ga"], w["gb"], w["bias"], w["aneg"], w["convw"], w["convw"][:, :2 * GDN_WIDTH].T, buf, bufT]
    out_shape = [
        jax.ShapeDtypeStruct((FOX_WIDTH, db), F32), jax.ShapeDtypeStruct((FOX_WIDTH, db), F32),
        jax.ShapeDtypeStruct((FOX_WIDTH, db), F32),
        jax.ShapeDtypeStruct((db, LANES), F32), jax.ShapeDtypeStruct((LANES, db), F32),
        jax.ShapeDtypeStruct((db, CONV_CH), F32), jax.ShapeDtypeStruct((2 * GDN_WIDTH, db), F32),
        jax.ShapeDtypeStruct((db, GDN_WIDTH), F32), jax.ShapeDtypeStruct((db, GDN_WIDTH), F32),
        jax.ShapeDtypeStruct((db, d), F32), jax.ShapeDtypeStruct((db, d), F32),
    ]
    return pl.pallas_call(
        _proj_sample_body, out_shape=out_shape,
        compiler_params=pltpu.CompilerParams(vmem_limit_bytes=VMEM_LIMIT),
        name="proj_sample",
    )(*ins)


def _take_lane(x, idx):
    lane = lax.broadcasted_iota(jnp.int32, x.shape, 1)
    return jnp.sum(jnp.where(lane == idx, x, 0.0), axis=-1, keepdims=True)


PAGES_PER_STEP = 8


def _fox_decode_body(pt_ref, qT_ref, kTn_ref, vTn_ref, smallT_ref, *rest, page):
    g_pages = PAGES_PER_STEP
    k_refs = rest[0:g_pages]
    v_refs = rest[g_pages:2 * g_pages]
    lf_refs = rest[2 * g_pages:3 * g_pages]
    o_ref, qb_s, acc_s, m_s, l_s, sfx_s = rest[3 * g_pages:]
    b = pl.program_id(0)
    j = pl.program_id(1)
    hd = FOX_HEAD_DIM
    lane = lax.broadcasted_iota(jnp.int32, (FOX_HEADS, page), 1)

    def per_head_dot(a, bm):
        return jnp.concatenate(
            [jnp.sum(a[h * hd:(h + 1) * hd, :] * bm[h * hd:(h + 1) * hd, :], axis=0, keepdims=True)
             for h in range(FOX_HEADS)], axis=0)

    @pl.when(j == 0)
    def _():
        qb = jnp.broadcast_to(_take_lane(qT_ref[...], b), (FOX_WIDTH, page))
        kn = jnp.broadcast_to(_take_lane(kTn_ref[...], b), (FOX_WIDTH, page))
        vn = jnp.broadcast_to(_take_lane(vTn_ref[...], b), (FOX_WIDTH, page))
        qb_s[...] = qb
        m_s[...] = per_head_dot(qb, kn)
        l_s[...] = jnp.ones(l_s.shape, F32)
        lane_w = lax.broadcasted_iota(jnp.int32, (FOX_WIDTH, page), 1)
        acc_s[...] = jnp.where(lane_w == 0, vn, 0.0)
        sfx_s[...] = jnp.broadcast_to(_take_lane(smallT_ref[0:FOX_HEADS, :], b), sfx_s.shape)

    carry = sfx_s[...]
    scores = []
    for g in range(g_pages):
        lf = lf_refs[g][0]
        incl = lf
        sh = 1
        while sh < page:
            incl = incl + jnp.where(lane + sh < page, pltpu.roll(incl, page - sh, axis=1), 0.0)
            sh *= 2
        bias = carry + (incl - lf)
        carry = carry + incl[:, 0:1]
        kt = k_refs[g][0].reshape(FOX_WIDTH, page)
        scores.append(per_head_dot(kt, qb_s[...]) + bias)
    sfx_s[...] = carry

    m_old = m_s[...]
    m_new = m_old
    for s in scores:
        m_new = jnp.maximum(m_new, jnp.max(s, axis=-1, keepdims=True))
    alpha = jnp.exp(m_old - m_new)
    probs = [jnp.exp(s - m_new) for s in scores]
    l_new = alpha * l_s[...]
    for p in probs:
        l_new = l_new + jnp.sum(p, axis=-1, keepdims=True)
    m_s[...] = m_new
    l_s[...] = l_new
    for h in range(FOX_HEADS):
        rs = slice(h * hd, (h + 1) * hd)
        acc = acc_s[rs, :] * alpha[h:h + 1, 0:1]
        for g in range(g_pages):
            acc = acc + probs[g][h:h + 1, :] * v_refs[g][0, h]
        acc_s[rs, :] = acc

    @pl.when(j == pl.num_programs(1) - 1)
    def _():
        inv_l = 1.0 / l_s[...]
        for h in range(FOX_HEADS):
            rs = slice(h * hd, (h + 1) * hd)
            o_ref[0, rs, :] = jnp.sum(acc_s[rs, :], axis=-1, keepdims=True) * inv_l[h:h + 1, 0:1]


def _fox_decode(page_table, qT, kTn, vTn, smallT, cache_kT, cache_vT, cache_lfT):
    db, n_pages = page_table.shape
    page = cache_kT.shape[-1]
    g_pages = PAGES_PER_STEP
    n_steps = n_pages // g_pages
    const = lambda shape: pl.BlockSpec(shape, lambda b, j, pt: (0,) * len(shape))

    def page_idx(b, j, pt, g):
        return pt[b * n_pages + (n_steps - 1 - j) * g_pages + (g_pages - 1 - g)]

    kv_spec = lambda g: pl.BlockSpec((1, FOX_HEADS, FOX_HEAD_DIM, page),
                                     lambda b, j, pt: (page_idx(b, j, pt, g), 0, 0, 0))
    lf_spec = lambda g: pl.BlockSpec((1, FOX_HEADS, page), lambda b, j, pt: (page_idx(b, j, pt, g), 0, 0))
    in_specs = ([const(qT.shape), const(kTn.shape), const(vTn.shape), const(smallT.shape)]
                + [kv_spec(g) for g in range(g_pages)] + [kv_spec(g) for g in range(g_pages)]
                + [lf_spec(g) for g in range(g_pages)])
    out = pl.pallas_call(
        functools.partial(_fox_decode_body, page=page),
        grid_spec=pltpu.PrefetchScalarGridSpec(
            num_scalar_prefetch=1, grid=(db, n_steps), in_specs=in_specs,
            out_specs=pl.BlockSpec((1, FOX_WIDTH, 1), lambda b, j, pt: (b, 0, 0)),
            scratch_shapes=[pltpu.VMEM((FOX_WIDTH, page), F32), pltpu.VMEM((FOX_WIDTH, page), F32),
                            pltpu.VMEM((FOX_HEADS, page), F32), pltpu.VMEM((FOX_HEADS, page), F32),
                            pltpu.VMEM((FOX_HEADS, page), F32)]),
        out_shape=jax.ShapeDtypeStruct((db, FOX_WIDTH, 1), F32),
        compiler_params=pltpu.CompilerParams(dimension_semantics=("arbitrary",) * 2, vmem_limit_bytes=VMEM_LIMIT),
        name="fox_decode",
    )(page_table.reshape(-1), qT, kTn, vTn, smallT, *([cache_kT] * g_pages), *([cache_vT] * g_pages),
      *([cache_lfT] * g_pages))
    return out.reshape(db, FOX_WIDTH)


def _gdn_decode_body(qkT_ref, vg_ref, smallT_ref, s_ref, o_ref, snew_ref):
    b = pl.program_id(0)
    qk = _take_lane(qkT_ref[...], b)
    gbeta = _take_lane(smallT_ref[FOX_HEADS:FOX_HEADS + 2 * GDN_HEADS, :], b)
    for h in range(GDN_HEADS):
        rs = slice(h * GDN_HEAD_DIM, (h + 1) * GDN_HEAD_DIM)
        q = qk[rs, :] * (GDN_HEAD_DIM ** -0.5)
        k = qk[GDN_WIDTH + h * GDN_HEAD_DIM:GDN_WIDTH + (h + 1) * GDN_HEAD_DIM, :]
        v = vg_ref[0, :, rs]
        st = s_ref[0, h] * jnp.exp(gbeta[h:h + 1, :])
        delta = (v - jnp.sum(k * st, axis=0, keepdims=True)) * gbeta[GDN_HEADS + h:GDN_HEADS + h + 1, :]
        st = st + k * delta
        snew_ref[0, h] = st
        o_ref[0, :, rs] = jnp.sum(q * st, axis=0, keepdims=True)


def _gdn_decode(qkT, vg, smallT, state):
    db = vg.shape[0]
    const = lambda shape: pl.BlockSpec(shape, lambda b: (0,) * len(shape))
    st_spec = pl.BlockSpec((1, GDN_HEADS, GDN_HEAD_DIM, GDN_HEAD_DIM), lambda b: (b, 0, 0, 0))
    row_spec = pl.BlockSpec((1, 1, GDN_WIDTH), lambda b: (b, 0, 0))
    o, s_new = pl.pallas_call(
        _gdn_decode_body,
        grid=(db,),
        in_specs=[const(qkT.shape), row_spec, const(smallT.shape), st_spec],
        out_specs=[row_spec, st_spec],
        out_shape=[jax.ShapeDtypeStruct((db, 1, GDN_WIDTH), F32), jax.ShapeDtypeStruct(state.shape, F32)],
        compiler_params=pltpu.CompilerParams(dimension_semantics=("arbitrary",)),
        name="gdn_decode",
    )(qkT, vg.reshape(db, 1, GDN_WIDTH), smallT, state)
    return o.reshape(db, GDN_WIDTH), s_new


def _mix_out_body(x_ref, oa_ref, og_ref, z_ref, ga_ref, gb_ref, ggdn_ref, wbf_ref, wbg_ref, wout_ref, gpost_ref,
                  o_ref):
    og = og_ref[...]
    z = z_ref[...]
    parts = []
    for h in range(GDN_HEADS):
        ls = slice(h * GDN_HEAD_DIM, (h + 1) * GDN_HEAD_DIM)
        parts.append(_rms(og[:, ls], ggdn_ref[...]) * _silu(z[:, ls]))
    ogn = jnp.concatenate(parts, axis=-1).astype(BF16)
    y_a = jnp.dot(oa_ref[...], wbf_ref[...], preferred_element_type=F32)
    y_g = jnp.dot(ogn, wbg_ref[...], preferred_element_type=F32)
    y = (ga_ref[...] * y_a + gb_ref[...] * y_g).astype(BF16)
    y = jnp.dot(y, wout_ref[...], preferred_element_type=F32)
    o_ref[...] = x_ref[...] + _rms(y, gpost_ref[...])


def _mix_out(x, o_a, o_g, z, ga, gb, g_gdn, wbf, wbg, wout, g_post, tm):
    n, d = x.shape
    row = lambda width: pl.BlockSpec((tm, width), lambda i: (i, 0))
    consts = [g_gdn, wbf, wbg, wout, g_post]
    return pl.pallas_call(
        _mix_out_body,
        grid=(pl.cdiv(n, tm),),
        in_specs=[row(d), row(FOX_WIDTH), row(GDN_WIDTH), row(GDN_WIDTH), row(d), row(d)]
        + [_const_spec(a.shape) for a in consts],
        out_specs=row(d),
        out_shape=jax.ShapeDtypeStruct((n, d), F32),
        compiler_params=pltpu.CompilerParams(dimension_semantics=("arbitrary",), vmem_limit_bytes=VMEM_LIMIT),
        name="mix_out",
    )(x, o_a, o_g, z, ga, gb, *consts)


def _chunk_gates(col_ref, tri_ref, rows, heads, lane, rr, cc):
    colc = col_ref[rows, :]
    cs = _dot3(tri_ref[...], colc)
    keep = cc <= rr
    out = []
    for head in heads:
        gc = jnp.sum(jnp.where(lane == FOX_HEADS + head, cs, 0.0), axis=-1, keepdims=True)
        beta = jnp.sum(jnp.where(lane == FOX_HEADS + GDN_HEADS + head, colc, 0.0), axis=-1, keepdims=True)
        cmat = jnp.broadcast_to(gc, (CHUNK, CHUNK))
        decay = jnp.where(keep, jnp.exp(jnp.where(keep, cmat - cmat.T, 0.0)), 0.0)
        out.append((gc, beta, decay))
    return out


def _gdn_lower_body(k_ref, col_ref, tri_ref, l_ref, *, heads_per_step):
    hg = pl.program_id(1)
    n_chunks = k_ref.shape[0] // CHUNK
    nt = (((1,), (1,)), ((), ()))
    rr = lax.broadcasted_iota(jnp.int32, (CHUNK, CHUNK), 0)
    cc = lax.broadcasted_iota(jnp.int32, (CHUNK, CHUNK), 1)
    lane = lax.broadcasted_iota(jnp.int32, (CHUNK, LANES), 1)
    heads = [hg * heads_per_step + e for e in range(heads_per_step)]

    def chunk(n, _):
        rows = pl.ds(pl.multiple_of(n * CHUNK, CHUNK), CHUNK)
        gates = _chunk_gates(col_ref, tri_ref, rows, heads, lane, rr, cc)
        for e, (_, beta, decay) in enumerate(gates):
            k = k_ref[rows, e * GDN_HEAD_DIM:(e + 1) * GDN_HEAD_DIM]
            kk = lax.dot_general((k * beta).astype(BF16), k.astype(BF16), nt, preferred_element_type=F32)
            l_ref[0, e, n] = jnp.where(cc < rr, kk * decay, 0.0)
        return 0

    lax.fori_loop(0, n_chunks, chunk, 0)


def _unit_lower_inverse_body(l_ref, t_ref, lt_s, tt_s):
    lt_s[...] = l_ref[0].T
    sub = lax.broadcasted_iota(jnp.int32, (8, LANES), 0)
    for i in range(CHUNK):
        for kg in range(i // 8 + 1):
            acc = jnp.where(sub + 8 * kg == i, 1.0, 0.0)
            for j in range(8 * kg, i):
                acc = acc - lt_s[i * CHUNK + j:i * CHUNK + j + 1, :] * tt_s[j * CHUNK + 8 * kg:j * CHUNK + 8 * kg + 8, :]
            tt_s[i * CHUNK + 8 * kg:i * CHUNK + 8 * kg + 8, :] = acc
        for kg in range(i // 8 + 1, CHUNK // 8):
            tt_s[i * CHUNK + 8 * kg:i * CHUNK + 8 * kg + 8, :] = jnp.zeros((8, LANES), F32)
    t_ref[0] = tt_s[...].T


def _unit_lower_inverse(low):
    groups = low.shape[0]
    spec = pl.BlockSpec((1, LANES, CHUNK * CHUNK), lambda g: (g, 0, 0))
    return pl.pallas_call(
        _unit_lower_inverse_body,
        grid=(groups,),
        in_specs=[spec], out_specs=spec,
        out_shape=jax.ShapeDtypeStruct(low.shape, F32),
        scratch_shapes=[pltpu.VMEM((CHUNK * CHUNK, LANES), F32), pltpu.VMEM((CHUNK * CHUNK, LANES), F32)],
        compiler_params=pltpu.CompilerParams(dimension_semantics=("arbitrary",), vmem_limit_bytes=VMEM_LIMIT),
        name="unit_lower_inverse",
    )(low)


def _gdn_prompt_body(q_ref, k_ref, v_ref, col_ref, tri_ref, t_ref, o_ref, s_ref,
                     u_s, w_s, qe_s, ke_s, at_s, eg_s, *, heads_per_step):
    hg = pl.program_id(1)
    seq = q_ref.shape[0]
    n_chunks = seq // CHUNK
    nt = (((1,), (1,)), ((), ()))
    tn = (((0,), (0,)), ((), ()))
    rr = lax.broadcasted_iota(jnp.int32, (CHUNK, CHUNK), 0)
    cc = lax.broadcasted_iota(jnp.int32, (CHUNK, CHUNK), 1)
    lane = lax.broadcasted_iota(jnp.int32, (CHUNK, LANES), 1)
    heads = [hg * heads_per_step + e for e in range(heads_per_step)]

    def prepare(n, _):
        rows = pl.ds(pl.multiple_of(n * CHUNK, CHUNK), CHUNK)
        gates = _chunk_gates(col_ref, tri_ref, rows, heads, lane, rr, cc)
        for e, (gc, beta, decay) in enumerate(gates):
            ls = slice(e * GDN_HEAD_DIM, (e + 1) * GDN_HEAD_DIM)
            q = q_ref[rows, ls] * (GDN_HEAD_DIM ** -0.5)
            k = k_ref[rows, ls]
            v = v_ref[rows, ls]
            egc = jnp.exp(gc)
            tinv = t_ref[0, e, n]
            t_hi = tinv.astype(BF16)
            t_lo = (tinv - t_hi.astype(F32)).astype(BF16)
            rhs = jnp.concatenate([v * beta, k * (beta * egc)], axis=1)
            r_hi = rhs.astype(BF16)
            r_lo = (rhs - r_hi.astype(F32)).astype(BF16)
            uw = jnp.dot(jnp.concatenate([t_hi, t_lo, t_hi], axis=1), jnp.concatenate([r_hi, r_hi, r_lo], axis=0),
                         preferred_element_type=F32)
            u_s[rows, ls] = uw[:, :GDN_HEAD_DIM]
            w_s[rows, ls] = uw[:, GDN_HEAD_DIM:].astype(BF16)
            attn = lax.dot_general(q.astype(BF16), k.astype(BF16), nt, preferred_element_type=F32) * decay
            at_s[e, rows, :] = attn.astype(BF16)
            g_last = gc[CHUNK - 1:CHUNK, :]
            qe_s[rows, ls] = (q * egc).astype(BF16)
            ke_s[rows, ls] = (k * jnp.exp(g_last - gc)).astype(BF16)
            eg_s[e, pl.ds(n, 1), :] = jnp.broadcast_to(jnp.exp(g_last), (1, LANES))
        return 0

    lax.fori_loop(0, n_chunks, prepare, 0)

    def scan(n, states):
        rows = pl.ds(pl.multiple_of(n * CHUNK, CHUNK), CHUNK)
        new_states = []
        for e in range(heads_per_step):
            ls = slice(e * GDN_HEAD_DIM, (e + 1) * GDN_HEAD_DIM)
            st = states[e]
            sb = st.astype(BF16)
            v_new = u_s[rows, ls] - jnp.dot(w_s[rows, ls], sb, preferred_element_type=F32)
            vb = v_new.astype(BF16)
            o_ref[rows, ls] = (jnp.dot(qe_s[rows, ls], sb, preferred_element_type=F32)
                               + jnp.dot(at_s[e, rows, :], vb, preferred_element_type=F32))
            new_states.append(st * eg_s[e, pl.ds(n, 1), :]
                              + lax.dot_general(ke_s[rows, ls], vb, tn, preferred_element_type=F32))
        return tuple(new_states)

    init = tuple(jnp.zeros((GDN_HEAD_DIM, GDN_HEAD_DIM), F32) for _ in range(heads_per_step))
    final = lax.fori_loop(0, n_chunks, scan, init)
    for e in range(heads_per_step):
        s_ref[0, e] = final[e]


def _gdn_prompt(qkv, col, tri64, batch, seq, heads_per_step=2):
    n = qkv.shape[0]
    wdt = heads_per_step * GDN_HEAD_DIM
    groups = GDN_HEADS // heads_per_step
    n_chunks = seq // CHUNK
    blk = lambda off: pl.BlockSpec((seq, wdt), lambda b, j: (b, off * groups + j))
    col_spec = pl.BlockSpec((seq, LANES), lambda b, j: (b, 0))
    tri_spec = pl.BlockSpec((CHUNK, CHUNK), lambda b, j: (0, 0))
    mat_spec = pl.BlockSpec((1, heads_per_step, n_chunks, CHUNK, CHUNK), lambda b, j: (b, j, 0, 0, 0))
    params = pltpu.CompilerParams(dimension_semantics=("arbitrary",) * 2, vmem_limit_bytes=VMEM_LIMIT)
    low = pl.pallas_call(
        functools.partial(_gdn_lower_body, heads_per_step=heads_per_step),
        grid=(batch, groups),
        in_specs=[blk(1), col_spec, tri_spec],
        out_specs=mat_spec,
        out_shape=jax.ShapeDtypeStruct((batch, GDN_HEADS, n_chunks, CHUNK, CHUNK), F32),
        compiler_params=params,
        name="gdn_lower",
    )(qkv, col, tri64)
    n_sys = batch * GDN_HEADS * n_chunks
    tinv = _unit_lower_inverse(low.reshape(n_sys // LANES, LANES, CHUNK * CHUNK)).reshape(low.shape)
    return pl.pallas_call(
        functools.partial(_gdn_prompt_body, heads_per_step=heads_per_step),
        grid=(batch, groups),
        in_specs=[blk(0), blk(1), blk(2), col_spec, tri_spec, mat_spec],
        out_specs=[pl.BlockSpec((seq, wdt), lambda b, j: (b, j)),
                   pl.BlockSpec((1, heads_per_step, GDN_HEAD_DIM, GDN_HEAD_DIM), lambda b, j: (b, j, 0, 0))],
        out_shape=[jax.ShapeDtypeStruct((n, GDN_WIDTH), F32),
                   jax.ShapeDtypeStruct((batch, GDN_HEADS, GDN_HEAD_DIM, GDN_HEAD_DIM), F32)],
        scratch_shapes=[pltpu.VMEM((seq, wdt), F32), pltpu.VMEM((seq, wdt), BF16), pltpu.VMEM((seq, wdt), BF16),
                        pltpu.VMEM((seq, wdt), BF16), pltpu.VMEM((heads_per_step, seq, CHUNK), BF16),
                        pltpu.VMEM((heads_per_step, seq // CHUNK, LANES), F32)],
        compiler_params=params,
        name="gdn_prompt",
    )(qkv, qkv, qkv, col, tri64, tinv)


def _fox_prompt(q, col, kT, vT, cT, tq):
    n = q.shape[0]
    batch, _, seq = kT.shape
    nq = seq // tq
    pair = 2 * FOX_HEAD_DIM
    return pl.pallas_call(
        functools.partial(_fox_prompt_body, tq=tq),
        grid=(batch, FOX_HEADS // 2, nq),
        in_specs=[
            pl.BlockSpec((tq, pair), lambda b, p, i: (b * nq + i, p)),
            pl.BlockSpec((tq, LANES), lambda b, p, i: (b * nq + i, 0)),
            pl.BlockSpec((1, pair, seq), lambda b, p, i: (b, p, 0)),
            pl.BlockSpec((1, pair, seq), lambda b, p, i: (b, p, 0)),
            pl.BlockSpec((1, FOX_HEADS, seq), lambda b, p, i: (b, 0, 0)),
        ],
        out_specs=pl.BlockSpec((tq, pair), lambda b, p, i: (b * nq + i, p)),
        out_shape=jax.ShapeDtypeStruct((n, FOX_WIDTH), BF16),
        scratch_shapes=[pltpu.VMEM((pair, seq), BF16), pltpu.VMEM((pair, seq), BF16)],
        compiler_params=pltpu.CompilerParams(dimension_semantics=("arbitrary",) * 3, vmem_limit_bytes=VMEM_LIMIT),
        name="fox_prompt",
    )(q, col, kT, vT, cT)


ROW_TILE = 512
FOX_Q_TILE = 512


def kernel(x_prompt, x_sample, cache_k, cache_v, cache_logf, state_gdn, state_conv, page_table,
           ffn1_norm_pre, ffn1_w_gate, ffn1_w_up, ffn1_w_down, ffn1_norm_post,
           mix_norm_pre, w_in, fox_forget_bias, gdn_conv_w, gdn_dt_bias, gdn_a_log, gdn_out_norm,
           w_branch_fox, w_branch_gdn, w_out, mix_norm_post,
           ffn2_norm_pre, ffn2_w_gate, ffn2_w_up, ffn2_w_down, ffn2_norm_post):
    batch, seq, d = x_prompt.shape
    db = x_sample.shape[0]
    assert x_sample.shape[1] == 1, "sample group carries one new token per sequence"
    depth = w_in.shape[0]
    xp = x_prompt.reshape(batch * seq, d)
    xs = x_sample.reshape(db, d)
    row = lambda v: v.reshape(1, -1).astype(F32)
    st_p, st_s = [], []
    for l in range(depth):
        ffn1 = _ffn_weights(ffn1_w_gate[l], ffn1_w_up[l], ffn1_w_down[l])
        ffn2 = _ffn_weights(ffn2_w_gate[l], ffn2_w_up[l], ffn2_w_down[l])
        w = _mixer_weights(w_in[l], fox_forget_bias[l], gdn_conv_w[l], gdn_dt_bias[l], gdn_a_log[l])
        out_w = (row(gdn_out_norm[l]), w_branch_fox[l].astype(BF16), w_branch_gdn[l].astype(BF16),
                 w_out[l].astype(BF16), row(mix_norm_post[l]))

        xp = _ffn(xp, row(ffn1_norm_pre[l]), *ffn1, row(ffn1_norm_post[l]), ROW_TILE)
        (q, kT, vT, lfT, cT, _, col, qkv, conv_p, z, ga, gb) = _proj_prompt(
            xp, row(mix_norm_pre[l]), w, batch, seq, ROW_TILE)
        o_a = _fox_prompt(q, col, kT, vT, cT, FOX_Q_TILE)
        o_g, s_p = _gdn_prompt(qkv, col, w["tri"][:CHUNK, :CHUNK], batch, seq)
        xp = _mix_out(xp, o_a, o_g, z, ga, gb, *out_w, ROW_TILE)
        xp = _ffn(xp, row(ffn2_norm_pre[l]), *ffn2, row(ffn2_norm_post[l]), ROW_TILE)
        to_heads = lambda t: t.reshape(batch, FOX_HEADS, FOX_HEAD_DIM, seq).transpose(0, 3, 1, 2)
        st_p.append((to_heads(kT), to_heads(vT), lfT.transpose(0, 2, 1), s_p, conv_p))

        xs = _ffn(xs, row(ffn1_norm_pre[l]), *ffn1, row(ffn1_norm_post[l]), db)
        buf = state_conv[l].transpose(1, 0, 2)
        bufT = state_conv[l][:, :, :2 * GDN_WIDTH].transpose(1, 2, 0)
        (qTs, kTs, vTs, small, smallT, u, qkT, vg, zs, gas, gbs) = _proj_sample(
            xs, row(mix_norm_pre[l]), w, buf, bufT)
        o_as = _fox_decode(page_table, qTs, kTs, vTs, smallT,
                           cache_k[l].transpose(0, 2, 3, 1), cache_v[l].transpose(0, 2, 3, 1),
                           cache_logf[l].transpose(0, 2, 1))
        o_gs, s_s = _gdn_decode(qkT, vg, smallT, state_gdn[l])
        xs = _mix_out(xs, o_as.astype(BF16), o_gs, zs, gas, gbs, *out_w, db)
        xs = _ffn(xs, row(ffn2_norm_pre[l]), *ffn2, row(ffn2_norm_post[l]), db)
        new_conv = jnp.concatenate([state_conv[l][:, 1:], u[:, None, :]], axis=1)
        st_s.append((kTs.T.reshape(db, 1, FOX_HEADS, FOX_HEAD_DIM), vTs.T.reshape(db, 1, FOX_HEADS, FOX_HEAD_DIM),
                     small[:, :FOX_HEADS].reshape(db, 1, FOX_HEADS), s_s, new_conv))

    stack = lambda states, i: jnp.stack([s[i] for s in states], axis=0)
    return (xp.reshape(batch, seq, d), xs.reshape(db, 1, d),
            *[stack(st_p, i) for i in range(5)], *[stack(st_s, i) for i in range(5)])
```

```python
import functools

import jax
import jax.numpy as jnp
from jax import lax
from jax.experimental import pallas as pl
from jax.experimental.pallas import tpu as pltpu

F32 = jnp.float32
BF16 = jnp.bfloat16
EPS = 1e-6
LANES = 128
VMEM_LIMIT = 56 * 1024 * 1024

FOX_HEADS = 8
FOX_HEAD_DIM = 64
FOX_WIDTH = FOX_HEADS * FOX_HEAD_DIM
GDN_HEADS = 4
GDN_HEAD_DIM = 128
GDN_WIDTH = GDN_HEADS * GDN_HEAD_DIM
CONV_WIDTH = 4
CONV_CH = 3 * GDN_WIDTH
CHUNK = 64
FF_CHUNK = 256


def _rms(x, g):
    return x * lax.rsqrt(jnp.mean(x * x, axis=-1, keepdims=True) + EPS) * g


def _sigmoid(x):
    return 1.0 / (1.0 + jnp.exp(-x))


def _silu(x):
    return x * _sigmoid(x)


def _const_spec(shape):
    n = len(shape)
    return pl.BlockSpec(shape, lambda *_: (0,) * n, pipeline_mode=pl.Buffered(1))


def _ffn_body(x_ref, gpre_ref, wg_ref, wu_ref, wd_ref, gpost_ref, o_ref):
    x = x_ref[...]
    h = _rms(x, gpre_ref[...]).astype(BF16)
    n_chunks = wg_ref.shape[0]

    def chunk(j, acc):
        g = jnp.dot(h, wg_ref[j], preferred_element_type=F32)
        u = jnp.dot(h, wu_ref[j], preferred_element_type=F32)
        a = (_silu(g) * u).astype(BF16)
        return acc + jnp.dot(a, wd_ref[j], preferred_element_type=F32)

    y = lax.fori_loop(0, n_chunks, chunk, jnp.zeros(x.shape, F32), unroll=True)
    o_ref[...] = x + 0.5 * _rms(y, gpost_ref[...])


def _ffn(x, g_pre, wg, wu, wd, g_post, tm):
    n, d = x.shape
    row = pl.BlockSpec((tm, d), lambda i: (i, 0))
    return pl.pallas_call(
        _ffn_body,
        grid=(pl.cdiv(n, tm),),
        in_specs=[row, _const_spec((1, d)), _const_spec(wg.shape), _const_spec(wu.shape),
                  _const_spec(wd.shape), _const_spec((1, d))],
        out_specs=row,
        out_shape=jax.ShapeDtypeStruct((n, d), F32),
        compiler_params=pltpu.CompilerParams(dimension_semantics=("arbitrary",), vmem_limit_bytes=VMEM_LIMIT),
        name="ffn",
    )(x, g_pre, wg, wu, wd, g_post)


def _ffn_weights(w_gate, w_up, w_down):
    d, ff = w_gate.shape
    nc = ff // FF_CHUNK
    wg = w_gate.astype(BF16).reshape(d, nc, FF_CHUNK).transpose(1, 0, 2)
    wu = w_up.astype(BF16).reshape(d, nc, FF_CHUNK).transpose(1, 0, 2)
    wd = w_down.astype(BF16).reshape(nc, FF_CHUNK, d)
    return wg, wu, wd


def _split3(x):
    hi = x.astype(BF16)
    r = x - hi.astype(F32)
    mid = r.astype(BF16)
    lo = (r - mid.astype(F32)).astype(BF16)
    return hi, mid, lo


def _dot3(tri_bf16, x):
    hi, mid, lo = _split3(x)
    d = functools.partial(jnp.dot, tri_bf16, preferred_element_type=F32)
    return d(hi) + d(mid) + d(lo)


def _softplus(x):
    return jnp.maximum(x, 0.0) + jnp.log(1.0 + jnp.exp(-jnp.abs(x)))


def _small_cols(small, bias_ref, aneg_ref):
    lane = lax.broadcasted_iota(jnp.int32, small.shape, 1)
    pre = small + bias_ref[...]
    sp = _softplus(pre)
    logf = pre - sp
    g = aneg_ref[...] * sp
    beta = _sigmoid(small)
    gb = jnp.where(lane < FOX_HEADS + GDN_HEADS, g, beta)
    return logf, gb


def _l2n_heads(x):
    outs = []
    for h in range(x.shape[1] // GDN_HEAD_DIM):
        xh = x[:, h * GDN_HEAD_DIM:(h + 1) * GDN_HEAD_DIM]
        outs.append(xh * lax.rsqrt(jnp.sum(xh * xh, axis=-1, keepdims=True) + EPS))
    return jnp.concatenate(outs, axis=-1)


def _proj_prompt_body(x_ref, gpre_ref, wq_ref, wkT_ref, wvT_ref, ws_ref, wc_ref, wz_ref, wga_ref, wgb_ref,
                      bias_ref, aneg_ref, convw_ref, tri_ref,
                      q_ref, kT_ref, vT_ref, lfT_ref, cT_ref, gbT_ref, col_ref, qkv_ref, cst_ref,
                      z_ref, ga_ref, gb_ref, ubuf, ccarry, *, tiles_per_seq):
    i = pl.program_id(0)
    tm = x_ref.shape[0]

    @pl.when(i % tiles_per_seq == 0)
    def _():
        ubuf[0:8, :] = jnp.zeros((8, ubuf.shape[1]), F32)
        ccarry[...] = jnp.zeros(ccarry.shape, F32)

    h = _rms(x_ref[...], gpre_ref[...]).astype(BF16)
    dot = functools.partial(jnp.dot, h, preferred_element_type=F32)
    nt = (((1,), (1,)), ((), ()))

    q_ref[...] = (dot(wq_ref[...]) * (FOX_HEAD_DIM ** -0.5)).astype(BF16)
    kT_ref[0] = lax.dot_general(wkT_ref[...], h, nt, preferred_element_type=F32)
    vT_ref[0] = lax.dot_general(wvT_ref[...], h, nt, preferred_element_type=F32)
    z_ref[...] = dot(wz_ref[...])
    ga_ref[...] = _sigmoid(dot(wga_ref[...]))
    gb_ref[...] = _sigmoid(dot(wgb_ref[...]))

    logf, gb = _small_cols(dot(ws_ref[...]), bias_ref, aneg_ref)
    carry = ccarry[...]
    blocks = []
    for r in range(tm // LANES):
        cb = _dot3(tri_ref[...], logf[r * LANES:(r + 1) * LANES, :]) + carry
        carry = cb[LANES - 1:LANES, :]
        blocks.append(cb)
    ccarry[...] = carry
    c = jnp.concatenate(blocks, axis=0)
    lane = lax.broadcasted_iota(jnp.int32, c.shape, 1)
    col = jnp.where(lane < FOX_HEADS, c, gb)
    col_ref[...] = col
    colT = col.T
    lfT_ref[0] = logf.T[0:FOX_HEADS, :]
    cT_ref[0] = colT[0:FOX_HEADS, :]
    gbT_ref[0] = colT[FOX_HEADS:2 * FOX_HEADS, :]

    ubuf[8:8 + tm, :] = dot(wc_ref[...])
    base = 8 - (CONV_WIDTH - 1)
    conv = ubuf[base:base + tm, :] * convw_ref[0:1, :]
    for k in range(1, CONV_WIDTH):
        conv = conv + ubuf[base + k:base + k + tm, :] * convw_ref[k:k + 1, :]
    cst_ref[0] = ubuf[tm + base:tm + 8, :]
    ubuf[0:8, :] = ubuf[tm:tm + 8, :]
    conv = _silu(conv)
    qkv_ref[:, 0:2 * GDN_WIDTH] = _l2n_heads(conv[:, 0:2 * GDN_WIDTH])
    qkv_ref[:, 2 * GDN_WIDTH:] = conv[:, 2 * GDN_WIDTH:]


def _proj_prompt(x, g_pre, w, batch, seq, tm):
    n, d = x.shape
    tps = seq // tm
    row = lambda width: pl.BlockSpec((tm, width), lambda i: (i, 0))
    seqT = lambda rows: pl.BlockSpec((1, rows, tm), lambda i: (i // tps, 0, i % tps))
    consts = [g_pre, w["q"], w["kT"], w["vT"], w["small"], w["conv"], w["z"], w["ga"], w["gb"],
              w["bias"], w["aneg"], w["convw"], w["tri"]]
    out_shape = [
        jax.ShapeDtypeStruct((n, FOX_WIDTH), BF16),
        jax.ShapeDtypeStruct((batch, FOX_WIDTH, seq), F32),
        jax.ShapeDtypeStruct((batch, FOX_WIDTH, seq), F32),
        jax.ShapeDtypeStruct((batch, FOX_HEADS, seq), F32),
        jax.ShapeDtypeStruct((batch, FOX_HEADS, seq), F32),
        jax.ShapeDtypeStruct((batch, 2 * GDN_HEADS, seq), F32),
        jax.ShapeDtypeStruct((n, LANES), F32),
        jax.ShapeDtypeStruct((n, CONV_CH), F32),
        jax.ShapeDtypeStruct((batch, CONV_WIDTH - 1, CONV_CH), F32),
        jax.ShapeDtypeStruct((n, GDN_WIDTH), F32),
        jax.ShapeDtypeStruct((n, d), F32),
        jax.ShapeDtypeStruct((n, d), F32),
    ]
    out_specs = [row(FOX_WIDTH), seqT(FOX_WIDTH), seqT(FOX_WIDTH), seqT(FOX_HEADS), seqT(FOX_HEADS),
                 seqT(2 * GDN_HEADS), row(LANES), row(CONV_CH),
                 pl.BlockSpec((1, CONV_WIDTH - 1, CONV_CH), lambda i: (i // tps, 0, 0)),
                 row(GDN_WIDTH), row(d), row(d)]
    return pl.pallas_call(
        functools.partial(_proj_prompt_body, tiles_per_seq=tps),
        grid=(n // tm,),
        in_specs=[row(d)] + [_const_spec(a.shape) for a in consts],
        out_specs=out_specs,
        out_shape=out_shape,
        scratch_shapes=[pltpu.VMEM((tm + 8, CONV_CH), F32), pltpu.VMEM((1, LANES), F32)],
        compiler_params=pltpu.CompilerParams(dimension_semantics=("arbitrary",), vmem_limit_bytes=VMEM_LIMIT),
        name="proj_prompt",
    )(x, *consts)


def _mixer_weights(w_in, b_f, conv_w, dt_bias, a_log):
    sizes = (FOX_WIDTH, FOX_WIDTH, FOX_WIDTH, FOX_HEADS, CONV_CH, GDN_HEADS, GDN_HEADS, GDN_WIDTH)
    d = w_in.shape[0]
    offs = [0]
    for s in sizes:
        offs.append(offs[-1] + s)
    part = lambda k: w_in[:, offs[k]:offs[k + 1]]
    ga0 = offs[-1]
    n_small = FOX_HEADS + 2 * GDN_HEADS
    small = jnp.concatenate([part(3), part(5), part(6), jnp.zeros((d, LANES - n_small), w_in.dtype)], axis=1)
    pad = lambda v, lo: jnp.zeros((1, LANES), F32).at[0, lo:lo + v.shape[0]].set(v.astype(F32))
    r = lax.broadcasted_iota(jnp.int32, (LANES, LANES), 0)
    c = lax.broadcasted_iota(jnp.int32, (LANES, LANES), 1)
    return {
        "q": part(0).astype(BF16), "kT": part(1).T.astype(BF16), "vT": part(2).T.astype(BF16),
        "small": small.astype(BF16), "conv": part(4).astype(BF16), "z": part(7).astype(BF16),
        "ga": w_in[:, ga0:ga0 + d].astype(BF16), "gb": w_in[:, ga0 + d:ga0 + 2 * d].astype(BF16),
        "bias": pad(b_f, 0) + pad(dt_bias, FOX_HEADS),
        "aneg": pad(-jnp.exp(a_log.astype(F32)), FOX_HEADS),
        "convw": conv_w.astype(F32),
        "tri": (c <= r).astype(BF16),
    }


NEG_BIG = -1e30


def _fox_prompt_body(q_ref, col_ref, kT_ref, vT_ref, cT_ref, o_ref, kb_ref, vb_ref, *, tq):
    p = pl.program_id(1)
    qi = pl.program_id(2)

    @pl.when(qi == 0)
    def _():
        kb_ref[...] = kT_ref[0].astype(BF16)
        vb_ref[...] = vT_ref[0].astype(BF16)

    nt = (((1,), (1,)), ((), ()))
    col = col_ref[...]
    lane = lax.broadcasted_iota(jnp.int32, col.shape, 1)
    hrows = [slice(e * FOX_HEAD_DIM, (e + 1) * FOX_HEAD_DIM) for e in range(2)]
    qs = [q_ref[:, r] for r in hrows]
    cqs = [jnp.sum(jnp.where(lane == 2 * p + e, col, 0.0), axis=-1, keepdims=True) for e in range(2)]

    def block(kb, carry, masked):
        ks = pl.ds(pl.multiple_of(kb * tq, tq), tq)
        raw = [jnp.dot(qs[e], kb_ref[hrows[e], ks], preferred_element_type=F32) for e in range(2)]
        stats = []
        for e in range(2):
            m, l, _ = carry[e]
            s = raw[e] + cqs[e] - cT_ref[0, pl.ds(2 * p + e, 1), ks]
            if masked:
                r = lax.broadcasted_iota(jnp.int32, s.shape, 0)
                c = lax.broadcasted_iota(jnp.int32, s.shape, 1)
                s = jnp.where(c <= r, s, NEG_BIG)
            m_new = jnp.maximum(m, jnp.max(s, axis=-1, keepdims=True))
            pr = jnp.exp(s - m_new)
            alpha = jnp.exp(m - m_new)
            stats.append((m_new, alpha * l + jnp.sum(pr, axis=-1, keepdims=True), alpha, pr.astype(BF16)))
        pvs = [lax.dot_general(stats[e][3], vb_ref[hrows[e], ks], nt, preferred_element_type=F32) for e in range(2)]
        return tuple((stats[e][0], stats[e][1], stats[e][2] * carry[e][2] + pvs[e]) for e in range(2))

    init = tuple((jnp.full((tq, 1), NEG_BIG, F32), jnp.zeros((tq, 1), F32), jnp.zeros((tq, FOX_HEAD_DIM), F32))
                 for _ in range(2))
    carry = lax.fori_loop(0, qi, lambda kb, c: block(kb, c, False), init)
    carry = block(qi, carry, True)
    o_ref[...] = jnp.concatenate([acc / l for (_, l, acc) in carry], axis=-1).astype(o_ref.dtype)


def _proj_sample_body(x_ref, gpre_ref, wqT_ref, wkT_ref, wvT_ref, ws_ref, wc_ref, wcT_ref, wz_ref, wga_ref, wgb_ref,
                      bias_ref, aneg_ref, convw_ref, convwT_ref, buf_ref, bufT_ref,
                      qT_ref, kT_ref, vT_ref, small_ref, smallT_ref, u_ref, qkT_ref, vg_ref, z_ref, ga_ref, gb_ref):
    h = _rms(x_ref[...], gpre_ref[...]).astype(BF16)
    dot = functools.partial(jnp.dot, h, preferred_element_type=F32)
    dot_t = lambda w: lax.dot_general(w, h, (((1,), (1,)), ((), ())), preferred_element_type=F32)
    qT_ref[...] = dot_t(wqT_ref[...]) * (FOX_HEAD_DIM ** -0.5)
    kT_ref[...] = dot_t(wkT_ref[...])
    vT_ref[...] = dot_t(wvT_ref[...])
    z_ref[...] = dot(wz_ref[...])
    ga_ref[...] = _sigmoid(dot(wga_ref[...]))
    gb_ref[...] = _sigmoid(dot(wgb_ref[...]))

    logf, gb = _small_cols(dot(ws_ref[...]), bias_ref, aneg_ref)
    lane = lax.broadcasted_iota(jnp.int32, logf.shape, 1)
    small = jnp.where(lane < FOX_HEADS, logf, gb)
    small_ref[...] = small
    pad = jnp.concatenate([small, jnp.zeros((LANES - small.shape[0], LANES), F32)], axis=0)
    smallT_ref[...] = pad.T[:, 0:small.shape[0]]

    u = dot(wc_ref[...])
    u_ref[...] = u
    vs = slice(2 * GDN_WIDTH, CONV_CH)
    conv_v = u[:, vs] * convw_ref[CONV_WIDTH - 1:CONV_WIDTH, vs]
    conv_qk = dot_t(wcT_ref[...]) * convwT_ref[:, CONV_WIDTH - 1:CONV_WIDTH]
    for k in range(CONV_WIDTH - 1):
        conv_v = conv_v + buf_ref[k][:, vs] * convw_ref[k:k + 1, vs]
        conv_qk = conv_qk + bufT_ref[k] * convwT_ref[:, k:k + 1]
    vg_ref[...] = _silu(conv_v)
    conv_qk = _silu(conv_qk)
    for hd in range(2 * GDN_HEADS):
        rs = slice(hd * GDN_HEAD_DIM, (hd + 1) * GDN_HEAD_DIM)
        xh = conv_qk[rs, :]
        qkT_ref[rs, :] = xh * lax.rsqrt(jnp.sum(xh * xh, axis=0, keepdims=True) + EPS)


def _proj_sample(x, g_pre, w, buf, bufT):
    db, d = x.shape
    ins = [x, g_pre, w["q"].T, w["kT"], w["vT"], w["small"], w["conv"], w["conv"][:, :2 * GDN_WIDTH].T, w["z"],
           w["ga"], w["gb"], w["bias"], w["aneg"], w["convw"], w["convw"][:, :2 * GDN_WIDTH].T, buf, bufT]
    out_shape = [
        jax.ShapeDtypeStruct((FOX_WIDTH, db), F32), jax.ShapeDtypeStruct((FOX_WIDTH, db), F32),
        jax.ShapeDtypeStruct((FOX_WIDTH, db), F32),
        jax.ShapeDtypeStruct((db, LANES), F32), jax.ShapeDtypeStruct((LANES, db), F32),
        jax.ShapeDtypeStruct((db, CONV_CH), F32), jax.ShapeDtypeStruct((2 * GDN_WIDTH, db), F32),
        jax.ShapeDtypeStruct((db, GDN_WIDTH), F32), jax.ShapeDtypeStruct((db, GDN_WIDTH), F32),
        jax.ShapeDtypeStruct((db, d), F32), jax.ShapeDtypeStruct((db, d), F32),
    ]
    return pl.pallas_call(
        _proj_sample_body, out_shape=out_shape,
        compiler_params=pltpu.CompilerParams(vmem_limit_bytes=VMEM_LIMIT),
        name="proj_sample",
    )(*ins)


def _take_lane(x, idx):
    lane = lax.broadcasted_iota(jnp.int32, x.shape, 1)
    return jnp.sum(jnp.where(lane == idx, x, 0.0), axis=-1, keepdims=True)


PAGES_PER_STEP = 8


def _fox_decode_body(pt_ref, qT_ref, kTn_ref, vTn_ref, smallT_ref, sfxw_ref, *rest, page):
    g_pages = PAGES_PER_STEP
    k_refs = rest[0:g_pages]
    v_refs = rest[g_pages:2 * g_pages]
    lf_refs = rest[2 * g_pages:3 * g_pages]
    o_ref, qb_s, acc_s, m_s, l_s, sfx_s, al_s, p_s = rest[3 * g_pages:]
    b = pl.program_id(0)
    j = pl.program_id(1)
    hd = FOX_HEAD_DIM

    def per_head_dot(a, bm):
        return jnp.concatenate(
            [jnp.sum(a[h * hd:(h + 1) * hd, :] * bm[h * hd:(h + 1) * hd, :], axis=0, keepdims=True)
             for h in range(FOX_HEADS)], axis=0)

    @pl.when(j == 0)
    def _():
        qb = jnp.broadcast_to(_take_lane(qT_ref[...], b), (FOX_WIDTH, page))
        kn = jnp.broadcast_to(_take_lane(kTn_ref[...], b), (FOX_WIDTH, page))
        vn = jnp.broadcast_to(_take_lane(vTn_ref[...], b), (FOX_WIDTH, page))
        qb_s[...] = qb
        m_s[...] = per_head_dot(qb, kn)
        l_s[...] = jnp.ones(l_s.shape, F32)
        lane_w = lax.broadcasted_iota(jnp.int32, (FOX_WIDTH, page), 1)
        acc_s[...] = jnp.where(lane_w == 0, vn, 0.0)
        sfx_s[...] = jnp.broadcast_to(_take_lane(smallT_ref[0:FOX_HEADS, :], b), sfx_s.shape)

    lfs = jnp.concatenate([lf_refs[g][0] for g in range(g_pages)], axis=0)
    hi, mid, lo = _split3(lfs)
    wdot = lambda a: jnp.dot(a, sfxw_ref[...], preferred_element_type=F32)
    red = wdot(hi) + wdot(mid) + wdot(lo)
    carry = sfx_s[...]
    scores = []
    for g in range(g_pages):
        hs = slice(g * FOX_HEADS, (g + 1) * FOX_HEADS)
        bias = carry + (red[hs, 0:page] - lfs[hs, :])
        carry = carry + red[hs, page:2 * page]
        kt = k_refs[g][0].reshape(FOX_WIDTH, page)
        scores.append(per_head_dot(kt, qb_s[...]) + bias)
    sfx_s[...] = carry

    m_old = m_s[...]
    smax = scores[0]
    for s in scores[1:]:
        smax = jnp.maximum(smax, s)
    m_new = jnp.maximum(m_old, jnp.max(smax, axis=-1, keepdims=True))
    al_s[...] = jnp.exp(m_old - m_new)
    psum = None
    for g, s in enumerate(scores):
        p = jnp.exp(s - m_new)
        p_s[g] = p
        psum = p if psum is None else psum + p
    m_s[...] = m_new
    l_s[...] = al_s[...] * l_s[...] + jnp.sum(psum, axis=-1, keepdims=True)
    for h in range(FOX_HEADS):
        rs = slice(h * hd, (h + 1) * hd)
        acc = acc_s[rs, :] * al_s[h:h + 1, :]
        for g in range(g_pages):
            acc = acc + p_s[g, h:h + 1, :] * v_refs[g][0, h]
        acc_s[rs, :] = acc

    @pl.when(j == pl.num_programs(1) - 1)
    def _():
        inv_l = 1.0 / l_s[...]
        for h in range(FOX_HEADS):
            rs = slice(h * hd, (h + 1) * hd)
            o_ref[0, rs, :] = jnp.sum(acc_s[rs, :], axis=-1, keepdims=True) * inv_l[h:h + 1, 0:1]


def _fox_decode(page_table, qT, kTn, vTn, smallT, cache_kT, cache_vT, cache_lfT):
    db, n_pages = page_table.shape
    page = cache_kT.shape[-1]
    g_pages = PAGES_PER_STEP
    n_steps = n_pages // g_pages
    const = lambda shape: pl.BlockSpec(shape, lambda b, j, pt: (0,) * len(shape))

    def page_idx(b, j, pt, g):
        return pt[b * n_pages + (n_steps - 1 - j) * g_pages + (g_pages - 1 - g)]

    kv_spec = lambda g: pl.BlockSpec((1, FOX_HEADS, FOX_HEAD_DIM, page),
                                     lambda b, j, pt: (page_idx(b, j, pt, g), 0, 0, 0))
    lf_spec = lambda g: pl.BlockSpec((1, FOX_HEADS, page), lambda b, j, pt: (page_idx(b, j, pt, g), 0, 0))
    ii = lax.broadcasted_iota(jnp.int32, (page, 2 * page), 0)
    jj = lax.broadcasted_iota(jnp.int32, (page, 2 * page), 1)
    sfxw = ((ii >= jj) | (jj >= page)).astype(BF16)
    in_specs = ([const(qT.shape), const(kTn.shape), const(vTn.shape), const(smallT.shape), const(sfxw.shape)]
                + [kv_spec(g) for g in range(g_pages)] + [kv_spec(g) for g in range(g_pages)]
                + [lf_spec(g) for g in range(g_pages)])
    out = pl.pallas_call(
        functools.partial(_fox_decode_body, page=page),
        grid_spec=pltpu.PrefetchScalarGridSpec(
            num_scalar_prefetch=1, grid=(db, n_steps), in_specs=in_specs,
            out_specs=pl.BlockSpec((1, FOX_WIDTH, 1), lambda b, j, pt: (b, 0, 0)),
            scratch_shapes=[pltpu.VMEM((FOX_WIDTH, page), F32), pltpu.VMEM((FOX_WIDTH, page), F32),
                            pltpu.VMEM((FOX_HEADS, page), F32), pltpu.VMEM((FOX_HEADS, page), F32),
                            pltpu.VMEM((FOX_HEADS, page), F32), pltpu.VMEM((FOX_HEADS, page), F32),
                            pltpu.VMEM((g_pages, FOX_HEADS, page), F32)]),
        out_shape=jax.ShapeDtypeStruct((db, FOX_WIDTH, 1), F32),
        compiler_params=pltpu.CompilerParams(dimension_semantics=("arbitrary",) * 2, vmem_limit_bytes=VMEM_LIMIT),
        name="fox_decode",
    )(page_table.reshape(-1), qT, kTn, vTn, smallT, sfxw, *([cache_kT] * g_pages), *([cache_vT] * g_pages),
      *([cache_lfT] * g_pages))
    return out.reshape(db, FOX_WIDTH)


def _gdn_decode_body(qkT_ref, vg_ref, smallT_ref, s_ref, o_ref, snew_ref):
    b = pl.program_id(0)
    qk = _take_lane(qkT_ref[...], b)
    gbeta = _take_lane(smallT_ref[FOX_HEADS:FOX_HEADS + 2 * GDN_HEADS, :], b)
    for h in range(GDN_HEADS):
        rs = slice(h * GDN_HEAD_DIM, (h + 1) * GDN_HEAD_DIM)
        q = qk[rs, :] * (GDN_HEAD_DIM ** -0.5)
        k = qk[GDN_WIDTH + h * GDN_HEAD_DIM:GDN_WIDTH + (h + 1) * GDN_HEAD_DIM, :]
        v = vg_ref[0, :, rs]
        st = s_ref[0, h] * jnp.exp(gbeta[h:h + 1, :])
        delta = (v - jnp.sum(k * st, axis=0, keepdims=True)) * gbeta[GDN_HEADS + h:GDN_HEADS + h + 1, :]
        st = st + k * delta
        snew_ref[0, h] = st
        o_ref[0, :, rs] = jnp.sum(q * st, axis=0, keepdims=True)


def _gdn_decode(qkT, vg, smallT, state):
    db = vg.shape[0]
    const = lambda shape: pl.BlockSpec(shape, lambda b: (0,) * len(shape))
    st_spec = pl.BlockSpec((1, GDN_HEADS, GDN_HEAD_DIM, GDN_HEAD_DIM), lambda b: (b, 0, 0, 0))
    row_spec = pl.BlockSpec((1, 1, GDN_WIDTH), lambda b: (b, 0, 0))
    o, s_new = pl.pallas_call(
        _gdn_decode_body,
        grid=(db,),
        in_specs=[const(qkT.shape), row_spec, const(smallT.shape), st_spec],
        out_specs=[row_spec, st_spec],
        out_shape=[jax.ShapeDtypeStruct((db, 1, GDN_WIDTH), F32), jax.ShapeDtypeStruct(state.shape, F32)],
        compiler_params=pltpu.CompilerParams(dimension_semantics=("arbitrary",)),
        name="gdn_decode",
    )(qkT, vg.reshape(db, 1, GDN_WIDTH), smallT, state)
    return o.reshape(db, GDN_WIDTH), s_new


def _mix_out_body(x_ref, oa_ref, og_ref, z_ref, ga_ref, gb_ref, ggdn_ref, wbf_ref, wbg_ref, wout_ref, gpost_ref,
                  o_ref):
    og = og_ref[...]
    z = z_ref[...]
    parts = []
    for h in range(GDN_HEADS):
        ls = slice(h * GDN_HEAD_DIM, (h + 1) * GDN_HEAD_DIM)
        parts.append(_rms(og[:, ls], ggdn_ref[...]) * _silu(z[:, ls]))
    ogn = jnp.concatenate(parts, axis=-1).astype(BF16)
    y_a = jnp.dot(oa_ref[...], wbf_ref[...], preferred_element_type=F32)
    y_g = jnp.dot(ogn, wbg_ref[...], preferred_element_type=F32)
    y = (ga_ref[...] * y_a + gb_ref[...] * y_g).astype(BF16)
    y = jnp.dot(y, wout_ref[...], preferred_element_type=F32)
    o_ref[...] = x_ref[...] + _rms(y, gpost_ref[...])


def _mix_out(x, o_a, o_g, z, ga, gb, g_gdn, wbf, wbg, wout, g_post, tm):
    n, d = x.shape
    row = lambda width: pl.BlockSpec((tm, width), lambda i: (i, 0))
    consts = [g_gdn, wbf, wbg, wout, g_post]
    return pl.pallas_call(
        _mix_out_body,
        grid=(pl.cdiv(n, tm),),
        in_specs=[row(d), row(FOX_WIDTH), row(GDN_WIDTH), row(GDN_WIDTH), row(d), row(d)]
        + [_const_spec(a.shape) for a in consts],
        out_specs=row(d),
        out_shape=jax.ShapeDtypeStruct((n, d), F32),
        compiler_params=pltpu.CompilerParams(dimension_semantics=("arbitrary",), vmem_limit_bytes=VMEM_LIMIT),
        name="mix_out",
    )(x, o_a, o_g, z, ga, gb, *consts)


def _chunk_gates(col_ref, tri_ref, rows, heads, lane, rr, cc):
    colc = col_ref[rows, :]
    cs = _dot3(tri_ref[...], colc)
    keep = cc <= rr
    out = []
    for head in heads:
        gc = jnp.sum(jnp.where(lane == FOX_HEADS + head, cs, 0.0), axis=-1, keepdims=True)
        beta = jnp.sum(jnp.where(lane == FOX_HEADS + GDN_HEADS + head, colc, 0.0), axis=-1, keepdims=True)
        cmat = jnp.broadcast_to(gc, (CHUNK, CHUNK))
        decay = jnp.where(keep, jnp.exp(jnp.where(keep, cmat - cmat.T, 0.0)), 0.0)
        out.append((gc, beta, decay))
    return out


def _gdn_lower_body(k_ref, col_ref, tri_ref, l_ref):
    n_chunks = k_ref.shape[0] // CHUNK
    nt = (((1,), (1,)), ((), ()))
    rr = lax.broadcasted_iota(jnp.int32, (CHUNK, CHUNK), 0)
    cc = lax.broadcasted_iota(jnp.int32, (CHUNK, CHUNK), 1)
    lane = lax.broadcasted_iota(jnp.int32, (CHUNK, LANES), 1)

    def chunk_pair(i, _):
        work = []
        for c in range(2):
            n = 2 * i + c
            rows = pl.ds(pl.multiple_of(n * CHUNK, CHUNK), CHUNK)
            work.append((n, rows, _chunk_gates(col_ref, tri_ref, rows, range(GDN_HEADS), lane, rr, cc)))
        prods = []
        for n, rows, gates in work:
            for e, (_, beta, decay) in enumerate(gates):
                k = k_ref[rows, e * GDN_HEAD_DIM:(e + 1) * GDN_HEAD_DIM]
                kk = lax.dot_general((k * beta).astype(BF16), k.astype(BF16), nt, preferred_element_type=F32)
                prods.append((n, e, kk, decay))
        for n, e, kk, decay in prods:
            l_ref[0, e, n] = jnp.where(cc < rr, kk * decay, 0.0)
        return 0

    lax.fori_loop(0, n_chunks // 2, chunk_pair, 0)


def _unit_lower_inverse_body(l_ref, t_ref, lt_s, tt_s):
    for i in range(CHUNK):
        lt_s[i * CHUNK:(i + 1) * CHUNK, :] = l_ref[0, :, i, :].T
    sub = lax.broadcasted_iota(jnp.int32, (8, LANES), 0)
    for i in range(CHUNK):
        for kg in range(i // 8 + 1):
            acc = jnp.where(sub + 8 * kg == i, 1.0, 0.0)
            for j in range(8 * kg, i):
                acc = acc - lt_s[i * CHUNK + j:i * CHUNK + j + 1, :] * tt_s[j * CHUNK + 8 * kg:j * CHUNK + 8 * kg + 8, :]
            tt_s[i * CHUNK + 8 * kg:i * CHUNK + 8 * kg + 8, :] = acc
        for kg in range(i // 8 + 1, CHUNK // 8):
            tt_s[i * CHUNK + 8 * kg:i * CHUNK + 8 * kg + 8, :] = jnp.zeros((8, LANES), F32)
    for i in range(CHUNK):
        t_ref[0, :, i, :] = tt_s[i * CHUNK:(i + 1) * CHUNK, :].T


def _unit_lower_inverse(low):
    groups = low.shape[0]
    spec = pl.BlockSpec((1, LANES, CHUNK, CHUNK), lambda g: (g, 0, 0, 0))
    return pl.pallas_call(
        _unit_lower_inverse_body,
        grid=(groups,),
        in_specs=[spec], out_specs=spec,
        out_shape=jax.ShapeDtypeStruct(low.shape, F32),
        scratch_shapes=[pltpu.VMEM((CHUNK * CHUNK, LANES), F32), pltpu.VMEM((CHUNK * CHUNK, LANES), F32)],
        compiler_params=pltpu.CompilerParams(dimension_semantics=("arbitrary",), vmem_limit_bytes=VMEM_LIMIT),
        name="unit_lower_inverse",
    )(low)


def _gdn_prompt_body(q_ref, k_ref, v_ref, col_ref, tri_ref, t_ref, o_ref, s_ref, state_s,
                     u_s, w_s, qe_s, ke_s, at_s, eg_s):
    seg = pl.program_id(1)
    n_chunks = q_ref.shape[0] // CHUNK
    nt = (((1,), (1,)), ((), ()))
    tn = (((0,), (0,)), ((), ()))
    rr = lax.broadcasted_iota(jnp.int32, (CHUNK, CHUNK), 0)
    cc = lax.broadcasted_iota(jnp.int32, (CHUNK, CHUNK), 1)
    lane = lax.broadcasted_iota(jnp.int32, (CHUNK, LANES), 1)

    @pl.when(seg == 0)
    def _():
        state_s[...] = jnp.zeros(state_s.shape, F32)

    heads = range(GDN_HEADS)
    hl = [slice(e * GDN_HEAD_DIM, (e + 1) * GDN_HEAD_DIM) for e in heads]
    fdot = functools.partial(jnp.dot, preferred_element_type=F32)

    def prepare(i, _):
        work = []
        for c in range(2):
            n = 2 * i + c
            rows = pl.ds(pl.multiple_of(n * CHUNK, CHUNK), CHUNK)
            work.append((n, rows, _chunk_gates(col_ref, tri_ref, rows, heads, lane, rr, cc)))
        pending = []
        for n, rows, gates in work:
            for e, (gc, beta, decay) in enumerate(gates):
                q = q_ref[rows, hl[e]] * (GDN_HEAD_DIM ** -0.5)
                k = k_ref[rows, hl[e]]
                v = v_ref[rows, hl[e]]
                egc = jnp.exp(gc)
                g_last = gc[CHUNK - 1:CHUNK, :]
                tinv = t_ref[0, e, n]
                t_hi = tinv.astype(BF16)
                t_lo = (tinv - t_hi.astype(F32)).astype(BF16)
                rhs = jnp.concatenate([v * beta, k * (beta * egc)], axis=1)
                r_hi = rhs.astype(BF16)
                r_lo = (rhs - r_hi.astype(F32)).astype(BF16)
                uw = fdot(jnp.concatenate([t_hi, t_lo, t_hi], axis=1), jnp.concatenate([r_hi, r_hi, r_lo], axis=0))
                attn = lax.dot_general(q.astype(BF16), k.astype(BF16), nt, preferred_element_type=F32)
                pending.append((n, rows, e, uw, attn, decay))
                qe_s[rows, hl[e]] = (q * egc).astype(BF16)
                ke_s[e, n] = (k * jnp.exp(g_last - gc)).T.astype(BF16)
                eg_s[e, pl.ds(n, 1), :] = jnp.broadcast_to(jnp.exp(g_last), (1, LANES))
        for n, rows, e, uw, attn, decay in pending:
            u_s[rows, hl[e]] = uw[:, :GDN_HEAD_DIM]
            w_s[rows, hl[e]] = uw[:, GDN_HEAD_DIM:].astype(BF16)
            at_s[e, rows, :] = (attn * decay).astype(BF16)
        return 0

    lax.fori_loop(0, n_chunks // 2, prepare, 0)

    def scan(n, states):
        rows = pl.ds(pl.multiple_of(n * CHUNK, CHUNK), CHUNK)
        sbs = [states[e].astype(BF16) for e in heads]
        ws = [fdot(w_s[rows, hl[e]], sbs[e]) for e in heads]
        qs = [fdot(qe_s[rows, hl[e]], sbs[e]) for e in heads]
        v_new = [(u_s[rows, hl[e]] - ws[e]).astype(BF16) for e in heads]
        av = [fdot(at_s[e, rows, :], v_new[e]) for e in heads]
        kv = [fdot(ke_s[e, n], v_new[e]) for e in heads]
        for e in heads:
            o_ref[rows, hl[e]] = qs[e] + av[e]
        return tuple(states[e] * eg_s[e, pl.ds(n, 1), :] + kv[e] for e in heads)

    init = tuple(state_s[e] for e in range(GDN_HEADS))
    final = lax.fori_loop(0, n_chunks, scan, init)
    for e in range(GDN_HEADS):
        state_s[e] = final[e]
        s_ref[0, e] = final[e]


GDN_SEGMENT = 512


def _gdn_prompt(qkv, col, tri64, batch, seq):
    n = qkv.shape[0]
    n_chunks = seq // CHUNK
    n_seg = seq // GDN_SEGMENT
    seg_chunks = GDN_SEGMENT // CHUNK
    blk = lambda off: pl.BlockSpec((GDN_SEGMENT, GDN_WIDTH), lambda b, s: (b * n_seg + s, off))
    col_spec = pl.BlockSpec((GDN_SEGMENT, LANES), lambda b, s: (b * n_seg + s, 0))
    tri_spec = pl.BlockSpec((CHUNK, CHUNK), lambda b, s: (0, 0))
    mat_spec = pl.BlockSpec((1, GDN_HEADS, seg_chunks, CHUNK, CHUNK), lambda b, s: (b, 0, s, 0, 0))
    params = pltpu.CompilerParams(dimension_semantics=("arbitrary",) * 2, vmem_limit_bytes=VMEM_LIMIT)
    low = pl.pallas_call(
        _gdn_lower_body,
        grid=(batch, n_seg),
        in_specs=[blk(1), col_spec, tri_spec],
        out_specs=mat_spec,
        out_shape=jax.ShapeDtypeStruct((batch, GDN_HEADS, n_chunks, CHUNK, CHUNK), F32),
        compiler_params=params,
        name="gdn_lower",
    )(qkv, col, tri64)
    n_sys = batch * GDN_HEADS * n_chunks
    tinv = _unit_lower_inverse(low.reshape(n_sys // LANES, LANES, CHUNK, CHUNK)).reshape(low.shape)
    state_spec = pl.BlockSpec((1, GDN_HEADS, GDN_HEAD_DIM, GDN_HEAD_DIM), lambda b, s: (b, 0, 0, 0))
    return pl.pallas_call(
        _gdn_prompt_body,
        grid=(batch, n_seg),
        in_specs=[blk(0), blk(1), blk(2), col_spec, tri_spec, mat_spec],
        out_specs=[blk(0), state_spec],
        out_shape=[jax.ShapeDtypeStruct((n, GDN_WIDTH), F32),
                   jax.ShapeDtypeStruct((batch, GDN_HEADS, GDN_HEAD_DIM, GDN_HEAD_DIM), F32)],
        scratch_shapes=[pltpu.VMEM((GDN_HEADS, GDN_HEAD_DIM, GDN_HEAD_DIM), F32),
                        pltpu.VMEM((GDN_SEGMENT, GDN_WIDTH), F32), pltpu.VMEM((GDN_SEGMENT, GDN_WIDTH), BF16),
                        pltpu.VMEM((GDN_SEGMENT, GDN_WIDTH), BF16),
                        pltpu.VMEM((GDN_HEADS, seg_chunks, GDN_HEAD_DIM, CHUNK), BF16),
                        pltpu.VMEM((GDN_HEADS, GDN_SEGMENT, CHUNK), BF16),
                        pltpu.VMEM((GDN_HEADS, seg_chunks, LANES), F32)],
        compiler_params=params,
        name="gdn_prompt",
    )(qkv, qkv, qkv, col, tri64, tinv)


def _fox_prompt(q, col, kT, vT, cT, tq):
    n = q.shape[0]
    batch, _, seq = kT.shape
    nq = seq // tq
    pair = 2 * FOX_HEAD_DIM
    return pl.pallas_call(
        functools.partial(_fox_prompt_body, tq=tq),
        grid=(batch, FOX_HEADS // 2, nq),
        in_specs=[
            pl.BlockSpec((tq, pair), lambda b, p, i: (b * nq + i, p)),
            pl.BlockSpec((tq, LANES), lambda b, p, i: (b * nq + i, 0)),
            pl.BlockSpec((1, pair, seq), lambda b, p, i: (b, p, 0)),
            pl.BlockSpec((1, pair, seq), lambda b, p, i: (b, p, 0)),
            pl.BlockSpec((1, FOX_HEADS, seq), lambda b, p, i: (b, 0, 0)),
        ],
        out_specs=pl.BlockSpec((tq, pair), lambda b, p, i: (b * nq + i, p)),
        out_shape=jax.ShapeDtypeStruct((n, FOX_WIDTH), BF16),
        scratch_shapes=[pltpu.VMEM((pair, seq), BF16), pltpu.VMEM((pair, seq), BF16)],
        compiler_params=pltpu.CompilerParams(dimension_semantics=("arbitrary",) * 3, vmem_limit_bytes=VMEM_LIMIT),
        name="fox_prompt",
    )(q, col, kT, vT, cT)


ROW_TILE = 512
FOX_Q_TILE = 512


def kernel(x_prompt, x_sample, cache_k, cache_v, cache_logf, state_gdn, state_conv, page_table,
           ffn1_norm_pre, ffn1_w_gate, ffn1_w_up, ffn1_w_down, ffn1_norm_post,
           mix_norm_pre, w_in, fox_forget_bias, gdn_conv_w, gdn_dt_bias, gdn_a_log, gdn_out_norm,
           w_branch_fox, w_branch_gdn, w_out, mix_norm_post,
           ffn2_norm_pre, ffn2_w_gate, ffn2_w_up, ffn2_w_down, ffn2_norm_post):
    batch, seq, d = x_prompt.shape
    db = x_sample.shape[0]
    assert x_sample.shape[1] == 1, "sample group carries one new token per sequence"
    depth = w_in.shape[0]
    xp = x_prompt.reshape(batch * seq, d)
    xs = x_sample.reshape(db, d)
    row = lambda v: v.reshape(1, -1).astype(F32)
    st_p, st_s = [], []
    for l in range(depth):
        ffn1 = _ffn_weights(ffn1_w_gate[l], ffn1_w_up[l], ffn1_w_down[l])
        ffn2 = _ffn_weights(ffn2_w_gate[l], ffn2_w_up[l], ffn2_w_down[l])
        w = _mixer_weights(w_in[l], fox_forget_bias[l], gdn_conv_w[l], gdn_dt_bias[l], gdn_a_log[l])
        out_w = (row(gdn_out_norm[l]), w_branch_fox[l].astype(BF16), w_branch_gdn[l].astype(BF16),
                 w_out[l].astype(BF16), row(mix_norm_post[l]))

        xp = _ffn(xp, row(ffn1_norm_pre[l]), *ffn1, row(ffn1_norm_post[l]), ROW_TILE)
        (q, kT, vT, lfT, cT, _, col, qkv, conv_p, z, ga, gb) = _proj_prompt(
            xp, row(mix_norm_pre[l]), w, batch, seq, ROW_TILE)
        o_a = _fox_prompt(q, col, kT, vT, cT, FOX_Q_TILE)
        o_g, s_p = _gdn_prompt(qkv, col, w["tri"][:CHUNK, :CHUNK], batch, seq)
        xp = _mix_out(xp, o_a, o_g, z, ga, gb, *out_w, ROW_TILE)
        xp = _ffn(xp, row(ffn2_norm_pre[l]), *ffn2, row(ffn2_norm_post[l]), ROW_TILE)
        to_heads = lambda t: t.reshape(batch, FOX_HEADS, FOX_HEAD_DIM, seq).transpose(0, 3, 1, 2)
        st_p.append((to_heads(kT), to_heads(vT), lfT.transpose(0, 2, 1), s_p, conv_p))

        xs = _ffn(xs, row(ffn1_norm_pre[l]), *ffn1, row(ffn1_norm_post[l]), db)
        buf = state_conv[l].transpose(1, 0, 2)
        bufT = state_conv[l][:, :, :2 * GDN_WIDTH].transpose(1, 2, 0)
        (qTs, kTs, vTs, small, smallT, u, qkT, vg, zs, gas, gbs) = _proj_sample(
            xs, row(mix_norm_pre[l]), w, buf, bufT)
        o_as = _fox_decode(page_table, qTs, kTs, vTs, smallT,
                           cache_k[l].transpose(0, 2, 3, 1), cache_v[l].transpose(0, 2, 3, 1),
                           cache_logf[l].transpose(0, 2, 1))
        o_gs, s_s = _gdn_decode(qkT, vg, smallT, state_gdn[l])
        xs = _mix_out(xs, o_as.astype(BF16), o_gs, zs, gas, gbs, *out_w, db)
        xs = _ffn(xs, row(ffn2_norm_pre[l]), *ffn2, row(ffn2_norm_post[l]), db)
        new_conv = jnp.concatenate([state_conv[l][:, 1:], u[:, None, :]], axis=1)
        st_s.append((kTs.T.reshape(db, 1, FOX_HEADS, FOX_HEAD_DIM), vTs.T.reshape(db, 1, FOX_HEADS, FOX_HEAD_DIM),
                     small[:, :FOX_HEADS].reshape(db, 1, FOX_HEADS), s_s, new_conv))

    stack = lambda states, i: jnp.stack([s[i] for s in states], axis=0)
    return (xp.reshape(batch, seq, d), xs.reshape(db, 1, d),
            *[stack(st_p, i) for i in range(5)], *[stack(st_s, i) for i in range(5)])
```

```python
import functools

import jax
import jax.numpy as jnp
from jax import lax
from jax.experimental import pallas as pl
from jax.experimental.pallas import tpu as pltpu

F32 = jnp.float32
BF16 = jnp.bfloat16
EPS = 1e-6
LOG2E = 1.4426950408889634
LANES = 128
VMEM_LIMIT = 56 * 1024 * 1024

FOX_HEADS = 8
FOX_HEAD_DIM = 64
FOX_WIDTH = FOX_HEADS * FOX_HEAD_DIM
GDN_HEADS = 4
GDN_HEAD_DIM = 128
GDN_WIDTH = GDN_HEADS * GDN_HEAD_DIM
CONV_WIDTH = 4
CONV_CH = 3 * GDN_WIDTH
CHUNK = 64
FF_CHUNK = 256


def _rms(x, g):
    return x * lax.rsqrt(jnp.mean(x * x, axis=-1, keepdims=True) + EPS) * g


def _sigmoid(x):
    return 1.0 / (1.0 + jnp.exp(-x))


def _silu(x):
    return x * _sigmoid(x)


def _const_spec(shape):
    n = len(shape)
    return pl.BlockSpec(shape, lambda *_: (0,) * n, pipeline_mode=pl.Buffered(1))


def _ffn_body(x_ref, gpre_ref, wg_ref, wu_ref, wd_ref, gpost_ref, o_ref):
    x = x_ref[...]
    h = _rms(x, gpre_ref[...]).astype(BF16)
    n_chunks = wg_ref.shape[0]

    def chunk(j, acc):
        g = jnp.dot(h, wg_ref[j], preferred_element_type=F32)
        u = jnp.dot(h, wu_ref[j], preferred_element_type=F32)
        a = (_silu(g) * u).astype(BF16)
        return acc + jnp.dot(a, wd_ref[j], preferred_element_type=F32)

    y = lax.fori_loop(0, n_chunks, chunk, jnp.zeros(x.shape, F32), unroll=True)
    o_ref[...] = x + 0.5 * _rms(y, gpost_ref[...])


def _ffn(x, g_pre, wg, wu, wd, g_post, tm):
    n, d = x.shape
    row = pl.BlockSpec((tm, d), lambda i: (i, 0))
    return pl.pallas_call(
        _ffn_body,
        grid=(pl.cdiv(n, tm),),
        in_specs=[row, _const_spec((1, d)), _const_spec(wg.shape), _const_spec(wu.shape),
                  _const_spec(wd.shape), _const_spec((1, d))],
        out_specs=row,
        out_shape=jax.ShapeDtypeStruct((n, d), F32),
        compiler_params=pltpu.CompilerParams(dimension_semantics=("arbitrary",), vmem_limit_bytes=VMEM_LIMIT),
        name="ffn",
    )(x, g_pre, wg, wu, wd, g_post)


def _ffn_weights(w_gate, w_up, w_down):
    d, ff = w_gate.shape
    nc = ff // FF_CHUNK
    wg = w_gate.astype(BF16).reshape(d, nc, FF_CHUNK).transpose(1, 0, 2)
    wu = w_up.astype(BF16).reshape(d, nc, FF_CHUNK).transpose(1, 0, 2)
    wd = w_down.astype(BF16).reshape(nc, FF_CHUNK, d)
    return wg, wu, wd


def _split3(x):
    hi = x.astype(BF16)
    r = x - hi.astype(F32)
    mid = r.astype(BF16)
    lo = (r - mid.astype(F32)).astype(BF16)
    return hi, mid, lo


def _dot3(tri_bf16, x):
    hi, mid, lo = _split3(x)
    d = functools.partial(jnp.dot, tri_bf16, preferred_element_type=F32)
    return d(hi) + d(mid) + d(lo)


def _softplus(x):
    return jnp.maximum(x, 0.0) + jnp.log(1.0 + jnp.exp(-jnp.abs(x)))


def _small_cols(small, bias_ref, aneg_ref):
    lane = lax.broadcasted_iota(jnp.int32, small.shape, 1)
    pre = small + bias_ref[...]
    sp = _softplus(pre)
    logf = pre - sp
    g = aneg_ref[...] * sp
    beta = _sigmoid(small)
    gb = jnp.where(lane < FOX_HEADS + GDN_HEADS, g, beta)
    return logf, gb


def _l2n_heads(x):
    outs = []
    for h in range(x.shape[1] // GDN_HEAD_DIM):
        xh = x[:, h * GDN_HEAD_DIM:(h + 1) * GDN_HEAD_DIM]
        outs.append(xh * lax.rsqrt(jnp.sum(xh * xh, axis=-1, keepdims=True) + EPS))
    return jnp.concatenate(outs, axis=-1)


def _proj_prompt_body(x_ref, gpre_ref, wq_ref, wkT_ref, wvT_ref, ws_ref, wc_ref, wz_ref, wga_ref, wgb_ref,
                      bias_ref, aneg_ref, convw_ref, tri_ref,
                      q_ref, kT_ref, vT_ref, lfT_ref, cT_ref, gbT_ref, col_ref, qkv_ref, cst_ref,
                      z_ref, ga_ref, gb_ref, ubuf, ccarry, *, tiles_per_seq):
    i = pl.program_id(0)
    tm = x_ref.shape[0]

    @pl.when(i % tiles_per_seq == 0)
    def _():
        ubuf[0:8, :] = jnp.zeros((8, ubuf.shape[1]), F32)
        ccarry[...] = jnp.zeros(ccarry.shape, F32)

    h = _rms(x_ref[...], gpre_ref[...]).astype(BF16)
    dot = functools.partial(jnp.dot, h, preferred_element_type=F32)
    nt = (((1,), (1,)), ((), ()))

    q_ref[...] = (dot(wq_ref[...]) * (LOG2E * FOX_HEAD_DIM ** -0.5)).astype(BF16)
    kT_ref[0] = lax.dot_general(wkT_ref[...], h, nt, preferred_element_type=F32)
    vT_ref[0] = lax.dot_general(wvT_ref[...], h, nt, preferred_element_type=F32)
    z_ref[...] = dot(wz_ref[...]).astype(BF16)
    ga_ref[...] = _sigmoid(dot(wga_ref[...])).astype(BF16)
    gb_ref[...] = _sigmoid(dot(wgb_ref[...])).astype(BF16)

    logf, gb = _small_cols(dot(ws_ref[...]), bias_ref, aneg_ref)
    carry = ccarry[...]
    blocks = []
    for r in range(tm // LANES):
        cb = _dot3(tri_ref[...], logf[r * LANES:(r + 1) * LANES, :]) + carry
        carry = cb[LANES - 1:LANES, :]
        blocks.append(cb)
    ccarry[...] = carry
    c = jnp.concatenate(blocks, axis=0)
    lane = lax.broadcasted_iota(jnp.int32, c.shape, 1)
    col = jnp.where(lane < FOX_HEADS, c, gb)
    col_ref[...] = col
    colT = col.T
    lfT_ref[0] = logf.T[0:FOX_HEADS, :]
    cT_ref[0] = colT[0:FOX_HEADS, :]
    gbT_ref[0] = colT[FOX_HEADS:2 * FOX_HEADS, :]

    ubuf[8:8 + tm, :] = dot(wc_ref[...])
    base = 8 - (CONV_WIDTH - 1)
    conv = ubuf[base:base + tm, :] * convw_ref[0:1, :]
    for k in range(1, CONV_WIDTH):
        conv = conv + ubuf[base + k:base + k + tm, :] * convw_ref[k:k + 1, :]
    cst_ref[0] = ubuf[tm + base:tm + 8, :]
    ubuf[0:8, :] = ubuf[tm:tm + 8, :]
    conv = _silu(conv)
    qkv_ref[:, 0:2 * GDN_WIDTH] = _l2n_heads(conv[:, 0:2 * GDN_WIDTH])
    qkv_ref[:, 2 * GDN_WIDTH:] = conv[:, 2 * GDN_WIDTH:]


def _proj_prompt(x, g_pre, w, batch, seq, tm):
    n, d = x.shape
    tps = seq // tm
    row = lambda width: pl.BlockSpec((tm, width), lambda i: (i, 0))
    seqT = lambda rows: pl.BlockSpec((1, rows, tm), lambda i: (i // tps, 0, i % tps))
    consts = [g_pre, w["q"], w["kT"], w["vT"], w["small"], w["conv"], w["z"], w["ga"], w["gb"],
              w["bias"], w["aneg"], w["convw"], w["tri"]]
    out_shape = [
        jax.ShapeDtypeStruct((n, FOX_WIDTH), BF16),
        jax.ShapeDtypeStruct((batch, FOX_WIDTH, seq), F32),
        jax.ShapeDtypeStruct((batch, FOX_WIDTH, seq), F32),
        jax.ShapeDtypeStruct((batch, FOX_HEADS, seq), F32),
        jax.ShapeDtypeStruct((batch, FOX_HEADS, seq), F32),
        jax.ShapeDtypeStruct((batch, 2 * GDN_HEADS, seq), F32),
        jax.ShapeDtypeStruct((n, LANES), F32),
        jax.ShapeDtypeStruct((n, CONV_CH), F32),
        jax.ShapeDtypeStruct((batch, CONV_WIDTH - 1, CONV_CH), F32),
        jax.ShapeDtypeStruct((n, GDN_WIDTH), BF16),
        jax.ShapeDtypeStruct((n, d), BF16),
        jax.ShapeDtypeStruct((n, d), BF16),
    ]
    out_specs = [row(FOX_WIDTH), seqT(FOX_WIDTH), seqT(FOX_WIDTH), seqT(FOX_HEADS), seqT(FOX_HEADS),
                 seqT(2 * GDN_HEADS), row(LANES), row(CONV_CH),
                 pl.BlockSpec((1, CONV_WIDTH - 1, CONV_CH), lambda i: (i // tps, 0, 0)),
                 row(GDN_WIDTH), row(d), row(d)]
    return pl.pallas_call(
        functools.partial(_proj_prompt_body, tiles_per_seq=tps),
        grid=(n // tm,),
        in_specs=[row(d)] + [_const_spec(a.shape) for a in consts],
        out_specs=out_specs,
        out_shape=out_shape,
        scratch_shapes=[pltpu.VMEM((tm + 8, CONV_CH), F32), pltpu.VMEM((1, LANES), F32)],
        compiler_params=pltpu.CompilerParams(dimension_semantics=("arbitrary",), vmem_limit_bytes=VMEM_LIMIT),
        name="proj_prompt",
    )(x, *consts)


def _mixer_weights(w_in, b_f, conv_w, dt_bias, a_log):
    sizes = (FOX_WIDTH, FOX_WIDTH, FOX_WIDTH, FOX_HEADS, CONV_CH, GDN_HEADS, GDN_HEADS, GDN_WIDTH)
    d = w_in.shape[0]
    offs = [0]
    for s in sizes:
        offs.append(offs[-1] + s)
    part = lambda k: w_in[:, offs[k]:offs[k + 1]]
    ga0 = offs[-1]
    n_small = FOX_HEADS + 2 * GDN_HEADS
    small = jnp.concatenate([part(3), part(5), part(6), jnp.zeros((d, LANES - n_small), w_in.dtype)], axis=1)
    pad = lambda v, lo: jnp.zeros((1, LANES), F32).at[0, lo:lo + v.shape[0]].set(v.astype(F32))
    r = lax.broadcasted_iota(jnp.int32, (LANES, LANES), 0)
    c = lax.broadcasted_iota(jnp.int32, (LANES, LANES), 1)
    return {
        "q": part(0).astype(BF16), "kT": part(1).T.astype(BF16), "vT": part(2).T.astype(BF16),
        "small": small.astype(BF16), "conv": part(4).astype(BF16), "z": part(7).astype(BF16),
        "ga": w_in[:, ga0:ga0 + d].astype(BF16), "gb": w_in[:, ga0 + d:ga0 + 2 * d].astype(BF16),
        "bias": pad(b_f, 0) + pad(dt_bias, FOX_HEADS),
        "aneg": pad(-jnp.exp(a_log.astype(F32)), FOX_HEADS),
        "convw": conv_w.astype(F32),
        "tri": (c <= r).astype(BF16),
    }


NEG_BIG = -1e30


def _split3_f32(x):
    hi, mid, lo = _split3(x)
    return hi.astype(F32), mid.astype(F32), lo.astype(F32)


def _fox_prompt_body(q_ref, col_ref, kT_ref, vT_ref, cT_ref, o_ref, ka_ref, vb_ref, *, tq):
    p = pl.program_id(1)
    qi = pl.program_id(2)
    hd = FOX_HEAD_DIM
    n_aug = 16

    @pl.when(qi == 0)
    def _():
        vb_ref[...] = vT_ref[0].astype(BF16)
        sub = lax.broadcasted_iota(jnp.int32, (n_aug, ka_ref.shape[2]), 0)
        for e in range(2):
            ka_ref[e, 0:hd, :] = kT_ref[0, e * hd:(e + 1) * hd, :].astype(BF16)
            hi, mid, lo = _split3_f32(cT_ref[0, pl.ds(2 * p + e, 1), :] * LOG2E)
            aug = jnp.where(sub < 3, 1.0, jnp.where(sub == 3, -hi, jnp.where(sub == 4, -mid,
                            jnp.where(sub == 5, -lo, 0.0))))
            ka_ref[e, hd:hd + n_aug, :] = aug.astype(BF16)

    nt = (((1,), (1,)), ((), ()))
    col = col_ref[...]
    lane = lax.broadcasted_iota(jnp.int32, col.shape, 1)
    hrows = [slice(e * hd, (e + 1) * hd) for e in range(2)]
    q_pair = q_ref[...].astype(F32)
    qs = []
    for e in range(2):
        cq = jnp.sum(jnp.where(lane == 2 * p + e, col, 0.0), axis=-1, keepdims=True) * LOG2E
        hi, mid, lo = _split3_f32(cq)
        extra = jnp.where(lane == hd, hi, jnp.where(lane == hd + 1, mid, jnp.where(lane == hd + 2, lo,
                          jnp.where(lane < hd + 6, 1.0, 0.0))))
        qh = q_pair if e == 0 else pltpu.roll(q_pair, hd, axis=1)
        qs.append(jnp.where(lane < hd, qh, extra)[:, 0:hd + n_aug].astype(BF16))

    def block(kb, carry, masked):
        ks = pl.ds(pl.multiple_of(kb * tq, tq), tq)
        raw = [jnp.dot(qs[e], ka_ref[e, :, ks], preferred_element_type=F32) for e in range(2)]
        stats = []
        for e in range(2):
            m, l, _ = carry[e]
            s = raw[e]
            if masked:
                r = lax.broadcasted_iota(jnp.int32, s.shape, 0)
                c = lax.broadcasted_iota(jnp.int32, s.shape, 1)
                s = jnp.where(c <= r, s, NEG_BIG)
            m_new = jnp.maximum(m, jnp.max(s, axis=-1, keepdims=True))
            pr = jnp.exp2(s - m_new)
            alpha = jnp.exp2(m - m_new)
            stats.append((m_new, alpha * l + jnp.sum(pr, axis=-1, keepdims=True), alpha, pr.astype(BF16)))
        pvs = [lax.dot_general(stats[e][3], vb_ref[hrows[e], ks], nt, preferred_element_type=F32) for e in range(2)]
        return tuple((stats[e][0], stats[e][1], stats[e][2] * carry[e][2] + pvs[e]) for e in range(2))

    init = tuple((jnp.full((tq, 1), NEG_BIG, F32), jnp.zeros((tq, 1), F32), jnp.zeros((tq, hd), F32))
                 for _ in range(2))
    carry = lax.fori_loop(0, qi, lambda kb, c: block(kb, c, False), init)
    carry = block(qi, carry, True)
    o_ref[...] = jnp.concatenate([acc / l for (_, l, acc) in carry], axis=-1).astype(o_ref.dtype)


def _proj_sample_body(x_ref, gpre_ref, wqT_ref, wkT_ref, wvT_ref, ws_ref, wc_ref, wcT_ref, wz_ref, wga_ref, wgb_ref,
                      bias_ref, aneg_ref, convw_ref, convwT_ref, buf_ref, bufT_ref,
                      qT_ref, kT_ref, vT_ref, small_ref, smallT_ref, u_ref, qkT_ref, vg_ref, z_ref, ga_ref, gb_ref):
    h = _rms(x_ref[...], gpre_ref[...]).astype(BF16)
    dot = functools.partial(jnp.dot, h, preferred_element_type=F32)
    dot_t = lambda w: lax.dot_general(w, h, (((1,), (1,)), ((), ())), preferred_element_type=F32)
    qT_ref[...] = dot_t(wqT_ref[...]) * (FOX_HEAD_DIM ** -0.5)
    kT_ref[...] = dot_t(wkT_ref[...])
    vT_ref[...] = dot_t(wvT_ref[...])
    z_ref[...] = dot(wz_ref[...]).astype(BF16)
    ga_ref[...] = _sigmoid(dot(wga_ref[...])).astype(BF16)
    gb_ref[...] = _sigmoid(dot(wgb_ref[...])).astype(BF16)

    logf, gb = _small_cols(dot(ws_ref[...]), bias_ref, aneg_ref)
    lane = lax.broadcasted_iota(jnp.int32, logf.shape, 1)
    small = jnp.where(lane < FOX_HEADS, logf, gb)
    small_ref[...] = small
    pad = jnp.concatenate([small, jnp.zeros((LANES - small.shape[0], LANES), F32)], axis=0)
    smallT_ref[...] = pad.T[:, 0:small.shape[0]]

    u = dot(wc_ref[...])
    u_ref[...] = u
    vs = slice(2 * GDN_WIDTH, CONV_CH)
    conv_v = u[:, vs] * convw_ref[CONV_WIDTH - 1:CONV_WIDTH, vs]
    conv_qk = dot_t(wcT_ref[...]) * convwT_ref[:, CONV_WIDTH - 1:CONV_WIDTH]
    for k in range(CONV_WIDTH - 1):
        conv_v = conv_v + buf_ref[k][:, vs] * convw_ref[k:k + 1, vs]
        conv_qk = conv_qk + bufT_ref[k] * convwT_ref[:, k:k + 1]
    vg_ref[...] = _silu(conv_v)
    conv_qk = _silu(conv_qk)
    for hd in range(2 * GDN_HEADS):
        rs = slice(hd * GDN_HEAD_DIM, (hd + 1) * GDN_HEAD_DIM)
        xh = conv_qk[rs, :]
        qkT_ref[rs, :] = xh * lax.rsqrt(jnp.sum(xh * xh, axis=0, keepdims=True) + EPS)


def _proj_sample(x, g_pre, w, buf, bufT):
    db, d = x.shape
    ins = [x, g_pre, w["q"].T, w["kT"], w["vT"], w["small"], w["conv"], w["conv"][:, :2 * GDN_WIDTH].T, w["z"],
           w["ga"], w["gb"], w["bias"], w["aneg"], w["convw"], w["convw"][:, :2 * GDN_WIDTH].T, buf, bufT]
    out_shape = [
        jax.ShapeDtypeStruct((FOX_WIDTH, db), F32), jax.ShapeDtypeStruct((FOX_WIDTH, db), F32),
        jax.ShapeDtypeStruct((FOX_WIDTH, db), F32),
        jax.ShapeDtypeStruct((db, LANES), F32), jax.ShapeDtypeStruct((LANES, db), F32),
        jax.ShapeDtypeStruct((db, CONV_CH), F32), jax.ShapeDtypeStruct((2 * GDN_WIDTH, db), F32),
        jax.ShapeDtypeStruct((db, GDN_WIDTH), F32), jax.ShapeDtypeStruct((db, GDN_WIDTH), BF16),
        jax.ShapeDtypeStruct((db, d), BF16), jax.ShapeDtypeStruct((db, d), BF16),
    ]
    return pl.pallas_call(
        _proj_sample_body, out_shape=out_shape,
        compiler_params=pltpu.CompilerParams(vmem_limit_bytes=VMEM_LIMIT),
        name="proj_sample",
    )(*ins)


def _take_lane(x, idx):
    lane = lax.broadcasted_iota(jnp.int32, x.shape, 1)
    return jnp.sum(jnp.where(lane == idx, x, 0.0), axis=-1, keepdims=True)


PAGES_PER_STEP = 8


def _fox_decode_body(pt_ref, qT_ref, kTn_ref, vTn_ref, smallT_ref, sfxw_ref, ck_hbm, cv_hbm, clf_hbm, o_ref,
                     kbuf, vbuf, lfbuf, sem, qb_s, acc_s, m_s, l_s, sfx_s, al_s, p_s, *, n_pages):
    g_pages = PAGES_PER_STEP
    n_steps = n_pages // g_pages
    page = kbuf.shape[-1]
    b = pl.program_id(0)
    n_seq = pl.num_programs(0)
    hd = FOX_HEAD_DIM

    def group_copies(seq, j, slot):
        base = seq * n_pages + (n_steps - 1 - j) * g_pages
        out = []
        for g in range(g_pages):
            pg = pt_ref[base + (g_pages - 1 - g)]
            out.append(pltpu.make_async_copy(ck_hbm.at[pg], kbuf.at[slot, g], sem.at[slot]))
            out.append(pltpu.make_async_copy(cv_hbm.at[pg], vbuf.at[slot, g], sem.at[slot]))
            out.append(pltpu.make_async_copy(clf_hbm.at[pg], lfbuf.at[slot, g], sem.at[slot]))
        return out

    def per_head_dot(a, bm):
        return jnp.concatenate(
            [jnp.sum(a[h * hd:(h + 1) * hd, :] * bm[h * hd:(h + 1) * hd, :], axis=0, keepdims=True)
             for h in range(FOX_HEADS)], axis=0)

    @pl.when(b == 0)
    def _():
        for c in group_copies(0, 0, 0):
            c.start()

    qb = jnp.broadcast_to(_take_lane(qT_ref[...], b), (FOX_WIDTH, page))
    kn = jnp.broadcast_to(_take_lane(kTn_ref[...], b), (FOX_WIDTH, page))
    vn = jnp.broadcast_to(_take_lane(vTn_ref[...], b), (FOX_WIDTH, page))
    qb_s[...] = qb
    m_s[...] = per_head_dot(qb, kn)
    l_s[...] = jnp.ones(l_s.shape, F32)
    lane_w = lax.broadcasted_iota(jnp.int32, (FOX_WIDTH, page), 1)
    acc_s[...] = jnp.where(lane_w == 0, vn, 0.0)
    sfx_s[...] = jnp.broadcast_to(_take_lane(smallT_ref[0:FOX_HEADS, :], b), sfx_s.shape)

    def step(j, _):
        slot = j % 2
        nxt = b * n_steps + j + 1

        @pl.when(nxt < n_seq * n_steps)
        def _():
            for c in group_copies(nxt // n_steps, nxt % n_steps, 1 - slot):
                c.start()

        for c in group_copies(b, j, slot):
            c.wait()

        lfs = lfbuf[slot].reshape(g_pages * FOX_HEADS, page)
        hi, mid, lo = _split3(lfs)
        wdot = lambda a: jnp.dot(a, sfxw_ref[...], preferred_element_type=F32)
        red = wdot(hi) + wdot(mid) + wdot(lo)
        carry = sfx_s[...]
        scores = []
        for g in range(g_pages):
            hs = slice(g * FOX_HEADS, (g + 1) * FOX_HEADS)
            bias = carry + (red[hs, 0:page] - lfs[hs, :])
            carry = carry + red[hs, page:2 * page]
            kt = kbuf[slot, g].reshape(FOX_WIDTH, page)
            scores.append(per_head_dot(kt, qb_s[...]) + bias)
        sfx_s[...] = carry

        m_old = m_s[...]
        smax = scores[0]
        for s in scores[1:]:
            smax = jnp.maximum(smax, s)
        m_new = jnp.maximum(m_old, jnp.max(smax, axis=-1, keepdims=True))
        al_s[...] = jnp.exp(m_old - m_new)
        psum = None
        for g, s in enumerate(scores):
            p = jnp.exp(s - m_new)
            p_s[g] = p
            psum = p if psum is None else psum + p
        m_s[...] = m_new
        l_s[...] = al_s[...] * l_s[...] + jnp.sum(psum, axis=-1, keepdims=True)
        for h in range(FOX_HEADS):
            rs = slice(h * hd, (h + 1) * hd)
            acc = acc_s[rs, :] * al_s[h:h + 1, :]
            for g in range(g_pages):
                acc = acc + p_s[g, h:h + 1, :] * vbuf[slot, g, h]
            acc_s[rs, :] = acc
        return 0

    lax.fori_loop(0, n_steps, step, 0)

    inv_l = 1.0 / l_s[...]
    for h in range(FOX_HEADS):
        rs = slice(h * hd, (h + 1) * hd)
        o_ref[0, rs, :] = jnp.sum(acc_s[rs, :], axis=-1, keepdims=True) * inv_l[h:h + 1, 0:1]


def _fox_decode(page_table, qT, kTn, vTn, smallT, cache_kT, cache_vT, cache_lfT):
    db, n_pages = page_table.shape
    page = cache_kT.shape[-1]
    g_pages = PAGES_PER_STEP
    assert (n_pages // g_pages) % 2 == 0 and n_pages % g_pages == 0
    const = lambda shape: pl.BlockSpec(shape, lambda b, pt: (0,) * len(shape))
    hbm = pl.BlockSpec(memory_space=pl.ANY)
    ii = lax.broadcasted_iota(jnp.int32, (page, 2 * page), 0)
    jj = lax.broadcasted_iota(jnp.int32, (page, 2 * page), 1)
    sfxw = ((ii >= jj) | (jj >= page)).astype(BF16)
    in_specs = [const(qT.shape), const(kTn.shape), const(vTn.shape), const(smallT.shape), const(sfxw.shape),
                hbm, hbm, hbm]
    out = pl.pallas_call(
        functools.partial(_fox_decode_body, n_pages=n_pages),
        grid_spec=pltpu.PrefetchScalarGridSpec(
            num_scalar_prefetch=1, grid=(db,), in_specs=in_specs,
            out_specs=pl.BlockSpec((1, FOX_WIDTH, 1), lambda b, pt: (b, 0, 0)),
            scratch_shapes=[pltpu.VMEM((2, g_pages, FOX_HEADS, FOX_HEAD_DIM, page), F32),
                            pltpu.VMEM((2, g_pages, FOX_HEADS, FOX_HEAD_DIM, page), F32),
                            pltpu.VMEM((2, g_pages, FOX_HEADS, page), F32),
                            pltpu.SemaphoreType.DMA((2,)),
                            pltpu.VMEM((FOX_WIDTH, page), F32), pltpu.VMEM((FOX_WIDTH, page), F32),
                            pltpu.VMEM((FOX_HEADS, page), F32), pltpu.VMEM((FOX_HEADS, page), F32),
                            pltpu.VMEM((FOX_HEADS, page), F32), pltpu.VMEM((FOX_HEADS, page), F32),
                            pltpu.VMEM((g_pages, FOX_HEADS, page), F32)]),
        out_shape=jax.ShapeDtypeStruct((db, FOX_WIDTH, 1), F32),
        compiler_params=pltpu.CompilerParams(dimension_semantics=("arbitrary",), vmem_limit_bytes=VMEM_LIMIT),
        name="fox_decode",
    )(page_table.reshape(-1), qT, kTn, vTn, smallT, sfxw, cache_kT, cache_vT, cache_lfT)
    return out.reshape(db, FOX_WIDTH)


def _gdn_decode_body(qkT_ref, vg_ref, smallT_ref, s_ref, o_ref, snew_ref):
    b = pl.program_id(0)
    qk = _take_lane(qkT_ref[...], b)
    gbeta = _take_lane(smallT_ref[FOX_HEADS:FOX_HEADS + 2 * GDN_HEADS, :], b)
    for h in range(GDN_HEADS):
        rs = slice(h * GDN_HEAD_DIM, (h + 1) * GDN_HEAD_DIM)
        q = qk[rs, :] * (GDN_HEAD_DIM ** -0.5)
        k = qk[GDN_WIDTH + h * GDN_HEAD_DIM:GDN_WIDTH + (h + 1) * GDN_HEAD_DIM, :]
        v = vg_ref[0, :, rs]
        st = s_ref[0, h] * jnp.exp(gbeta[h:h + 1, :])
        delta = (v - jnp.sum(k * st, axis=0, keepdims=True)) * gbeta[GDN_HEADS + h:GDN_HEADS + h + 1, :]
        st = st + k * delta
        snew_ref[0, h] = st
        o_ref[0, :, rs] = jnp.sum(q * st, axis=0, keepdims=True)


def _gdn_decode(qkT, vg, smallT, state):
    db = vg.shape[0]
    const = lambda shape: pl.BlockSpec(shape, lambda b: (0,) * len(shape))
    st_spec = pl.BlockSpec((1, GDN_HEADS, GDN_HEAD_DIM, GDN_HEAD_DIM), lambda b: (b, 0, 0, 0))
    row_spec = pl.BlockSpec((1, 1, GDN_WIDTH), lambda b: (b, 0, 0))
    o, s_new = pl.pallas_call(
        _gdn_decode_body,
        grid=(db,),
        in_specs=[const(qkT.shape), row_spec, const(smallT.shape), st_spec],
        out_specs=[row_spec, st_spec],
        out_shape=[jax.ShapeDtypeStruct((db, 1, GDN_WIDTH), F32), jax.ShapeDtypeStruct(state.shape, F32)],
        compiler_params=pltpu.CompilerParams(dimension_semantics=("arbitrary",)),
        name="gdn_decode",
    )(qkT, vg.reshape(db, 1, GDN_WIDTH), smallT, state)
    return o.reshape(db, GDN_WIDTH), s_new


def _mix_out_body(x_ref, oa_ref, og_ref, z_ref, ga_ref, gb_ref, ggdn_ref, wbf_ref, wbg_ref, wout_ref, gpost_ref,
                  o_ref):
    og = og_ref[...]
    z = z_ref[...].astype(F32)
    parts = []
    for h in range(GDN_HEADS):
        ls = slice(h * GDN_HEAD_DIM, (h + 1) * GDN_HEAD_DIM)
        parts.append(_rms(og[:, ls], ggdn_ref[...]) * _silu(z[:, ls]))
    ogn = jnp.concatenate(parts, axis=-1).astype(BF16)
    y_a = jnp.dot(oa_ref[...], wbf_ref[...], preferred_element_type=F32)
    y_g = jnp.dot(ogn, wbg_ref[...], preferred_element_type=F32)
    y = (ga_ref[...].astype(F32) * y_a + gb_ref[...].astype(F32) * y_g).astype(BF16)
    y = jnp.dot(y, wout_ref[...], preferred_element_type=F32)
    o_ref[...] = x_ref[...] + _rms(y, gpost_ref[...])


def _mix_out(x, o_a, o_g, z, ga, gb, g_gdn, wbf, wbg, wout, g_post, tm):
    n, d = x.shape
    row = lambda width: pl.BlockSpec((tm, width), lambda i: (i, 0))
    consts = [g_gdn, wbf, wbg, wout, g_post]
    return pl.pallas_call(
        _mix_out_body,
        grid=(pl.cdiv(n, tm),),
        in_specs=[row(d), row(FOX_WIDTH), row(GDN_WIDTH), row(GDN_WIDTH), row(d), row(d)]
        + [_const_spec(a.shape) for a in consts],
        out_specs=row(d),
        out_shape=jax.ShapeDtypeStruct((n, d), F32),
        compiler_params=pltpu.CompilerParams(dimension_semantics=("arbitrary",), vmem_limit_bytes=VMEM_LIMIT),
        name="mix_out",
    )(x, o_a, o_g, z, ga, gb, *consts)


def _chunk_gates(col_ref, tri_ref, rows, heads, lane, rr, cc):
    colc = col_ref[rows, :]
    cs = _dot3(tri_ref[...], colc)
    keep = cc <= rr
    out = []
    for head in heads:
        gc = jnp.sum(jnp.where(lane == FOX_HEADS + head, cs, 0.0), axis=-1, keepdims=True)
        beta = jnp.sum(jnp.where(lane == FOX_HEADS + GDN_HEADS + head, colc, 0.0), axis=-1, keepdims=True)
        cmat = jnp.broadcast_to(gc, (CHUNK, CHUNK))
        decay = jnp.where(keep, jnp.exp(jnp.where(keep, cmat - cmat.T, 0.0)), 0.0)
        out.append((gc, beta, decay))
    return out


def _gdn_lower_body(k_ref, col_ref, tri_ref, l_ref):
    n_chunks = k_ref.shape[0] // CHUNK
    nt = (((1,), (1,)), ((), ()))
    rr = lax.broadcasted_iota(jnp.int32, (CHUNK, CHUNK), 0)
    cc = lax.broadcasted_iota(jnp.int32, (CHUNK, CHUNK), 1)
    lane = lax.broadcasted_iota(jnp.int32, (CHUNK, LANES), 1)

    def chunk_pair(i, _):
        work = []
        for c in range(2):
            n = 2 * i + c
            rows = pl.ds(pl.multiple_of(n * CHUNK, CHUNK), CHUNK)
            work.append((n, rows, _chunk_gates(col_ref, tri_ref, rows, range(GDN_HEADS), lane, rr, cc)))
        prods = []
        for n, rows, gates in work:
            for e, (_, beta, decay) in enumerate(gates):
                k = k_ref[rows, e * GDN_HEAD_DIM:(e + 1) * GDN_HEAD_DIM]
                kk = lax.dot_general((k * beta).astype(BF16), k.astype(BF16), nt, preferred_element_type=F32)
                prods.append((n, e, kk, decay))
        for n, e, kk, decay in prods:
            l_ref[0, e, n] = jnp.where(cc < rr, kk * decay, 0.0)
        return 0

    lax.fori_loop(0, n_chunks // 2, chunk_pair, 0)


def _unit_lower_inverse_body(l_ref, t_ref, lt_s, tt_s):
    for i in range(CHUNK):
        lt_s[i * CHUNK:(i + 1) * CHUNK, :] = l_ref[0, :, i, :].T
    sub = lax.broadcasted_iota(jnp.int32, (8, LANES), 0)
    for i in range(CHUNK):
        for kg in range(i // 8 + 1):
            acc = jnp.where(sub + 8 * kg == i, 1.0, 0.0)
            for j in range(8 * kg, i):
                acc = acc - lt_s[i * CHUNK + j:i * CHUNK + j + 1, :] * tt_s[j * CHUNK + 8 * kg:j * CHUNK + 8 * kg + 8, :]
            tt_s[i * CHUNK + 8 * kg:i * CHUNK + 8 * kg + 8, :] = acc
        for kg in range(i // 8 + 1, CHUNK // 8):
            tt_s[i * CHUNK + 8 * kg:i * CHUNK + 8 * kg + 8, :] = jnp.zeros((8, LANES), F32)
    for i in range(CHUNK):
        t_ref[0, :, i, :] = tt_s[i * CHUNK:(i + 1) * CHUNK, :].T


def _unit_lower_inverse(low):
    groups = low.shape[0]
    spec = pl.BlockSpec((1, LANES, CHUNK, CHUNK), lambda g: (g, 0, 0, 0))
    return pl.pallas_call(
        _unit_lower_inverse_body,
        grid=(groups,),
        in_specs=[spec], out_specs=spec,
        out_shape=jax.ShapeDtypeStruct(low.shape, F32),
        scratch_shapes=[pltpu.VMEM((CHUNK * CHUNK, LANES), F32), pltpu.VMEM((CHUNK * CHUNK, LANES), F32)],
        compiler_params=pltpu.CompilerParams(dimension_semantics=("arbitrary",), vmem_limit_bytes=VMEM_LIMIT),
        name="unit_lower_inverse",
    )(low)


def _gdn_prompt_body(q_ref, k_ref, v_ref, col_ref, tri_ref, t_ref, o_ref, s_ref, state_s,
                     u_s, w_s, qe_s, ke_s, at_s, eg_s):
    seg = pl.program_id(1)
    n_chunks = q_ref.shape[0] // CHUNK
    nt = (((1,), (1,)), ((), ()))
    tn = (((0,), (0,)), ((), ()))
    rr = lax.broadcasted_iota(jnp.int32, (CHUNK, CHUNK), 0)
    cc = lax.broadcasted_iota(jnp.int32, (CHUNK, CHUNK), 1)
    lane = lax.broadcasted_iota(jnp.int32, (CHUNK, LANES), 1)

    @pl.when(seg == 0)
    def _():
        state_s[...] = jnp.zeros(state_s.shape, F32)

    heads = range(GDN_HEADS)
    hl = [slice(e * GDN_HEAD_DIM, (e + 1) * GDN_HEAD_DIM) for e in heads]
    fdot = functools.partial(jnp.dot, preferred_element_type=F32)

    def prepare(i, _):
        work = []
        for c in range(2):
            n = 2 * i + c
            rows = pl.ds(pl.multiple_of(n * CHUNK, CHUNK), CHUNK)
            work.append((n, rows, _chunk_gates(col_ref, tri_ref, rows, heads, lane, rr, cc)))
        pending = []
        for n, rows, gates in work:
            for e, (gc, beta, decay) in enumerate(gates):
                q = q_ref[rows, hl[e]] * (GDN_HEAD_DIM ** -0.5)
                k = k_ref[rows, hl[e]]
                v = v_ref[rows, hl[e]]
                egc = jnp.exp(gc)
                g_last = gc[CHUNK - 1:CHUNK, :]
                tinv = t_ref[0, e, n]
                t_hi = tinv.astype(BF16)
                t_lo = (tinv - t_hi.astype(F32)).astype(BF16)
                rhs = jnp.concatenate([v * beta, k * (beta * egc)], axis=1)
                r_hi = rhs.astype(BF16)
                r_lo = (rhs - r_hi.astype(F32)).astype(BF16)
                uw = fdot(jnp.concatenate([t_hi, t_lo, t_hi], axis=1), jnp.concatenate([r_hi, r_hi, r_lo], axis=0))
                attn = lax.dot_general(q.astype(BF16), k.astype(BF16), nt, preferred_element_type=F32)
                pending.append((n, rows, e, uw, attn, decay))
                qe_s[rows, hl[e]] = (q * egc).astype(BF16)
                ke_s[e, n] = (k * jnp.exp(g_last - gc)).T.astype(BF16)
                eg_s[e, pl.ds(n, 1), :] = jnp.broadcast_to(jnp.exp(g_last), (1, LANES))
        for n, rows, e, uw, attn, decay in pending:
            u_s[rows, hl[e]] = uw[:, :GDN_HEAD_DIM]
            w_s[rows, hl[e]] = uw[:, GDN_HEAD_DIM:].astype(BF16)
            at_s[e, rows, :] = (attn * decay).astype(BF16)
        return 0

    lax.fori_loop(0, n_chunks // 2, prepare, 0)

    def scan(n, states):
        rows = pl.ds(pl.multiple_of(n * CHUNK, CHUNK), CHUNK)
        sbs = [states[e].astype(BF16) for e in heads]
        ws = [fdot(w_s[rows, hl[e]], sbs[e]) for e in heads]
        qs = [fdot(qe_s[rows, hl[e]], sbs[e]) for e in heads]
        v_new = [(u_s[rows, hl[e]] - ws[e]).astype(BF16) for e in heads]
        av = [fdot(at_s[e, rows, :], v_new[e]) for e in heads]
        kv = [fdot(ke_s[e, n], v_new[e]) for e in heads]
        for e in heads:
            o_ref[rows, hl[e]] = qs[e] + av[e]
        return tuple(states[e] * eg_s[e, pl.ds(n, 1), :] + kv[e] for e in heads)

    init = tuple(state_s[e] for e in range(GDN_HEADS))
    final = lax.fori_loop(0, n_chunks, scan, init)
    for e in range(GDN_HEADS):
        state_s[e] = final[e]
        s_ref[0, e] = final[e]


GDN_SEGMENT = 512


def _gdn_prompt(qkv, col, tri64, batch, seq):
    n = qkv.shape[0]
    n_chunks = seq // CHUNK
    n_seg = seq // GDN_SEGMENT
    seg_chunks = GDN_SEGMENT // CHUNK
    blk = lambda off: pl.BlockSpec((GDN_SEGMENT, GDN_WIDTH), lambda b, s: (b * n_seg + s, off))
    col_spec = pl.BlockSpec((GDN_SEGMENT, LANES), lambda b, s: (b * n_seg + s, 0))
    tri_spec = pl.BlockSpec((CHUNK, CHUNK), lambda b, s: (0, 0))
    mat_spec = pl.BlockSpec((1, GDN_HEADS, seg_chunks, CHUNK, CHUNK), lambda b, s: (b, 0, s, 0, 0))
    params = pltpu.CompilerParams(dimension_semantics=("arbitrary",) * 2, vmem_limit_bytes=VMEM_LIMIT)
    low = pl.pallas_call(
        _gdn_lower_body,
        grid=(batch, n_seg),
        in_specs=[blk(1), col_spec, tri_spec],
        out_specs=mat_spec,
        out_shape=jax.ShapeDtypeStruct((batch, GDN_HEADS, n_chunks, CHUNK, CHUNK), F32),
        compiler_params=params,
        name="gdn_lower",
    )(qkv, col, tri64)
    n_sys = batch * GDN_HEADS * n_chunks
    tinv = _unit_lower_inverse(low.reshape(n_sys // LANES, LANES, CHUNK, CHUNK)).reshape(low.shape)
    state_spec = pl.BlockSpec((1, GDN_HEADS, GDN_HEAD_DIM, GDN_HEAD_DIM), lambda b, s: (b, 0, 0, 0))
    return pl.pallas_call(
        _gdn_prompt_body,
        grid=(batch, n_seg),
        in_specs=[blk(0), blk(1), blk(2), col_spec, tri_spec, mat_spec],
        out_specs=[blk(0), state_spec],
        out_shape=[jax.ShapeDtypeStruct((n, GDN_WIDTH), F32),
                   jax.ShapeDtypeStruct((batch, GDN_HEADS, GDN_HEAD_DIM, GDN_HEAD_DIM), F32)],
        scratch_shapes=[pltpu.VMEM((GDN_HEADS, GDN_HEAD_DIM, GDN_HEAD_DIM), F32),
                        pltpu.VMEM((GDN_SEGMENT, GDN_WIDTH), F32), pltpu.VMEM((GDN_SEGMENT, GDN_WIDTH), BF16),
                        pltpu.VMEM((GDN_SEGMENT, GDN_WIDTH), BF16),
                        pltpu.VMEM((GDN_HEADS, seg_chunks, GDN_HEAD_DIM, CHUNK), BF16),
                        pltpu.VMEM((GDN_HEADS, GDN_SEGMENT, CHUNK), BF16),
                        pltpu.VMEM((GDN_HEADS, seg_chunks, LANES), F32)],
        compiler_params=params,
        name="gdn_prompt",
    )(qkv, qkv, qkv, col, tri64, tinv)


def _fox_prompt(q, col, kT, vT, cT, tq):
    n = q.shape[0]
    batch, _, seq = kT.shape
    nq = seq // tq
    pair = 2 * FOX_HEAD_DIM
    return pl.pallas_call(
        functools.partial(_fox_prompt_body, tq=tq),
        grid=(batch, FOX_HEADS // 2, nq),
        in_specs=[
            pl.BlockSpec((tq, pair), lambda b, p, i: (b * nq + i, p)),
            pl.BlockSpec((tq, LANES), lambda b, p, i: (b * nq + i, 0)),
            pl.BlockSpec((1, pair, seq), lambda b, p, i: (b, p, 0)),
            pl.BlockSpec((1, pair, seq), lambda b, p, i: (b, p, 0)),
            pl.BlockSpec((1, FOX_HEADS, seq), lambda b, p, i: (b, 0, 0)),
        ],
        out_specs=pl.BlockSpec((tq, pair), lambda b, p, i: (b * nq + i, p)),
        out_shape=jax.ShapeDtypeStruct((n, FOX_WIDTH), BF16),
        scratch_shapes=[pltpu.VMEM((2, FOX_HEAD_DIM + 16, seq), BF16), pltpu.VMEM((pair, seq), BF16)],
        compiler_params=pltpu.CompilerParams(dimension_semantics=("arbitrary",) * 3, vmem_limit_bytes=VMEM_LIMIT),
        name="fox_prompt",
    )(q, col, kT, vT, cT)


ROW_TILE = 512
FOX_Q_TILE = 512


def kernel(x_prompt, x_sample, cache_k, cache_v, cache_logf, state_gdn, state_conv, page_table,
           ffn1_norm_pre, ffn1_w_gate, ffn1_w_up, ffn1_w_down, ffn1_norm_post,
           mix_norm_pre, w_in, fox_forget_bias, gdn_conv_w, gdn_dt_bias, gdn_a_log, gdn_out_norm,
           w_branch_fox, w_branch_gdn, w_out, mix_norm_post,
           ffn2_norm_pre, ffn2_w_gate, ffn2_w_up, ffn2_w_down, ffn2_norm_post):
    batch, seq, d = x_prompt.shape
    db = x_sample.shape[0]
    assert x_sample.shape[1] == 1, "sample group carries one new token per sequence"
    depth = w_in.shape[0]
    xp = x_prompt.reshape(batch * seq, d)
    xs = x_sample.reshape(db, d)
    row = lambda v: v.reshape(1, -1).astype(F32)
    st_p, st_s = [], []
    for l in range(depth):
        ffn1 = _ffn_weights(ffn1_w_gate[l], ffn1_w_up[l], ffn1_w_down[l])
        ffn2 = _ffn_weights(ffn2_w_gate[l], ffn2_w_up[l], ffn2_w_down[l])
        w = _mixer_weights(w_in[l], fox_forget_bias[l], gdn_conv_w[l], gdn_dt_bias[l], gdn_a_log[l])
        out_w = (row(gdn_out_norm[l]), w_branch_fox[l].astype(BF16), w_branch_gdn[l].astype(BF16),
                 w_out[l].astype(BF16), row(mix_norm_post[l]))

        xp = _ffn(xp, row(ffn1_norm_pre[l]), *ffn1, row(ffn1_norm_post[l]), ROW_TILE)
        (q, kT, vT, lfT, cT, _, col, qkv, conv_p, z, ga, gb) = _proj_prompt(
            xp, row(mix_norm_pre[l]), w, batch, seq, ROW_TILE)
        o_a = _fox_prompt(q, col, kT, vT, cT, FOX_Q_TILE)
        o_g, s_p = _gdn_prompt(qkv, col, w["tri"][:CHUNK, :CHUNK], batch, seq)
        xp = _mix_out(xp, o_a, o_g, z, ga, gb, *out_w, ROW_TILE)
        xp = _ffn(xp, row(ffn2_norm_pre[l]), *ffn2, row(ffn2_norm_post[l]), ROW_TILE)
        to_heads = lambda t: t.reshape(batch, FOX_HEADS, FOX_HEAD_DIM, seq).transpose(0, 3, 1, 2)
        st_p.append((to_heads(kT), to_heads(vT), lfT.transpose(0, 2, 1), s_p, conv_p))

        xs = _ffn(xs, row(ffn1_norm_pre[l]), *ffn1, row(ffn1_norm_post[l]), db)
        buf = state_conv[l].transpose(1, 0, 2)
        bufT = state_conv[l][:, :, :2 * GDN_WIDTH].transpose(1, 2, 0)
        (qTs, kTs, vTs, small, smallT, u, qkT, vg, zs, gas, gbs) = _proj_sample(
            xs, row(mix_norm_pre[l]), w, buf, bufT)
        o_as = _fox_decode(page_table, qTs, kTs, vTs, smallT,
                           cache_k[l].transpose(0, 2, 3, 1), cache_v[l].transpose(0, 2, 3, 1),
                           cache_logf[l].transpose(0, 2, 1))
        o_gs, s_s = _gdn_decode(qkT, vg, smallT, state_gdn[l])
        xs = _mix_out(xs, o_as.astype(BF16), o_gs, zs, gas, gbs, *out_w, db)
        xs = _ffn(xs, row(ffn2_norm_pre[l]), *ffn2, row(ffn2_norm_post[l]), db)
        new_conv = jnp.concatenate([state_conv[l][:, 1:], u[:, None, :]], axis=1)
        st_s.append((kTs.T.reshape(db, 1, FOX_HEADS, FOX_HEAD_DIM), vTs.T.reshape(db, 1, FOX_HEADS, FOX_HEAD_DIM),
                     small[:, :FOX_HEADS].reshape(db, 1, FOX_HEADS), s_s, new_conv))

    stack = lambda states, i: jnp.stack([s[i] for s in states], axis=0)
    return (xp.reshape(batch, seq, d), xs.reshape(db, 1, d),
            *[stack(st_p, i) for i in range(5)], *[stack(st_s, i) for i in range(5)])
```

```python
import functools

import jax
import jax.numpy as jnp
from jax import lax
from jax.experimental import pallas as pl
from jax.experimental.pallas import tpu as pltpu

F32 = jnp.float32
BF16 = jnp.bfloat16
EPS = 1e-6
LOG2E = 1.4426950408889634
LANES = 128
VMEM_LIMIT = 56 * 1024 * 1024

FOX_HEADS = 8
FOX_HEAD_DIM = 64
FOX_WIDTH = FOX_HEADS * FOX_HEAD_DIM
GDN_HEADS = 4
GDN_HEAD_DIM = 128
GDN_WIDTH = GDN_HEADS * GDN_HEAD_DIM
CONV_WIDTH = 4
CONV_CH = 3 * GDN_WIDTH
CHUNK = 64
FF_CHUNK = 256


def _rms(x, g):
    return x * lax.rsqrt(jnp.mean(x * x, axis=-1, keepdims=True) + EPS) * g


def _sigmoid(x):
    return 1.0 / (1.0 + jnp.exp(-x))


def _silu(x):
    return x * _sigmoid(x)


def _const_spec(shape):
    n = len(shape)
    return pl.BlockSpec(shape, lambda *_: (0,) * n, pipeline_mode=pl.Buffered(1))


def _ffn_body(x_ref, gpre_ref, wg_ref, wu_ref, wd_ref, gpost_ref, o_ref):
    x = x_ref[...]
    h = _rms(x, gpre_ref[...]).astype(BF16)
    y = jnp.zeros(x.shape, F32)
    for j in range(wg_ref.shape[1] // FF_CHUNK):
        cols = slice(j * FF_CHUNK, (j + 1) * FF_CHUNK)
        g = jnp.dot(h, wg_ref[:, cols], preferred_element_type=F32)
        u = jnp.dot(h, wu_ref[:, cols], preferred_element_type=F32)
        a = (_silu(g) * u).astype(BF16)
        y = y + jnp.dot(a, wd_ref[cols, :], preferred_element_type=F32)
    o_ref[...] = x + 0.5 * _rms(y, gpost_ref[...])


def _ffn(x, g_pre, wg, wu, wd, g_post, tm):
    n, d = x.shape
    row = pl.BlockSpec((tm, d), lambda i: (i, 0))
    return pl.pallas_call(
        _ffn_body,
        grid=(pl.cdiv(n, tm),),
        in_specs=[row, _const_spec((1, d)), _const_spec(wg.shape), _const_spec(wu.shape),
                  _const_spec(wd.shape), _const_spec((1, d))],
        out_specs=row,
        out_shape=jax.ShapeDtypeStruct((n, d), F32),
        compiler_params=pltpu.CompilerParams(dimension_semantics=("arbitrary",), vmem_limit_bytes=VMEM_LIMIT),
        name="ffn",
    )(x, g_pre, wg, wu, wd, g_post)


def _ffn_weights(w_gate, w_up, w_down):
    assert w_gate.shape[1] % FF_CHUNK == 0
    return w_gate.astype(BF16), w_up.astype(BF16), w_down.astype(BF16)


def _split3(x):
    hi = x.astype(BF16)
    r = x - hi.astype(F32)
    mid = r.astype(BF16)
    lo = (r - mid.astype(F32)).astype(BF16)
    return hi, mid, lo


def _dot3(tri_bf16, x):
    hi, mid, lo = _split3(x)
    d = functools.partial(jnp.dot, tri_bf16, preferred_element_type=F32)
    return d(hi) + d(mid) + d(lo)


def _softplus(x):
    return jnp.maximum(x, 0.0) + jnp.log(1.0 + jnp.exp(-jnp.abs(x)))


def _small_cols(small, bias_ref, aneg_ref):
    lane = lax.broadcasted_iota(jnp.int32, small.shape, 1)
    pre = small + bias_ref[...]
    sp = _softplus(pre)
    logf = pre - sp
    g = aneg_ref[...] * sp
    beta = _sigmoid(small)
    gb = jnp.where(lane < FOX_HEADS + GDN_HEADS, g, beta)
    return logf, gb


def _l2n_heads(x):
    outs = []
    for h in range(x.shape[1] // GDN_HEAD_DIM):
        xh = x[:, h * GDN_HEAD_DIM:(h + 1) * GDN_HEAD_DIM]
        outs.append(xh * lax.rsqrt(jnp.sum(xh * xh, axis=-1, keepdims=True) + EPS))
    return jnp.concatenate(outs, axis=-1)


def _proj_prompt_body(x_ref, gpre_ref, wq_ref, wkT_ref, wvT_ref, ws_ref, wc_ref, wz_ref, wga_ref, wgb_ref,
                      bias_ref, aneg_ref, convw_ref, tri_ref,
                      q_ref, kT_ref, vT_ref, lfT_ref, cT_ref, gbT_ref, col_ref, qkv_ref, cst_ref,
                      z_ref, ga_ref, gb_ref, ubuf, ccarry, *, tiles_per_seq):
    i = pl.program_id(0)
    tm = x_ref.shape[0]

    @pl.when(i % tiles_per_seq == 0)
    def _():
        ubuf[0:8, :] = jnp.zeros((8, ubuf.shape[1]), F32)
        ccarry[...] = jnp.zeros(ccarry.shape, F32)

    h = _rms(x_ref[...], gpre_ref[...]).astype(BF16)
    dot = functools.partial(jnp.dot, h, preferred_element_type=F32)
    nt = (((1,), (1,)), ((), ()))

    ubuf[8:8 + tm, :] = dot(wc_ref[...])
    small = dot(ws_ref[...])
    q_ref[...] = (dot(wq_ref[...]) * (LOG2E * FOX_HEAD_DIM ** -0.5)).astype(BF16)
    kT_ref[0] = lax.dot_general(wkT_ref[...], h, nt, preferred_element_type=F32)
    vT_ref[0] = lax.dot_general(wvT_ref[...], h, nt, preferred_element_type=F32)
    z_ref[...] = dot(wz_ref[...]).astype(BF16)
    ga_ref[...] = _sigmoid(dot(wga_ref[...])).astype(BF16)
    gb_ref[...] = _sigmoid(dot(wgb_ref[...])).astype(BF16)

    logf, gb = _small_cols(small, bias_ref, aneg_ref)
    carry = ccarry[...]
    blocks = []
    for r in range(tm // LANES):
        cb = _dot3(tri_ref[...], logf[r * LANES:(r + 1) * LANES, :]) + carry
        carry = cb[LANES - 1:LANES, :]
        blocks.append(cb)
    ccarry[...] = carry
    c = jnp.concatenate(blocks, axis=0)
    lane = lax.broadcasted_iota(jnp.int32, c.shape, 1)
    col = jnp.where(lane < FOX_HEADS, c, gb)
    col_ref[...] = col
    colT = col.T
    lfT_ref[0] = logf.T[0:FOX_HEADS, :]
    cT_ref[0] = colT[0:FOX_HEADS, :]
    gbT_ref[0] = colT[FOX_HEADS:2 * FOX_HEADS, :]

    base = 8 - (CONV_WIDTH - 1)
    conv = ubuf[base:base + tm, :] * convw_ref[0:1, :]
    for k in range(1, CONV_WIDTH):
        conv = conv + ubuf[base + k:base + k + tm, :] * convw_ref[k:k + 1, :]
    cst_ref[0] = ubuf[tm + base:tm + 8, :]
    ubuf[0:8, :] = ubuf[tm:tm + 8, :]
    conv = _silu(conv)
    qkv_ref[:, 0:2 * GDN_WIDTH] = _l2n_heads(conv[:, 0:2 * GDN_WIDTH])
    qkv_ref[:, 2 * GDN_WIDTH:] = conv[:, 2 * GDN_WIDTH:]


def _proj_prompt(x, g_pre, w, batch, seq, tm):
    n, d = x.shape
    tps = seq // tm
    row = lambda width: pl.BlockSpec((tm, width), lambda i: (i, 0))
    seqT = lambda rows: pl.BlockSpec((1, rows, tm), lambda i: (i // tps, 0, i % tps))
    consts = [g_pre, w["q"], w["kT"], w["vT"], w["small"], w["conv"], w["z"], w["ga"], w["gb"],
              w["bias"], w["aneg"], w["convw"], w["tri"]]
    out_shape = [
        jax.ShapeDtypeStruct((n, FOX_WIDTH), BF16),
        jax.ShapeDtypeStruct((batch, FOX_WIDTH, seq), F32),
        jax.ShapeDtypeStruct((batch, FOX_WIDTH, seq), F32),
        jax.ShapeDtypeStruct((batch, FOX_HEADS, seq), F32),
        jax.ShapeDtypeStruct((batch, FOX_HEADS, seq), F32),
        jax.ShapeDtypeStruct((batch, 2 * GDN_HEADS, seq), F32),
        jax.ShapeDtypeStruct((n, LANES), F32),
        jax.ShapeDtypeStruct((n, CONV_CH), F32),
        jax.ShapeDtypeStruct((batch, CONV_WIDTH - 1, CONV_CH), F32),
        jax.ShapeDtypeStruct((n, GDN_WIDTH), BF16),
        jax.ShapeDtypeStruct((n, d), BF16),
        jax.ShapeDtypeStruct((n, d), BF16),
    ]
    out_specs = [row(FOX_WIDTH), seqT(FOX_WIDTH), seqT(FOX_WIDTH), seqT(FOX_HEADS), seqT(FOX_HEADS),
                 seqT(2 * GDN_HEADS), row(LANES), row(CONV_CH),
                 pl.BlockSpec((1, CONV_WIDTH - 1, CONV_CH), lambda i: (i // tps, 0, 0)),
                 row(GDN_WIDTH), row(d), row(d)]
    return pl.pallas_call(
        functools.partial(_proj_prompt_body, tiles_per_seq=tps),
        grid=(n // tm,),
        in_specs=[row(d)] + [_const_spec(a.shape) for a in consts],
        out_specs=out_specs,
        out_shape=out_shape,
        scratch_shapes=[pltpu.VMEM((tm + 8, CONV_CH), F32), pltpu.VMEM((1, LANES), F32)],
        compiler_params=pltpu.CompilerParams(dimension_semantics=("arbitrary",), vmem_limit_bytes=VMEM_LIMIT),
        name="proj_prompt",
    )(x, *consts)


def _mixer_weights(w_in, b_f, conv_w, dt_bias, a_log):
    sizes = (FOX_WIDTH, FOX_WIDTH, FOX_WIDTH, FOX_HEADS, CONV_CH, GDN_HEADS, GDN_HEADS, GDN_WIDTH)
    d = w_in.shape[0]
    offs = [0]
    for s in sizes:
        offs.append(offs[-1] + s)
    part = lambda k: w_in[:, offs[k]:offs[k + 1]]
    ga0 = offs[-1]
    n_small = FOX_HEADS + 2 * GDN_HEADS
    small = jnp.concatenate([part(3), part(5), part(6), jnp.zeros((d, LANES - n_small), w_in.dtype)], axis=1)
    pad = lambda v, lo: jnp.zeros((1, LANES), F32).at[0, lo:lo + v.shape[0]].set(v.astype(F32))
    r = lax.broadcasted_iota(jnp.int32, (LANES, LANES), 0)
    c = lax.broadcasted_iota(jnp.int32, (LANES, LANES), 1)
    return {
        "q": part(0).astype(BF16), "kT": part(1).T.astype(BF16), "vT": part(2).T.astype(BF16),
        "small": small.astype(BF16), "conv": part(4).astype(BF16), "z": part(7).astype(BF16),
        "ga": w_in[:, ga0:ga0 + d].astype(BF16), "gb": w_in[:, ga0 + d:ga0 + 2 * d].astype(BF16),
        "bias": pad(b_f, 0) + pad(dt_bias, FOX_HEADS),
        "aneg": pad(-jnp.exp(a_log.astype(F32)), FOX_HEADS),
        "convw": conv_w.astype(F32),
        "tri": (c <= r).astype(BF16),
    }


NEG_BIG = -1e30


def _split3_f32(x):
    hi, mid, lo = _split3(x)
    return hi.astype(F32), mid.astype(F32), lo.astype(F32)


def _fox_prompt_body(q_ref, col_ref, kT_ref, vT_ref, cT_ref, o_ref, ka_ref, vb_ref, *, tq):
    p = pl.program_id(1)
    qi = pl.program_id(2)
    hd = FOX_HEAD_DIM
    n_aug = 16

    @pl.when(qi == 0)
    def _():
        vb_ref[...] = vT_ref[0].astype(BF16)
        sub = lax.broadcasted_iota(jnp.int32, (n_aug, ka_ref.shape[2]), 0)
        for e in range(2):
            ka_ref[e, 0:hd, :] = kT_ref[0, e * hd:(e + 1) * hd, :].astype(BF16)
            hi, mid, lo = _split3_f32(cT_ref[0, pl.ds(2 * p + e, 1), :] * LOG2E)
            aug = jnp.where(sub < 3, 1.0, jnp.where(sub == 3, -hi, jnp.where(sub == 4, -mid,
                            jnp.where(sub == 5, -lo, 0.0))))
            ka_ref[e, hd:hd + n_aug, :] = aug.astype(BF16)

    nt = (((1,), (1,)), ((), ()))
    col = col_ref[...]
    lane = lax.broadcasted_iota(jnp.int32, col.shape, 1)
    hrows = [slice(e * hd, (e + 1) * hd) for e in range(2)]
    q_pair = q_ref[...].astype(F32)
    qs = []
    for e in range(2):
        cq = jnp.sum(jnp.where(lane == 2 * p + e, col, 0.0), axis=-1, keepdims=True) * LOG2E
        hi, mid, lo = _split3_f32(cq)
        extra = jnp.where(lane == hd, hi, jnp.where(lane == hd + 1, mid, jnp.where(lane == hd + 2, lo,
                          jnp.where(lane < hd + 6, 1.0, 0.0))))
        qh = q_pair if e == 0 else pltpu.roll(q_pair, hd, axis=1)
        qs.append(jnp.where(lane < hd, qh, extra)[:, 0:hd + n_aug].astype(BF16))

    def block(kb, carry, masked):
        ks = pl.ds(pl.multiple_of(kb * tq, tq), tq)
        raw = [jnp.dot(qs[e], ka_ref[e, :, ks], preferred_element_type=F32) for e in range(2)]
        stats = []
        for e in range(2):
            m, l, _ = carry[e]
            s = raw[e]
            if masked:
                r = lax.broadcasted_iota(jnp.int32, s.shape, 0)
                c = lax.broadcasted_iota(jnp.int32, s.shape, 1)
                s = jnp.where(c <= r, s, NEG_BIG)
            m_new = jnp.maximum(m, jnp.max(s, axis=-1, keepdims=True))
            pr = jnp.exp2(s - m_new)
            alpha = jnp.exp2(m - m_new)
            stats.append((m_new, alpha * l + jnp.sum(pr, axis=-1, keepdims=True), alpha, pr.astype(BF16)))
        pvs = [lax.dot_general(stats[e][3], vb_ref[hrows[e], ks], nt, preferred_element_type=F32) for e in range(2)]
        return tuple((stats[e][0], stats[e][1], stats[e][2] * carry[e][2] + pvs[e]) for e in range(2))

    init = tuple((jnp.full((tq, 1), NEG_BIG, F32), jnp.zeros((tq, 1), F32), jnp.zeros((tq, hd), F32))
                 for _ in range(2))
    carry = lax.fori_loop(0, qi, lambda kb, c: block(kb, c, False), init)
    carry = block(qi, carry, True)
    o_ref[...] = jnp.concatenate([acc / l for (_, l, acc) in carry], axis=-1).astype(o_ref.dtype)


def _proj_sample_body(x_ref, gpre_ref, wqT_ref, wkT_ref, wvT_ref, ws_ref, wc_ref, wcT_ref, wz_ref, wga_ref, wgb_ref,
                      bias_ref, aneg_ref, convw_ref, convwT_ref, buf_ref, bufT_ref,
                      qT_ref, kT_ref, vT_ref, small_ref, smallT_ref, u_ref, qkT_ref, vg_ref, z_ref, ga_ref, gb_ref):
    h = _rms(x_ref[...], gpre_ref[...]).astype(BF16)
    dot = functools.partial(jnp.dot, h, preferred_element_type=F32)
    dot_t = lambda w: lax.dot_general(w, h, (((1,), (1,)), ((), ())), preferred_element_type=F32)
    qT_ref[...] = dot_t(wqT_ref[...]) * (FOX_HEAD_DIM ** -0.5)
    kT_ref[...] = dot_t(wkT_ref[...])
    vT_ref[...] = dot_t(wvT_ref[...])
    z_ref[...] = dot(wz_ref[...]).astype(BF16)
    ga_ref[...] = _sigmoid(dot(wga_ref[...])).astype(BF16)
    gb_ref[...] = _sigmoid(dot(wgb_ref[...])).astype(BF16)

    logf, gb = _small_cols(dot(ws_ref[...]), bias_ref, aneg_ref)
    lane = lax.broadcasted_iota(jnp.int32, logf.shape, 1)
    small = jnp.where(lane < FOX_HEADS, logf, gb)
    small_ref[...] = small
    pad = jnp.concatenate([small, jnp.zeros((LANES - small.shape[0], LANES), F32)], axis=0)
    smallT_ref[...] = pad.T[:, 0:small.shape[0]]

    u = dot(wc_ref[...])
    u_ref[...] = u
    vs = slice(2 * GDN_WIDTH, CONV_CH)
    conv_v = u[:, vs] * convw_ref[CONV_WIDTH - 1:CONV_WIDTH, vs]
    conv_qk = dot_t(wcT_ref[...]) * convwT_ref[:, CONV_WIDTH - 1:CONV_WIDTH]
    for k in range(CONV_WIDTH - 1):
        conv_v = conv_v + buf_ref[k][:, vs] * convw_ref[k:k + 1, vs]
        conv_qk = conv_qk + bufT_ref[k] * convwT_ref[:, k:k + 1]
    vg_ref[...] = _silu(conv_v)
    conv_qk = _silu(conv_qk)
    for hd in range(2 * GDN_HEADS):
        rs = slice(hd * GDN_HEAD_DIM, (hd + 1) * GDN_HEAD_DIM)
        xh = conv_qk[rs, :]
        qkT_ref[rs, :] = xh * lax.rsqrt(jnp.sum(xh * xh, axis=0, keepdims=True) + EPS)


def _proj_sample(x, g_pre, w, buf, bufT):
    db, d = x.shape
    ins = [x, g_pre, w["q"].T, w["kT"], w["vT"], w["small"], w["conv"], w["conv"][:, :2 * GDN_WIDTH].T, w["z"],
           w["ga"], w["gb"], w["bias"], w["aneg"], w["convw"], w["convw"][:, :2 * GDN_WIDTH].T, buf, bufT]
    out_shape = [
        jax.ShapeDtypeStruct((FOX_WIDTH, db), F32), jax.ShapeDtypeStruct((FOX_WIDTH, db), F32),
        jax.ShapeDtypeStruct((FOX_WIDTH, db), F32),
        jax.ShapeDtypeStruct((db, LANES), F32), jax.ShapeDtypeStruct((LANES, db), F32),
        jax.ShapeDtypeStruct((db, CONV_CH), F32), jax.ShapeDtypeStruct((2 * GDN_WIDTH, db), F32),
        jax.ShapeDtypeStruct((db, GDN_WIDTH), F32), jax.ShapeDtypeStruct((db, GDN_WIDTH), BF16),
        jax.ShapeDtypeStruct((db, d), BF16), jax.ShapeDtypeStruct((db, d), BF16),
    ]
    return pl.pallas_call(
        _proj_sample_body, out_shape=out_shape,
        compiler_params=pltpu.CompilerParams(vmem_limit_bytes=VMEM_LIMIT),
        name="proj_sample",
    )(*ins)


def _take_lane(x, idx):
    lane = lax.broadcasted_iota(jnp.int32, x.shape, 1)
    return jnp.sum(jnp.where(lane == idx, x, 0.0), axis=-1, keepdims=True)


PAGES_PER_STEP = 8
DECODE_RING = 4


def _fox_decode_body(pt_ref, qT_ref, kTn_ref, vTn_ref, smallT_ref, sfxw_ref, ck_hbm, cv_hbm, clf_hbm, o_ref,
                     kbuf, vbuf, lfbuf, sem, qb_s, acc_s, m_s, l_s, sfx_s, al_s, p_s, *, n_pages):
    g_pages = PAGES_PER_STEP
    n_steps = n_pages // g_pages
    page = kbuf.shape[-1]
    b = pl.program_id(0)
    n_seq = pl.num_programs(0)
    hd = FOX_HEAD_DIM

    def group_copies(seq, j, slot):
        base = seq * n_pages + (n_steps - 1 - j) * g_pages
        out = []
        for g in range(g_pages):
            pg = pt_ref[base + (g_pages - 1 - g)]
            out.append(pltpu.make_async_copy(ck_hbm.at[pg], kbuf.at[slot, g], sem.at[slot]))
            out.append(pltpu.make_async_copy(cv_hbm.at[pg], vbuf.at[slot, g], sem.at[slot]))
            out.append(pltpu.make_async_copy(clf_hbm.at[pg], lfbuf.at[slot, g], sem.at[slot]))
        return out

    def per_head_dot(a, bm):
        return jnp.concatenate(
            [jnp.sum(a[h * hd:(h + 1) * hd, :] * bm[h * hd:(h + 1) * hd, :], axis=0, keepdims=True)
             for h in range(FOX_HEADS)], axis=0)

    ring = kbuf.shape[0]

    @pl.when(b == 0)
    def _():
        for t in range(ring - 1):
            for c in group_copies(t // n_steps, t % n_steps, t):
                c.start()

    qb = jnp.broadcast_to(_take_lane(qT_ref[...], b), (FOX_WIDTH, page))
    kn = jnp.broadcast_to(_take_lane(kTn_ref[...], b), (FOX_WIDTH, page))
    vn = jnp.broadcast_to(_take_lane(vTn_ref[...], b), (FOX_WIDTH, page))
    qb_s[...] = qb
    m_s[...] = per_head_dot(qb, kn)
    l_s[...] = jnp.ones(l_s.shape, F32)
    lane_w = lax.broadcasted_iota(jnp.int32, (FOX_WIDTH, page), 1)
    acc_s[...] = jnp.where(lane_w == 0, vn, 0.0)
    sfx_s[...] = jnp.broadcast_to(_take_lane(smallT_ref[0:FOX_HEADS, :], b), sfx_s.shape)

    def step(j, _):
        t = b * n_steps + j
        slot = t % ring
        nxt = t + ring - 1

        @pl.when(nxt < n_seq * n_steps)
        def _():
            for c in group_copies(nxt // n_steps, nxt % n_steps, nxt % ring):
                c.start()

        for c in group_copies(b, j, slot):
            c.wait()

        lfs = lfbuf[slot].reshape(g_pages * FOX_HEADS, page)
        hi, mid, lo = _split3(lfs)
        wdot = lambda a: jnp.dot(a, sfxw_ref[...], preferred_element_type=F32)
        red = wdot(hi) + wdot(mid) + wdot(lo)
        carry = sfx_s[...]
        scores = []
        for g in range(g_pages):
            hs = slice(g * FOX_HEADS, (g + 1) * FOX_HEADS)
            bias = carry + (red[hs, 0:page] - lfs[hs, :])
            carry = carry + red[hs, page:2 * page]
            kt = kbuf[slot, g].reshape(FOX_WIDTH, page)
            scores.append(per_head_dot(kt, qb_s[...]) + bias)
        sfx_s[...] = carry

        m_old = m_s[...]
        smax = scores[0]
        for s in scores[1:]:
            smax = jnp.maximum(smax, s)
        m_new = jnp.maximum(m_old, jnp.max(smax, axis=-1, keepdims=True))
        al_s[...] = jnp.exp(m_old - m_new)
        psum = None
        for g, s in enumerate(scores):
            p = jnp.exp(s - m_new)
            p_s[g] = p
            psum = p if psum is None else psum + p
        m_s[...] = m_new
        l_s[...] = al_s[...] * l_s[...] + jnp.sum(psum, axis=-1, keepdims=True)
        for h in range(FOX_HEADS):
            rs = slice(h * hd, (h + 1) * hd)
            acc = acc_s[rs, :] * al_s[h:h + 1, :]
            for g in range(g_pages):
                acc = acc + p_s[g, h:h + 1, :] * vbuf[slot, g, h]
            acc_s[rs, :] = acc
        return 0

    lax.fori_loop(0, n_steps, step, 0)

    inv_l = 1.0 / l_s[...]
    for h in range(FOX_HEADS):
        rs = slice(h * hd, (h + 1) * hd)
        o_ref[0, rs, :] = jnp.sum(acc_s[rs, :], axis=-1, keepdims=True) * inv_l[h:h + 1, 0:1]


def _fox_decode(page_table, qT, kTn, vTn, smallT, cache_kT, cache_vT, cache_lfT):
    db, n_pages = page_table.shape
    page = cache_kT.shape[-1]
    g_pages = PAGES_PER_STEP
    ring = DECODE_RING
    assert n_pages % g_pages == 0 and db * (n_pages // g_pages) >= ring - 1
    const = lambda shape: pl.BlockSpec(shape, lambda b, pt: (0,) * len(shape))
    hbm = pl.BlockSpec(memory_space=pl.ANY)
    ii = lax.broadcasted_iota(jnp.int32, (page, 2 * page), 0)
    jj = lax.broadcasted_iota(jnp.int32, (page, 2 * page), 1)
    sfxw = ((ii >= jj) | (jj >= page)).astype(BF16)
    in_specs = [const(qT.shape), const(kTn.shape), const(vTn.shape), const(smallT.shape), const(sfxw.shape),
                hbm, hbm, hbm]
    out = pl.pallas_call(
        functools.partial(_fox_decode_body, n_pages=n_pages),
        grid_spec=pltpu.PrefetchScalarGridSpec(
            num_scalar_prefetch=1, grid=(db,), in_specs=in_specs,
            out_specs=pl.BlockSpec((1, FOX_WIDTH, 1), lambda b, pt: (b, 0, 0)),
            scratch_shapes=[pltpu.VMEM((ring, g_pages, FOX_HEADS, FOX_HEAD_DIM, page), F32),
                            pltpu.VMEM((ring, g_pages, FOX_HEADS, FOX_HEAD_DIM, page), F32),
                            pltpu.VMEM((ring, g_pages, FOX_HEADS, page), F32),
                            pltpu.SemaphoreType.DMA((ring,)),
                            pltpu.VMEM((FOX_WIDTH, page), F32), pltpu.VMEM((FOX_WIDTH, page), F32),
                            pltpu.VMEM((FOX_HEADS, page), F32), pltpu.VMEM((FOX_HEADS, page), F32),
                            pltpu.VMEM((FOX_HEADS, page), F32), pltpu.VMEM((FOX_HEADS, page), F32),
                            pltpu.VMEM((g_pages, FOX_HEADS, page), F32)]),
        out_shape=jax.ShapeDtypeStruct((db, FOX_WIDTH, 1), F32),
        compiler_params=pltpu.CompilerParams(dimension_semantics=("arbitrary",), vmem_limit_bytes=VMEM_LIMIT),
        name="fox_decode",
    )(page_table.reshape(-1), qT, kTn, vTn, smallT, sfxw, cache_kT, cache_vT, cache_lfT)
    return out.reshape(db, FOX_WIDTH)


def _gdn_decode_body(qkT_ref, vg_ref, smallT_ref, s_ref, o_ref, snew_ref):
    b = pl.program_id(0)
    qk = _take_lane(qkT_ref[...], b)
    gbeta = _take_lane(smallT_ref[FOX_HEADS:FOX_HEADS + 2 * GDN_HEADS, :], b)
    for h in range(GDN_HEADS):
        rs = slice(h * GDN_HEAD_DIM, (h + 1) * GDN_HEAD_DIM)
        q = qk[rs, :] * (GDN_HEAD_DIM ** -0.5)
        k = qk[GDN_WIDTH + h * GDN_HEAD_DIM:GDN_WIDTH + (h + 1) * GDN_HEAD_DIM, :]
        v = vg_ref[0, :, rs]
        st = s_ref[0, h] * jnp.exp(gbeta[h:h + 1, :])
        delta = (v - jnp.sum(k * st, axis=0, keepdims=True)) * gbeta[GDN_HEADS + h:GDN_HEADS + h + 1, :]
        st = st + k * delta
        snew_ref[0, h] = st
        o_ref[0, :, rs] = jnp.sum(q * st, axis=0, keepdims=True)


def _gdn_decode(qkT, vg, smallT, state):
    db = vg.shape[0]
    const = lambda shape: pl.BlockSpec(shape, lambda b: (0,) * len(shape))
    st_spec = pl.BlockSpec((1, GDN_HEADS, GDN_HEAD_DIM, GDN_HEAD_DIM), lambda b: (b, 0, 0, 0))
    row_spec = pl.BlockSpec((1, 1, GDN_WIDTH), lambda b: (b, 0, 0))
    o, s_new = pl.pallas_call(
        _gdn_decode_body,
        grid=(db,),
        in_specs=[const(qkT.shape), row_spec, const(smallT.shape), st_spec],
        out_specs=[row_spec, st_spec],
        out_shape=[jax.ShapeDtypeStruct((db, 1, GDN_WIDTH), F32), jax.ShapeDtypeStruct(state.shape, F32)],
        compiler_params=pltpu.CompilerParams(dimension_semantics=("arbitrary",)),
        name="gdn_decode",
    )(qkT, vg.reshape(db, 1, GDN_WIDTH), smallT, state)
    return o.reshape(db, GDN_WIDTH), s_new


def _mix_out_body(x_ref, oa_ref, og_ref, z_ref, ga_ref, gb_ref, ggdn_ref, wbf_ref, wbg_ref, wout_ref, gpost_ref,
                  o_ref):
    og = og_ref[...]
    z = z_ref[...].astype(F32)
    parts = []
    for h in range(GDN_HEADS):
        ls = slice(h * GDN_HEAD_DIM, (h + 1) * GDN_HEAD_DIM)
        parts.append(_rms(og[:, ls], ggdn_ref[...]) * _silu(z[:, ls]))
    ogn = jnp.concatenate(parts, axis=-1).astype(BF16)
    y_a = jnp.dot(oa_ref[...], wbf_ref[...], preferred_element_type=F32)
    y_g = jnp.dot(ogn, wbg_ref[...], preferred_element_type=F32)
    y = (ga_ref[...].astype(F32) * y_a + gb_ref[...].astype(F32) * y_g).astype(BF16)
    y = jnp.dot(y, wout_ref[...], preferred_element_type=F32)
    o_ref[...] = x_ref[...] + _rms(y, gpost_ref[...])


def _mix_out(x, o_a, o_g, z, ga, gb, g_gdn, wbf, wbg, wout, g_post, tm):
    n, d = x.shape
    row = lambda width: pl.BlockSpec((tm, width), lambda i: (i, 0))
    consts = [g_gdn, wbf, wbg, wout, g_post]
    return pl.pallas_call(
        _mix_out_body,
        grid=(pl.cdiv(n, tm),),
        in_specs=[row(d), row(FOX_WIDTH), row(GDN_WIDTH), row(GDN_WIDTH), row(d), row(d)]
        + [_const_spec(a.shape) for a in consts],
        out_specs=row(d),
        out_shape=jax.ShapeDtypeStruct((n, d), F32),
        compiler_params=pltpu.CompilerParams(dimension_semantics=("arbitrary",), vmem_limit_bytes=VMEM_LIMIT),
        name="mix_out",
    )(x, o_a, o_g, z, ga, gb, *consts)


def _chunk_gates(col_ref, tri_ref, rows, heads, lane, rr, cc):
    colc = col_ref[rows, :]
    cs = _dot3(tri_ref[...], colc)
    keep = cc <= rr
    out = []
    for head in heads:
        gc = jnp.sum(jnp.where(lane == FOX_HEADS + head, cs, 0.0), axis=-1, keepdims=True)
        beta = jnp.sum(jnp.where(lane == FOX_HEADS + GDN_HEADS + head, colc, 0.0), axis=-1, keepdims=True)
        cmat = jnp.broadcast_to(gc, (CHUNK, CHUNK))
        decay = jnp.where(keep, jnp.exp(jnp.where(keep, cmat - cmat.T, 0.0)), 0.0)
        out.append((gc, beta, decay))
    return out


def _gdn_lower_body(k_ref, col_ref, tri_ref, l_ref):
    n_chunks = k_ref.shape[0] // CHUNK
    nt = (((1,), (1,)), ((), ()))
    rr = lax.broadcasted_iota(jnp.int32, (CHUNK, CHUNK), 0)
    cc = lax.broadcasted_iota(jnp.int32, (CHUNK, CHUNK), 1)
    lane = lax.broadcasted_iota(jnp.int32, (CHUNK, LANES), 1)

    def chunk_pair(i, _):
        work = []
        for c in range(2):
            n = 2 * i + c
            rows = pl.ds(pl.multiple_of(n * CHUNK, CHUNK), CHUNK)
            work.append((n, rows, _chunk_gates(col_ref, tri_ref, rows, range(GDN_HEADS), lane, rr, cc)))
        prods = []
        for n, rows, gates in work:
            for e, (_, beta, decay) in enumerate(gates):
                k = k_ref[rows, e * GDN_HEAD_DIM:(e + 1) * GDN_HEAD_DIM]
                kk = lax.dot_general((k * beta).astype(BF16), k.astype(BF16), nt, preferred_element_type=F32)
                prods.append((n, e, kk, decay))
        for n, e, kk, decay in prods:
            l_ref[0, e, n] = jnp.where(cc < rr, kk * decay, 0.0)
        return 0

    lax.fori_loop(0, n_chunks // 2, chunk_pair, 0)


def _unit_lower_inverse_body(l_ref, t_ref, lt_s, tt_s):
    for i in range(CHUNK):
        lt_s[i * CHUNK:(i + 1) * CHUNK, :] = l_ref[0, :, i, :].T
    sub = lax.broadcasted_iota(jnp.int32, (8, LANES), 0)
    for i in range(CHUNK):
        for kg in range(i // 8 + 1):
            acc = jnp.where(sub + 8 * kg == i, 1.0, 0.0)
            for j in range(8 * kg, i):
                acc = acc - lt_s[i * CHUNK + j:i * CHUNK + j + 1, :] * tt_s[j * CHUNK + 8 * kg:j * CHUNK + 8 * kg + 8, :]
            tt_s[i * CHUNK + 8 * kg:i * CHUNK + 8 * kg + 8, :] = acc
        for kg in range(i // 8 + 1, CHUNK // 8):
            tt_s[i * CHUNK + 8 * kg:i * CHUNK + 8 * kg + 8, :] = jnp.zeros((8, LANES), F32)
    for i in range(CHUNK):
        t_ref[0, :, i, :] = tt_s[i * CHUNK:(i + 1) * CHUNK, :].T


def _unit_lower_inverse(low):
    groups = low.shape[0]
    spec = pl.BlockSpec((1, LANES, CHUNK, CHUNK), lambda g: (g, 0, 0, 0))
    return pl.pallas_call(
        _unit_lower_inverse_body,
        grid=(groups,),
        in_specs=[spec], out_specs=spec,
        out_shape=jax.ShapeDtypeStruct(low.shape, F32),
        scratch_shapes=[pltpu.VMEM((CHUNK * CHUNK, LANES), F32), pltpu.VMEM((CHUNK * CHUNK, LANES), F32)],
        compiler_params=pltpu.CompilerParams(dimension_semantics=("arbitrary",), vmem_limit_bytes=VMEM_LIMIT),
        name="unit_lower_inverse",
    )(low)


def _gdn_prompt_body(q_ref, k_ref, v_ref, col_ref, tri_ref, t_ref, o_ref, s_ref, state_s,
                     u_s, w_s, qe_s, ke_s, at_s, eg_s):
    seg = pl.program_id(1)
    n_chunks = q_ref.shape[0] // CHUNK
    nt = (((1,), (1,)), ((), ()))
    tn = (((0,), (0,)), ((), ()))
    rr = lax.broadcasted_iota(jnp.int32, (CHUNK, CHUNK), 0)
    cc = lax.broadcasted_iota(jnp.int32, (CHUNK, CHUNK), 1)
    lane = lax.broadcasted_iota(jnp.int32, (CHUNK, LANES), 1)

    @pl.when(seg == 0)
    def _():
        state_s[...] = jnp.zeros(state_s.shape, F32)

    heads = range(GDN_HEADS)
    hl = [slice(e * GDN_HEAD_DIM, (e + 1) * GDN_HEAD_DIM) for e in heads]
    fdot = functools.partial(jnp.dot, preferred_element_type=F32)

    def prepare(i, _):
        work = []
        for c in range(2):
            n = 2 * i + c
            rows = pl.ds(pl.multiple_of(n * CHUNK, CHUNK), CHUNK)
            work.append((n, rows, _chunk_gates(col_ref, tri_ref, rows, heads, lane, rr, cc)))
        pending = []
        for n, rows, gates in work:
            for e, (gc, beta, decay) in enumerate(gates):
                q = q_ref[rows, hl[e]] * (GDN_HEAD_DIM ** -0.5)
                k = k_ref[rows, hl[e]]
                v = v_ref[rows, hl[e]]
                egc = jnp.exp(gc)
                g_last = gc[CHUNK - 1:CHUNK, :]
                tinv = t_ref[0, e, n]
                t_hi = tinv.astype(BF16)
                t_lo = (tinv - t_hi.astype(F32)).astype(BF16)
                rhs = jnp.concatenate([v * beta, k * (beta * egc)], axis=1)
                r_hi = rhs.astype(BF16)
                r_lo = (rhs - r_hi.astype(F32)).astype(BF16)
                uw = fdot(jnp.concatenate([t_hi, t_lo, t_hi], axis=1), jnp.concatenate([r_hi, r_hi, r_lo], axis=0))
                attn = lax.dot_general(q.astype(BF16), k.astype(BF16), nt, preferred_element_type=F32)
                pending.append((n, rows, e, uw, attn, decay))
                qe_s[rows, hl[e]] = (q * egc).astype(BF16)
                ke_s[e, n] = (k * jnp.exp(g_last - gc)).T.astype(BF16)
                eg_s[e, pl.ds(n, 1), :] = jnp.broadcast_to(jnp.exp(g_last), (1, LANES))
        for n, rows, e, uw, attn, decay in pending:
            u_s[rows, hl[e]] = uw[:, :GDN_HEAD_DIM]
            w_s[rows, hl[e]] = uw[:, GDN_HEAD_DIM:].astype(BF16)
            at_s[e, rows, :] = (attn * decay).astype(BF16)
        return 0

    lax.fori_loop(0, n_chunks // 2, prepare, 0)

    def scan(n, states):
        rows = pl.ds(pl.multiple_of(n * CHUNK, CHUNK), CHUNK)
        sbs = [states[e].astype(BF16) for e in heads]
        ws = [fdot(w_s[rows, hl[e]], sbs[e]) for e in heads]
        qs = [fdot(qe_s[rows, hl[e]], sbs[e]) for e in heads]
        v_new = [(u_s[rows, hl[e]] - ws[e]).astype(BF16) for e in heads]
        av = [fdot(at_s[e, rows, :], v_new[e]) for e in heads]
        kv = [fdot(ke_s[e, n], v_new[e]) for e in heads]
        for e in heads:
            o_ref[rows, hl[e]] = qs[e] + av[e]
        return tuple(states[e] * eg_s[e, pl.ds(n, 1), :] + kv[e] for e in heads)

    init = tuple(state_s[e] for e in range(GDN_HEADS))
    final = lax.fori_loop(0, n_chunks, scan, init)
    for e in range(GDN_HEADS):
        state_s[e] = final[e]
        s_ref[0, e] = final[e]


GDN_SEGMENT = 512


def _gdn_prompt(qkv, col, tri64, batch, seq):
    n = qkv.shape[0]
    n_chunks = seq // CHUNK
    n_seg = seq // GDN_SEGMENT
    seg_chunks = GDN_SEGMENT // CHUNK
    blk = lambda off: pl.BlockSpec((GDN_SEGMENT, GDN_WIDTH), lambda b, s: (b * n_seg + s, off))
    col_spec = pl.BlockSpec((GDN_SEGMENT, LANES), lambda b, s: (b * n_seg + s, 0))
    tri_spec = pl.BlockSpec((CHUNK, CHUNK), lambda b, s: (0, 0))
    mat_spec = pl.BlockSpec((1, GDN_HEADS, seg_chunks, CHUNK, CHUNK), lambda b, s: (b, 0, s, 0, 0))
    params = pltpu.CompilerParams(dimension_semantics=("arbitrary",) * 2, vmem_limit_bytes=VMEM_LIMIT)
    low = pl.pallas_call(
        _gdn_lower_body,
        grid=(batch, n_seg),
        in_specs=[blk(1), col_spec, tri_spec],
        out_specs=mat_spec,
        out_shape=jax.ShapeDtypeStruct((batch, GDN_HEADS, n_chunks, CHUNK, CHUNK), F32),
        compiler_params=params,
        name="gdn_lower",
    )(qkv, col, tri64)
    n_sys = batch * GDN_HEADS * n_chunks
    tinv = _unit_lower_inverse(low.reshape(n_sys // LANES, LANES, CHUNK, CHUNK)).reshape(low.shape)
    state_spec = pl.BlockSpec((1, GDN_HEADS, GDN_HEAD_DIM, GDN_HEAD_DIM), lambda b, s: (b, 0, 0, 0))
    return pl.pallas_call(
        _gdn_prompt_body,
        grid=(batch, n_seg),
        in_specs=[blk(0), blk(1), blk(2), col_spec, tri_spec, mat_spec],
        out_specs=[blk(0), state_spec],
        out_shape=[jax.ShapeDtypeStruct((n, GDN_WIDTH), F32),
                   jax.ShapeDtypeStruct((batch, GDN_HEADS, GDN_HEAD_DIM, GDN_HEAD_DIM), F32)],
        scratch_shapes=[pltpu.VMEM((GDN_HEADS, GDN_HEAD_DIM, GDN_HEAD_DIM), F32),
                        pltpu.VMEM((GDN_SEGMENT, GDN_WIDTH), F32), pltpu.VMEM((GDN_SEGMENT, GDN_WIDTH), BF16),
                        pltpu.VMEM((GDN_SEGMENT, GDN_WIDTH), BF16),
                        pltpu.VMEM((GDN_HEADS, seg_chunks, GDN_HEAD_DIM, CHUNK), BF16),
                        pltpu.VMEM((GDN_HEADS, GDN_SEGMENT, CHUNK), BF16),
                        pltpu.VMEM((GDN_HEADS, seg_chunks, LANES), F32)],
        compiler_params=params,
        name="gdn_prompt",
    )(qkv, qkv, qkv, col, tri64, tinv)


def _fox_prompt(q, col, kT, vT, cT, tq):
    n = q.shape[0]
    batch, _, seq = kT.shape
    nq = seq // tq
    pair = 2 * FOX_HEAD_DIM
    return pl.pallas_call(
        functools.partial(_fox_prompt_body, tq=tq),
        grid=(batch, FOX_HEADS // 2, nq),
        in_specs=[
            pl.BlockSpec((tq, pair), lambda b, p, i: (b * nq + i, p)),
            pl.BlockSpec((tq, LANES), lambda b, p, i: (b * nq + i, 0)),
            pl.BlockSpec((1, pair, seq), lambda b, p, i: (b, p, 0)),
            pl.BlockSpec((1, pair, seq), lambda b, p, i: (b, p, 0)),
            pl.BlockSpec((1, FOX_HEADS, seq), lambda b, p, i: (b, 0, 0)),
        ],
        out_specs=pl.BlockSpec((tq, pair), lambda b, p, i: (b * nq + i, p)),
        out_shape=jax.ShapeDtypeStruct((n, FOX_WIDTH), BF16),
        scratch_shapes=[pltpu.VMEM((2, FOX_HEAD_DIM + 16, seq), BF16), pltpu.VMEM((pair, seq), BF16)],
        compiler_params=pltpu.CompilerParams(dimension_semantics=("arbitrary",) * 3, vmem_limit_bytes=VMEM_LIMIT),
        name="fox_prompt",
    )(q, col, kT, vT, cT)


ROW_TILE = 512
FOX_Q_TILE = 512


def kernel(x_prompt, x_sample, cache_k, cache_v, cache_logf, state_gdn, state_conv, page_table,
           ffn1_norm_pre, ffn1_w_gate, ffn1_w_up, ffn1_w_down, ffn1_norm_post,
           mix_norm_pre, w_in, fox_forget_bias, gdn_conv_w, gdn_dt_bias, gdn_a_log, gdn_out_norm,
           w_branch_fox, w_branch_gdn, w_out, mix_norm_post,
           ffn2_norm_pre, ffn2_w_gate, ffn2_w_up, ffn2_w_down, ffn2_norm_post):
    batch, seq, d = x_prompt.shape
    db = x_sample.shape[0]
    assert x_sample.shape[1] == 1, "sample group carries one new token per sequence"
    depth = w_in.shape[0]
    xp = x_prompt.reshape(batch * seq, d)
    xs = x_sample.reshape(db, d)
    row = lambda v: v.reshape(1, -1).astype(F32)
    st_p, st_s = [], []
    for l in range(depth):
        ffn1 = _ffn_weights(ffn1_w_gate[l], ffn1_w_up[l], ffn1_w_down[l])
        ffn2 = _ffn_weights(ffn2_w_gate[l], ffn2_w_up[l], ffn2_w_down[l])
        w = _mixer_weights(w_in[l], fox_forget_bias[l], gdn_conv_w[l], gdn_dt_bias[l], gdn_a_log[l])
        out_w = (row(gdn_out_norm[l]), w_branch_fox[l].astype(BF16), w_branch_gdn[l].astype(BF16),
                 w_out[l].astype(BF16), row(mix_norm_post[l]))

        xp = _ffn(xp, row(ffn1_norm_pre[l]), *ffn1, row(ffn1_norm_post[l]), ROW_TILE)
        (q, kT, vT, lfT, cT, _, col, qkv, conv_p, z, ga, gb) = _proj_prompt(
            xp, row(mix_norm_pre[l]), w, batch, seq, ROW_TILE)
        o_a = _fox_prompt(q, col, kT, vT, cT, FOX_Q_TILE)
        o_g, s_p = _gdn_prompt(qkv, col, w["tri"][:CHUNK, :CHUNK], batch, seq)
        xp = _mix_out(xp, o_a, o_g, z, ga, gb, *out_w, ROW_TILE)
        xp = _ffn(xp, row(ffn2_norm_pre[l]), *ffn2, row(ffn2_norm_post[l]), ROW_TILE)
        to_heads = lambda t: t.reshape(batch, FOX_HEADS, FOX_HEAD_DIM, seq).transpose(0, 3, 1, 2)
        st_p.append((to_heads(kT), to_heads(vT), lfT.transpose(0, 2, 1), s_p, conv_p))

        xs = _ffn(xs, row(ffn1_norm_pre[l]), *ffn1, row(ffn1_norm_post[l]), db)
        buf = state_conv[l].transpose(1, 0, 2)
        bufT = state_conv[l][:, :, :2 * GDN_WIDTH].transpose(1, 2, 0)
        (qTs, kTs, vTs, small, smallT, u, qkT, vg, zs, gas, gbs) = _proj_sample(
            xs, row(mix_norm_pre[l]), w, buf, bufT)
        o_as = _fox_decode(page_table, qTs, kTs, vTs, smallT,
                           cache_k[l].transpose(0, 2, 3, 1), cache_v[l].transpose(0, 2, 3, 1),
                           cache_logf[l].transpose(0, 2, 1))
        o_gs, s_s = _gdn_decode(qkT, vg, smallT, state_gdn[l])
        xs = _mix_out(xs, o_as.astype(BF16), o_gs, zs, gas, gbs, *out_w, db)
        xs = _ffn(xs, row(ffn2_norm_pre[l]), *ffn2, row(ffn2_norm_post[l]), db)
        new_conv = jnp.concatenate([state_conv[l][:, 1:], u[:, None, :]], axis=1)
        st_s.append((kTs.T.reshape(db, 1, FOX_HEADS, FOX_HEAD_DIM), vTs.T.reshape(db, 1, FOX_HEADS, FOX_HEAD_DIM),
                     small[:, :FOX_HEADS].reshape(db, 1, FOX_HEADS), s_s, new_conv))

    stack = lambda states, i: jnp.stack([s[i] for s in states], axis=0)
    return (xp.reshape(batch, seq, d), xs.reshape(db, 1, d),
            *[stack(st_p, i) for i in range(5)], *[stack(st_s, i) for i in range(5)])
```

```python
import functools

import jax
import jax.numpy as jnp
from jax import lax
from jax.experimental import pallas as pl
from jax.experimental.pallas import tpu as pltpu

F32 = jnp.float32
BF16 = jnp.bfloat16
EPS = 1e-6
LOG2E = 1.4426950408889634
LANES = 128
VMEM_LIMIT = 56 * 1024 * 1024

FOX_HEADS = 8
FOX_HEAD_DIM = 64
FOX_WIDTH = FOX_HEADS * FOX_HEAD_DIM
GDN_HEADS = 4
GDN_HEAD_DIM = 128
GDN_WIDTH = GDN_HEADS * GDN_HEAD_DIM
CONV_WIDTH = 4
CONV_CH = 3 * GDN_WIDTH
CHUNK = 64
FF_CHUNK = 256


def _rms(x, g):
    return x * lax.rsqrt(jnp.mean(x * x, axis=-1, keepdims=True) + EPS) * g


def _sigmoid(x):
    return 1.0 / (1.0 + jnp.exp(-x))


def _silu(x):
    return x * _sigmoid(x)


def _const_spec(shape):
    n = len(shape)
    return pl.BlockSpec(shape, lambda *_: (0,) * n, pipeline_mode=pl.Buffered(1))


def _ffn_body(x_ref, gpre_ref, wg_ref, wu_ref, wd_ref, gpost_ref, o_ref):
    x = x_ref[...]
    h = _rms(x, gpre_ref[...]).astype(BF16)
    y = jnp.zeros(x.shape, F32)
    for j in range(wg_ref.shape[1] // FF_CHUNK):
        cols = slice(j * FF_CHUNK, (j + 1) * FF_CHUNK)
        g = jnp.dot(h, wg_ref[:, cols].astype(BF16), preferred_element_type=F32)
        u = jnp.dot(h, wu_ref[:, cols].astype(BF16), preferred_element_type=F32)
        a = (_silu(g) * u).astype(BF16)
        y = y + jnp.dot(a, wd_ref[cols, :].astype(BF16), preferred_element_type=F32)
    o_ref[...] = x + 0.5 * _rms(y, gpost_ref[...])


def _ffn(x, g_pre, wg, wu, wd, g_post, tm):
    n, d = x.shape
    row = pl.BlockSpec((tm, d), lambda i: (i, 0))
    return pl.pallas_call(
        _ffn_body,
        grid=(pl.cdiv(n, tm),),
        in_specs=[row, _const_spec((1, d)), _const_spec(wg.shape), _const_spec(wu.shape),
                  _const_spec(wd.shape), _const_spec((1, d))],
        out_specs=row,
        out_shape=jax.ShapeDtypeStruct((n, d), F32),
        compiler_params=pltpu.CompilerParams(dimension_semantics=("arbitrary",), vmem_limit_bytes=VMEM_LIMIT),
        name="ffn",
    )(x, g_pre, wg, wu, wd, g_post)


def _ffn_weights(w_gate, w_up, w_down):
    assert w_gate.shape[1] % FF_CHUNK == 0
    return w_gate, w_up, w_down


def _split3(x):
    hi = x.astype(BF16)
    r = x - hi.astype(F32)
    mid = r.astype(BF16)
    lo = (r - mid.astype(F32)).astype(BF16)
    return hi, mid, lo


def _dot3(tri_bf16, x):
    hi, mid, lo = _split3(x)
    d = functools.partial(jnp.dot, tri_bf16, preferred_element_type=F32)
    return d(hi) + d(mid) + d(lo)


def _softplus(x):
    return jnp.maximum(x, 0.0) + jnp.log(1.0 + jnp.exp(-jnp.abs(x)))


def _small_cols(small, bias_ref, aneg_ref):
    lane = lax.broadcasted_iota(jnp.int32, small.shape, 1)
    pre = small + bias_ref[...]
    sp = _softplus(pre)
    logf = pre - sp
    g = aneg_ref[...] * sp
    beta = _sigmoid(small)
    gb = jnp.where(lane < FOX_HEADS + GDN_HEADS, g, beta)
    return logf, gb


def _l2n_heads(x):
    outs = []
    for h in range(x.shape[1] // GDN_HEAD_DIM):
        xh = x[:, h * GDN_HEAD_DIM:(h + 1) * GDN_HEAD_DIM]
        outs.append(xh * lax.rsqrt(jnp.sum(xh * xh, axis=-1, keepdims=True) + EPS))
    return jnp.concatenate(outs, axis=-1)


def _proj_prompt_body(x_ref, gpre_ref, wq_ref, wkT_ref, wvT_ref, ws_ref, wc_ref, wz_ref, wga_ref, wgb_ref,
                      bias_ref, aneg_ref, convw_ref, tri_ref,
                      q_ref, kT_ref, vT_ref, lfT_ref, cT_ref, gbT_ref, col_ref, qkv_ref, cst_ref,
                      z_ref, ga_ref, gb_ref, ubuf, ccarry, *, tiles_per_seq):
    i = pl.program_id(0)
    tm = x_ref.shape[0]

    @pl.when(i % tiles_per_seq == 0)
    def _():
        ubuf[0:8, :] = jnp.zeros((8, ubuf.shape[1]), F32)
        ccarry[...] = jnp.zeros(ccarry.shape, F32)

    h = _rms(x_ref[...], gpre_ref[...]).astype(BF16)
    dot = functools.partial(jnp.dot, h, preferred_element_type=F32)
    nt = (((1,), (1,)), ((), ()))

    ubuf[8:8 + tm, :] = dot(wc_ref[...])
    small = dot(ws_ref[...])
    q_ref[...] = (dot(wq_ref[...]) * (LOG2E * FOX_HEAD_DIM ** -0.5)).astype(BF16)
    kT_ref[0] = lax.dot_general(wkT_ref[...], h, nt, preferred_element_type=F32)
    vT_ref[0] = lax.dot_general(wvT_ref[...], h, nt, preferred_element_type=F32)
    z_ref[...] = dot(wz_ref[...]).astype(BF16)
    ga_ref[...] = _sigmoid(dot(wga_ref[...])).astype(BF16)
    gb_ref[...] = _sigmoid(dot(wgb_ref[...])).astype(BF16)

    logf, gb = _small_cols(small, bias_ref, aneg_ref)
    carry = ccarry[...]
    blocks = []
    for r in range(tm // LANES):
        cb = _dot3(tri_ref[...], logf[r * LANES:(r + 1) * LANES, :]) + carry
        carry = cb[LANES - 1:LANES, :]
        blocks.append(cb)
    ccarry[...] = carry
    c = jnp.concatenate(blocks, axis=0)
    lane = lax.broadcasted_iota(jnp.int32, c.shape, 1)
    col = jnp.where(lane < FOX_HEADS, c, gb)
    col_ref[...] = col
    colT = col.T
    lfT_ref[0] = logf.T[0:FOX_HEADS, :]
    cT_ref[0] = colT[0:FOX_HEADS, :]
    gbT_ref[0] = colT[FOX_HEADS:2 * FOX_HEADS, :]

    base = 8 - (CONV_WIDTH - 1)
    conv = ubuf[base:base + tm, :] * convw_ref[0:1, :]
    for k in range(1, CONV_WIDTH):
        conv = conv + ubuf[base + k:base + k + tm, :] * convw_ref[k:k + 1, :]
    cst_ref[0] = ubuf[tm + base:tm + 8, :]
    ubuf[0:8, :] = ubuf[tm:tm + 8, :]
    conv = _silu(conv)
    qkv_ref[:, 0:2 * GDN_WIDTH] = _l2n_heads(conv[:, 0:2 * GDN_WIDTH])
    qkv_ref[:, 2 * GDN_WIDTH:] = conv[:, 2 * GDN_WIDTH:]


def _proj_prompt(x, g_pre, w, batch, seq, tm):
    n, d = x.shape
    tps = seq // tm
    row = lambda width: pl.BlockSpec((tm, width), lambda i: (i, 0))
    seqT = lambda rows: pl.BlockSpec((1, rows, tm), lambda i: (i // tps, 0, i % tps))
    consts = [g_pre, w["q"], w["kT"], w["vT"], w["small"], w["conv"], w["z"], w["ga"], w["gb"],
              w["bias"], w["aneg"], w["convw"], w["tri"]]
    out_shape = [
        jax.ShapeDtypeStruct((n, FOX_WIDTH), BF16),
        jax.ShapeDtypeStruct((batch, FOX_WIDTH, seq), F32),
        jax.ShapeDtypeStruct((batch, FOX_WIDTH, seq), F32),
        jax.ShapeDtypeStruct((batch, FOX_HEADS, seq), F32),
        jax.ShapeDtypeStruct((batch, FOX_HEADS, seq), F32),
        jax.ShapeDtypeStruct((batch, 2 * GDN_HEADS, seq), F32),
        jax.ShapeDtypeStruct((n, LANES), F32),
        jax.ShapeDtypeStruct((n, CONV_CH), F32),
        jax.ShapeDtypeStruct((batch, CONV_WIDTH - 1, CONV_CH), F32),
        jax.ShapeDtypeStruct((n, GDN_WIDTH), BF16),
        jax.ShapeDtypeStruct((n, d), BF16),
        jax.ShapeDtypeStruct((n, d), BF16),
    ]
    out_specs = [row(FOX_WIDTH), seqT(FOX_WIDTH), seqT(FOX_WIDTH), seqT(FOX_HEADS), seqT(FOX_HEADS),
                 seqT(2 * GDN_HEADS), row(LANES), row(CONV_CH),
                 pl.BlockSpec((1, CONV_WIDTH - 1, CONV_CH), lambda i: (i // tps, 0, 0)),
                 row(GDN_WIDTH), row(d), row(d)]
    return pl.pallas_call(
        functools.partial(_proj_prompt_body, tiles_per_seq=tps),
        grid=(n // tm,),
        in_specs=[row(d)] + [_const_spec(a.shape) for a in consts],
        out_specs=out_specs,
        out_shape=out_shape,
        scratch_shapes=[pltpu.VMEM((tm + 8, CONV_CH), F32), pltpu.VMEM((1, LANES), F32)],
        compiler_params=pltpu.CompilerParams(dimension_semantics=("arbitrary",), vmem_limit_bytes=VMEM_LIMIT),
        name="proj_prompt",
    )(x, *consts)


def _mixer_weights(w_in, b_f, conv_w, dt_bias, a_log):
    sizes = (FOX_WIDTH, FOX_WIDTH, FOX_WIDTH, FOX_HEADS, CONV_CH, GDN_HEADS, GDN_HEADS, GDN_WIDTH)
    d = w_in.shape[0]
    offs = [0]
    for s in sizes:
        offs.append(offs[-1] + s)
    part = lambda k: w_in[:, offs[k]:offs[k + 1]]
    ga0 = offs[-1]
    n_small = FOX_HEADS + 2 * GDN_HEADS
    small = jnp.concatenate([part(3), part(5), part(6), jnp.zeros((d, LANES - n_small), w_in.dtype)], axis=1)
    pad = lambda v, lo: jnp.zeros((1, LANES), F32).at[0, lo:lo + v.shape[0]].set(v.astype(F32))
    r = lax.broadcasted_iota(jnp.int32, (LANES, LANES), 0)
    c = lax.broadcasted_iota(jnp.int32, (LANES, LANES), 1)
    return {
        "q": part(0).astype(BF16), "kT": part(1).T.astype(BF16), "vT": part(2).T.astype(BF16),
        "small": small.astype(BF16), "conv": part(4).astype(BF16), "z": part(7).astype(BF16),
        "ga": w_in[:, ga0:ga0 + d].astype(BF16), "gb": w_in[:, ga0 + d:ga0 + 2 * d].astype(BF16),
        "bias": pad(b_f, 0) + pad(dt_bias, FOX_HEADS),
        "aneg": pad(-jnp.exp(a_log.astype(F32)), FOX_HEADS),
        "convw": conv_w.astype(F32),
        "tri": (c <= r).astype(BF16),
    }


NEG_BIG = -1e30


def _split3_f32(x):
    hi, mid, lo = _split3(x)
    return hi.astype(F32), mid.astype(F32), lo.astype(F32)


def _fox_prompt_body(q_ref, col_ref, kT_ref, vT_ref, cT_ref, o_ref, ka_ref, vb_ref, *, tq):
    p = pl.program_id(1)
    qi = pl.program_id(2)
    hd = FOX_HEAD_DIM
    n_aug = 16

    @pl.when(qi == 0)
    def _():
        vb_ref[...] = vT_ref[0].astype(BF16)
        sub = lax.broadcasted_iota(jnp.int32, (n_aug, ka_ref.shape[2]), 0)
        for e in range(2):
            ka_ref[e, 0:hd, :] = kT_ref[0, e * hd:(e + 1) * hd, :].astype(BF16)
            hi, mid, lo = _split3_f32(cT_ref[0, pl.ds(2 * p + e, 1), :] * LOG2E)
            aug = jnp.where(sub < 3, 1.0, jnp.where(sub == 3, -hi, jnp.where(sub == 4, -mid,
                            jnp.where(sub == 5, -lo, 0.0))))
            ka_ref[e, hd:hd + n_aug, :] = aug.astype(BF16)

    nt = (((1,), (1,)), ((), ()))
    col = col_ref[...]
    lane = lax.broadcasted_iota(jnp.int32, col.shape, 1)
    hrows = [slice(e * hd, (e + 1) * hd) for e in range(2)]
    q_pair = q_ref[...].astype(F32)
    qs = []
    for e in range(2):
        cq = jnp.sum(jnp.where(lane == 2 * p + e, col, 0.0), axis=-1, keepdims=True) * LOG2E
        hi, mid, lo = _split3_f32(cq)
        extra = jnp.where(lane == hd, hi, jnp.where(lane == hd + 1, mid, jnp.where(lane == hd + 2, lo,
                          jnp.where(lane < hd + 6, 1.0, 0.0))))
        qh = q_pair if e == 0 else pltpu.roll(q_pair, hd, axis=1)
        qs.append(jnp.where(lane < hd, qh, extra)[:, 0:hd + n_aug].astype(BF16))

    def block(kb, carry, masked):
        ks = pl.ds(pl.multiple_of(kb * tq, tq), tq)
        raw = [jnp.dot(qs[e], ka_ref[e, :, ks], preferred_element_type=F32) for e in range(2)]
        stats = []
        for e in range(2):
            m, l, _ = carry[e]
            s = raw[e]
            if masked:
                r = lax.broadcasted_iota(jnp.int32, s.shape, 0)
                c = lax.broadcasted_iota(jnp.int32, s.shape, 1)
                s = jnp.where(c <= r, s, NEG_BIG)
            m_new = jnp.maximum(m, jnp.max(s, axis=-1, keepdims=True))
            pr = jnp.exp2(s - m_new)
            alpha = jnp.exp2(m - m_new)
            stats.append((m_new, alpha * l + jnp.sum(pr, axis=-1, keepdims=True), alpha, pr.astype(BF16)))
        pvs = [lax.dot_general(stats[e][3], vb_ref[hrows[e], ks], nt, preferred_element_type=F32) for e in range(2)]
        return tuple((stats[e][0], stats[e][1], stats[e][2] * carry[e][2] + pvs[e]) for e in range(2))

    init = tuple((jnp.full((tq, 1), NEG_BIG, F32), jnp.zeros((tq, 1), F32), jnp.zeros((tq, hd), F32))
                 for _ in range(2))
    carry = lax.fori_loop(0, qi, lambda kb, c: block(kb, c, False), init)
    carry = block(qi, carry, True)
    o_ref[...] = jnp.concatenate([acc / l for (_, l, acc) in carry], axis=-1).astype(o_ref.dtype)


def _proj_sample_body(x_ref, gpre_ref, wqT_ref, wkT_ref, wvT_ref, ws_ref, wc_ref, wcT_ref, wz_ref, wga_ref, wgb_ref,
                      bias_ref, aneg_ref, convw_ref, convwT_ref, buf_ref, bufT_ref,
                      qT_ref, kT_ref, vT_ref, small_ref, smallT_ref, u_ref, qkT_ref, vg_ref, z_ref, ga_ref, gb_ref):
    h = _rms(x_ref[...], gpre_ref[...]).astype(BF16)
    dot = functools.partial(jnp.dot, h, preferred_element_type=F32)
    dot_t = lambda w: lax.dot_general(w, h, (((1,), (1,)), ((), ())), preferred_element_type=F32)
    qT_ref[...] = dot_t(wqT_ref[...]) * (FOX_HEAD_DIM ** -0.5)
    kT_ref[...] = dot_t(wkT_ref[...])
    vT_ref[...] = dot_t(wvT_ref[...])
    z_ref[...] = dot(wz_ref[...]).astype(BF16)
    ga_ref[...] = _sigmoid(dot(wga_ref[...])).astype(BF16)
    gb_ref[...] = _sigmoid(dot(wgb_ref[...])).astype(BF16)

    logf, gb = _small_cols(dot(ws_ref[...]), bias_ref, aneg_ref)
    lane = lax.broadcasted_iota(jnp.int32, logf.shape, 1)
    small = jnp.where(lane < FOX_HEADS, logf, gb)
    small_ref[...] = small
    pad = jnp.concatenate([small, jnp.zeros((LANES - small.shape[0], LANES), F32)], axis=0)
    smallT_ref[...] = pad.T[:, 0:small.shape[0]]

    u = dot(wc_ref[...])
    u_ref[...] = u
    vs = slice(2 * GDN_WIDTH, CONV_CH)
    conv_v = u[:, vs] * convw_ref[CONV_WIDTH - 1:CONV_WIDTH, vs]
    conv_qk = dot_t(wcT_ref[...]) * convwT_ref[:, CONV_WIDTH - 1:CONV_WIDTH]
    for k in range(CONV_WIDTH - 1):
        conv_v = conv_v + buf_ref[k][:, vs] * convw_ref[k:k + 1, vs]
        conv_qk = conv_qk + bufT_ref[k] * convwT_ref[:, k:k + 1]
    vg_ref[...] = _silu(conv_v)
    conv_qk = _silu(conv_qk)
    for hd in range(2 * GDN_HEADS):
        rs = slice(hd * GDN_HEAD_DIM, (hd + 1) * GDN_HEAD_DIM)
        xh = conv_qk[rs, :]
        qkT_ref[rs, :] = xh * lax.rsqrt(jnp.sum(xh * xh, axis=0, keepdims=True) + EPS)


def _proj_sample(x, g_pre, w, buf, bufT):
    db, d = x.shape
    ins = [x, g_pre, w["q"].T, w["kT"], w["vT"], w["small"], w["conv"], w["conv"][:, :2 * GDN_WIDTH].T, w["z"],
           w["ga"], w["gb"], w["bias"], w["aneg"], w["convw"], w["convw"][:, :2 * GDN_WIDTH].T, buf, bufT]
    out_shape = [
        jax.ShapeDtypeStruct((FOX_WIDTH, db), F32), jax.ShapeDtypeStruct((FOX_WIDTH, db), F32),
        jax.ShapeDtypeStruct((FOX_WIDTH, db), F32),
        jax.ShapeDtypeStruct((db, LANES), F32), jax.ShapeDtypeStruct((LANES, db), F32),
        jax.ShapeDtypeStruct((db, CONV_CH), F32), jax.ShapeDtypeStruct((2 * GDN_WIDTH, db), F32),
        jax.ShapeDtypeStruct((db, GDN_WIDTH), F32), jax.ShapeDtypeStruct((db, GDN_WIDTH), BF16),
        jax.ShapeDtypeStruct((db, d), BF16), jax.ShapeDtypeStruct((db, d), BF16),
    ]
    return pl.pallas_call(
        _proj_sample_body, out_shape=out_shape,
        compiler_params=pltpu.CompilerParams(vmem_limit_bytes=VMEM_LIMIT),
        name="proj_sample",
    )(*ins)


def _take_lane(x, idx):
    lane = lax.broadcasted_iota(jnp.int32, x.shape, 1)
    return jnp.sum(jnp.where(lane == idx, x, 0.0), axis=-1, keepdims=True)


PAGES_PER_STEP = 8
DECODE_RING = 4


def _fox_decode_body(pt_ref, qT_ref, kTn_ref, vTn_ref, smallT_ref, sfxw_ref, ck_hbm, cv_hbm, clf_hbm, o_ref,
                     kbuf, vbuf, lfbuf, sem, qb_s, acc_s, m_s, l_s, sfx_s, al_s, p_s, *, n_pages):
    g_pages = PAGES_PER_STEP
    n_steps = n_pages // g_pages
    page = kbuf.shape[-1]
    b = pl.program_id(0)
    n_seq = pl.num_programs(0)
    hd = FOX_HEAD_DIM

    def group_copies(seq, j, slot):
        base = seq * n_pages + (n_steps - 1 - j) * g_pages
        out = []
        for g in range(g_pages):
            pg = pt_ref[base + (g_pages - 1 - g)]
            out.append(pltpu.make_async_copy(ck_hbm.at[pg], kbuf.at[slot, g], sem.at[slot]))
            out.append(pltpu.make_async_copy(cv_hbm.at[pg], vbuf.at[slot, g], sem.at[slot]))
            out.append(pltpu.make_async_copy(clf_hbm.at[pg], lfbuf.at[slot, g], sem.at[slot]))
        return out

    def per_head_dot(a, bm):
        return jnp.concatenate(
            [jnp.sum(a[h * hd:(h + 1) * hd, :] * bm[h * hd:(h + 1) * hd, :], axis=0, keepdims=True)
             for h in range(FOX_HEADS)], axis=0)

    ring = kbuf.shape[0]

    @pl.when(b == 0)
    def _():
        for t in range(ring - 1):
            for c in group_copies(t // n_steps, t % n_steps, t):
                c.start()

    qb = jnp.broadcast_to(_take_lane(qT_ref[...], b), (FOX_WIDTH, page))
    kn = jnp.broadcast_to(_take_lane(kTn_ref[...], b), (FOX_WIDTH, page))
    vn = jnp.broadcast_to(_take_lane(vTn_ref[...], b), (FOX_WIDTH, page))
    qb_s[...] = qb
    m_s[...] = per_head_dot(qb, kn)
    l_s[...] = jnp.ones(l_s.shape, F32)
    lane_w = lax.broadcasted_iota(jnp.int32, (FOX_WIDTH, page), 1)
    acc_s[...] = jnp.where(lane_w == 0, vn, 0.0)
    sfx_s[...] = jnp.broadcast_to(_take_lane(smallT_ref[0:FOX_HEADS, :], b), sfx_s.shape)

    def step(j, _):
        t = b * n_steps + j
        slot = t % ring
        nxt = t + ring - 1

        @pl.when(nxt < n_seq * n_steps)
        def _():
            for c in group_copies(nxt // n_steps, nxt % n_steps, nxt % ring):
                c.start()

        for c in group_copies(b, j, slot):
            c.wait()

        lfs = lfbuf[slot].reshape(g_pages * FOX_HEADS, page)
        hi, mid, lo = _split3(lfs)
        wdot = lambda a: jnp.dot(a, sfxw_ref[...], preferred_element_type=F32)
        red = wdot(hi) + wdot(mid) + wdot(lo)
        carry = sfx_s[...]
        scores = []
        for g in range(g_pages):
            hs = slice(g * FOX_HEADS, (g + 1) * FOX_HEADS)
            bias = carry + (red[hs, 0:page] - lfs[hs, :])
            carry = carry + red[hs, page:2 * page]
            kt = kbuf[slot, g].reshape(FOX_WIDTH, page)
            scores.append(per_head_dot(kt, qb_s[...]) + bias)
        sfx_s[...] = carry

        m_old = m_s[...]
        smax = scores[0]
        for s in scores[1:]:
            smax = jnp.maximum(smax, s)
        m_new = jnp.maximum(m_old, jnp.max(smax, axis=-1, keepdims=True))
        al_s[...] = jnp.exp(m_old - m_new)
        psum = None
        for g, s in enumerate(scores):
            p = jnp.exp(s - m_new)
            p_s[g] = p
            psum = p if psum is None else psum + p
        m_s[...] = m_new
        l_s[...] = al_s[...] * l_s[...] + jnp.sum(psum, axis=-1, keepdims=True)
        for h in range(FOX_HEADS):
            rs = slice(h * hd, (h + 1) * hd)
            acc = acc_s[rs, :] * al_s[h:h + 1, :]
            for g in range(g_pages):
                acc = acc + p_s[g, h:h + 1, :] * vbuf[slot, g, h]
            acc_s[rs, :] = acc
        return 0

    lax.fori_loop(0, n_steps, step, 0)

    inv_l = 1.0 / l_s[...]
    for h in range(FOX_HEADS):
        rs = slice(h * hd, (h + 1) * hd)
        o_ref[0, rs, :] = jnp.sum(acc_s[rs, :], axis=-1, keepdims=True) * inv_l[h:h + 1, 0:1]


def _fox_decode(page_table, qT, kTn, vTn, smallT, cache_kT, cache_vT, cache_lfT):
    db, n_pages = page_table.shape
    page = cache_kT.shape[-1]
    g_pages = PAGES_PER_STEP
    ring = DECODE_RING
    assert n_pages % g_pages == 0 and db * (n_pages // g_pages) >= ring - 1
    const = lambda shape: pl.BlockSpec(shape, lambda b, pt: (0,) * len(shape))
    hbm = pl.BlockSpec(memory_space=pl.ANY)
    ii = lax.broadcasted_iota(jnp.int32, (page, 2 * page), 0)
    jj = lax.broadcasted_iota(jnp.int32, (page, 2 * page), 1)
    sfxw = ((ii >= jj) | (jj >= page)).astype(BF16)
    in_specs = [const(qT.shape), const(kTn.shape), const(vTn.shape), const(smallT.shape), const(sfxw.shape),
                hbm, hbm, hbm]
    out = pl.pallas_call(
        functools.partial(_fox_decode_body, n_pages=n_pages),
        grid_spec=pltpu.PrefetchScalarGridSpec(
            num_scalar_prefetch=1, grid=(db,), in_specs=in_specs,
            out_specs=pl.BlockSpec((1, FOX_WIDTH, 1), lambda b, pt: (b, 0, 0)),
            scratch_shapes=[pltpu.VMEM((ring, g_pages, FOX_HEADS, FOX_HEAD_DIM, page), F32),
                            pltpu.VMEM((ring, g_pages, FOX_HEADS, FOX_HEAD_DIM, page), F32),
                            pltpu.VMEM((ring, g_pages, FOX_HEADS, page), F32),
                            pltpu.SemaphoreType.DMA((ring,)),
                            pltpu.VMEM((FOX_WIDTH, page), F32), pltpu.VMEM((FOX_WIDTH, page), F32),
                            pltpu.VMEM((FOX_HEADS, page), F32), pltpu.VMEM((FOX_HEADS, page), F32),
                            pltpu.VMEM((FOX_HEADS, page), F32), pltpu.VMEM((FOX_HEADS, page), F32),
                            pltpu.VMEM((g_pages, FOX_HEADS, page), F32)]),
        out_shape=jax.ShapeDtypeStruct((db, FOX_WIDTH, 1), F32),
        compiler_params=pltpu.CompilerParams(dimension_semantics=("arbitrary",), vmem_limit_bytes=VMEM_LIMIT),
        name="fox_decode",
    )(page_table.reshape(-1), qT, kTn, vTn, smallT, sfxw, cache_kT, cache_vT, cache_lfT)
    return out.reshape(db, FOX_WIDTH)


def _gdn_decode_body(qkT_ref, vg_ref, smallT_ref, s_ref, o_ref, snew_ref):
    b = pl.program_id(0)
    qk = _take_lane(qkT_ref[...], b)
    gbeta = _take_lane(smallT_ref[FOX_HEADS:FOX_HEADS + 2 * GDN_HEADS, :], b)
    for h in range(GDN_HEADS):
        rs = slice(h * GDN_HEAD_DIM, (h + 1) * GDN_HEAD_DIM)
        q = qk[rs, :] * (GDN_HEAD_DIM ** -0.5)
        k = qk[GDN_WIDTH + h * GDN_HEAD_DIM:GDN_WIDTH + (h + 1) * GDN_HEAD_DIM, :]
        v = vg_ref[0, :, rs]
        st = s_ref[0, h] * jnp.exp(gbeta[h:h + 1, :])
        delta = (v - jnp.sum(k * st, axis=0, keepdims=True)) * gbeta[GDN_HEADS + h:GDN_HEADS + h + 1, :]
        st = st + k * delta
        snew_ref[0, h] = st
        o_ref[0, :, rs] = jnp.sum(q * st, axis=0, keepdims=True)


def _gdn_decode(qkT, vg, smallT, state):
    db = vg.shape[0]
    const = lambda shape: pl.BlockSpec(shape, lambda b: (0,) * len(shape))
    st_spec = pl.BlockSpec((1, GDN_HEADS, GDN_HEAD_DIM, GDN_HEAD_DIM), lambda b: (b, 0, 0, 0))
    row_spec = pl.BlockSpec((1, 1, GDN_WIDTH), lambda b: (b, 0, 0))
    o, s_new = pl.pallas_call(
        _gdn_decode_body,
        grid=(db,),
        in_specs=[const(qkT.shape), row_spec, const(smallT.shape), st_spec],
        out_specs=[row_spec, st_spec],
        out_shape=[jax.ShapeDtypeStruct((db, 1, GDN_WIDTH), F32), jax.ShapeDtypeStruct(state.shape, F32)],
        compiler_params=pltpu.CompilerParams(dimension_semantics=("arbitrary",)),
        name="gdn_decode",
    )(qkT, vg.reshape(db, 1, GDN_WIDTH), smallT, state)
    return o.reshape(db, GDN_WIDTH), s_new


def _mix_out_body(x_ref, oa_ref, og_ref, z_ref, ga_ref, gb_ref, ggdn_ref, wbf_ref, wbg_ref, wout_ref, gpost_ref,
                  o_ref):
    og = og_ref[...]
    z = z_ref[...].astype(F32)
    parts = []
    for h in range(GDN_HEADS):
        ls = slice(h * GDN_HEAD_DIM, (h + 1) * GDN_HEAD_DIM)
        parts.append(_rms(og[:, ls], ggdn_ref[...]) * _silu(z[:, ls]))
    ogn = jnp.concatenate(parts, axis=-1).astype(BF16)
    y_a = jnp.dot(oa_ref[...], wbf_ref[...].astype(BF16), preferred_element_type=F32)
    y_g = jnp.dot(ogn, wbg_ref[...].astype(BF16), preferred_element_type=F32)
    y = (ga_ref[...].astype(F32) * y_a + gb_ref[...].astype(F32) * y_g).astype(BF16)
    y = jnp.dot(y, wout_ref[...].astype(BF16), preferred_element_type=F32)
    o_ref[...] = x_ref[...] + _rms(y, gpost_ref[...])


def _mix_out(x, o_a, o_g, z, ga, gb, g_gdn, wbf, wbg, wout, g_post, tm):
    n, d = x.shape
    row = lambda width: pl.BlockSpec((tm, width), lambda i: (i, 0))
    consts = [g_gdn, wbf, wbg, wout, g_post]
    return pl.pallas_call(
        _mix_out_body,
        grid=(pl.cdiv(n, tm),),
        in_specs=[row(d), row(FOX_WIDTH), row(GDN_WIDTH), row(GDN_WIDTH), row(d), row(d)]
        + [_const_spec(a.shape) for a in consts],
        out_specs=row(d),
        out_shape=jax.ShapeDtypeStruct((n, d), F32),
        compiler_params=pltpu.CompilerParams(dimension_semantics=("arbitrary",), vmem_limit_bytes=VMEM_LIMIT),
        name="mix_out",
    )(x, o_a, o_g, z, ga, gb, *consts)


def _chunk_gates(col_ref, tri_ref, rows, heads, lane, rr, cc):
    colc = col_ref[rows, :]
    cs = _dot3(tri_ref[...], colc)
    keep = cc <= rr
    out = []
    for head in heads:
        gc = jnp.sum(jnp.where(lane == FOX_HEADS + head, cs, 0.0), axis=-1, keepdims=True)
        beta = jnp.sum(jnp.where(lane == FOX_HEADS + GDN_HEADS + head, colc, 0.0), axis=-1, keepdims=True)
        cmat = jnp.broadcast_to(gc, (CHUNK, CHUNK))
        decay = jnp.where(keep, jnp.exp(jnp.where(keep, cmat - cmat.T, 0.0)), 0.0)
        out.append((gc, beta, decay))
    return out


def _gdn_lower_body(k_ref, col_ref, tri_ref, l_ref):
    n_chunks = k_ref.shape[0] // CHUNK
    nt = (((1,), (1,)), ((), ()))
    rr = lax.broadcasted_iota(jnp.int32, (CHUNK, CHUNK), 0)
    cc = lax.broadcasted_iota(jnp.int32, (CHUNK, CHUNK), 1)
    lane = lax.broadcasted_iota(jnp.int32, (CHUNK, LANES), 1)

    def chunk_pair(i, _):
        work = []
        for c in range(2):
            n = 2 * i + c
            rows = pl.ds(pl.multiple_of(n * CHUNK, CHUNK), CHUNK)
            work.append((n, rows, _chunk_gates(col_ref, tri_ref, rows, range(GDN_HEADS), lane, rr, cc)))
        prods = []
        for n, rows, gates in work:
            for e, (_, beta, decay) in enumerate(gates):
                k = k_ref[rows, e * GDN_HEAD_DIM:(e + 1) * GDN_HEAD_DIM]
                kk = lax.dot_general((k * beta).astype(BF16), k.astype(BF16), nt, preferred_element_type=F32)
                prods.append((n, e, kk, decay))
        for n, e, kk, decay in prods:
            l_ref[0, e, n] = jnp.where(cc < rr, kk * decay, 0.0)
        return 0

    lax.fori_loop(0, n_chunks // 2, chunk_pair, 0)


def _unit_lower_inverse_body(l_ref, t_ref, lt_s, tt_s):
    for i in range(CHUNK):
        lt_s[i * CHUNK:(i + 1) * CHUNK, :] = l_ref[0, :, i, :].T
    sub = lax.broadcasted_iota(jnp.int32, (8, LANES), 0)
    for i in range(CHUNK):
        for kg in range(i // 8 + 1):
            acc = jnp.where(sub + 8 * kg == i, 1.0, 0.0)
            for j in range(8 * kg, i):
                acc = acc - lt_s[i * CHUNK + j:i * CHUNK + j + 1, :] * tt_s[j * CHUNK + 8 * kg:j * CHUNK + 8 * kg + 8, :]
            tt_s[i * CHUNK + 8 * kg:i * CHUNK + 8 * kg + 8, :] = acc
        for kg in range(i // 8 + 1, CHUNK // 8):
            tt_s[i * CHUNK + 8 * kg:i * CHUNK + 8 * kg + 8, :] = jnp.zeros((8, LANES), F32)
    for i in range(CHUNK):
        t_ref[0, :, i, :] = tt_s[i * CHUNK:(i + 1) * CHUNK, :].T


def _unit_lower_inverse(low):
    groups = low.shape[0]
    spec = pl.BlockSpec((1, LANES, CHUNK, CHUNK), lambda g: (g, 0, 0, 0))
    return pl.pallas_call(
        _unit_lower_inverse_body,
        grid=(groups,),
        in_specs=[spec], out_specs=spec,
        out_shape=jax.ShapeDtypeStruct(low.shape, F32),
        scratch_shapes=[pltpu.VMEM((CHUNK * CHUNK, LANES), F32), pltpu.VMEM((CHUNK * CHUNK, LANES), F32)],
        compiler_params=pltpu.CompilerParams(dimension_semantics=("arbitrary",), vmem_limit_bytes=VMEM_LIMIT),
        name="unit_lower_inverse",
    )(low)


def _gdn_prompt_body(q_ref, k_ref, v_ref, col_ref, tri_ref, t_ref, o_ref, s_ref, state_s,
                     u_s, w_s, qe_s, ke_s, at_s, eg_s):
    seg = pl.program_id(1)
    n_seqs = q_ref.shape[0]
    n_chunks = q_ref.shape[1] // CHUNK
    nt = (((1,), (1,)), ((), ()))
    rr = lax.broadcasted_iota(jnp.int32, (CHUNK, CHUNK), 0)
    cc = lax.broadcasted_iota(jnp.int32, (CHUNK, CHUNK), 1)
    lane = lax.broadcasted_iota(jnp.int32, (CHUNK, LANES), 1)

    @pl.when(seg == 0)
    def _():
        state_s[...] = jnp.zeros(state_s.shape, F32)

    heads = range(GDN_HEADS)
    units = [(b, e) for b in range(n_seqs) for e in heads]
    hl = [slice(e * GDN_HEAD_DIM, (e + 1) * GDN_HEAD_DIM) for e in heads]
    fdot = functools.partial(jnp.dot, preferred_element_type=F32)

    def prepare(n, _):
        rows = pl.ds(pl.multiple_of(n * CHUNK, CHUNK), CHUNK)
        gates = [_chunk_gates(col_ref.at[b], tri_ref, rows, heads, lane, rr, cc) for b in range(n_seqs)]
        pending = []
        for b, e in units:
            gc, beta, decay = gates[b][e]
            q = q_ref[b, rows, hl[e]] * (GDN_HEAD_DIM ** -0.5)
            k = k_ref[b, rows, hl[e]]
            v = v_ref[b, rows, hl[e]]
            egc = jnp.exp(gc)
            g_last = gc[CHUNK - 1:CHUNK, :]
            tinv = t_ref[b, e, n]
            t_hi = tinv.astype(BF16)
            t_lo = (tinv - t_hi.astype(F32)).astype(BF16)
            rhs = jnp.concatenate([v * beta, k * (beta * egc)], axis=1)
            r_hi = rhs.astype(BF16)
            r_lo = (rhs - r_hi.astype(F32)).astype(BF16)
            uw = fdot(jnp.concatenate([t_hi, t_lo, t_hi], axis=1), jnp.concatenate([r_hi, r_hi, r_lo], axis=0))
            attn = lax.dot_general(q.astype(BF16), k.astype(BF16), nt, preferred_element_type=F32)
            pending.append((b, e, uw, attn, decay))
            qe_s[b, rows, hl[e]] = (q * egc).astype(BF16)
            ke_s[b, e, n] = (k * jnp.exp(g_last - gc)).T.astype(BF16)
            eg_s[b, e, pl.ds(n, 1), :] = jnp.broadcast_to(jnp.exp(g_last), (1, LANES))
        for b, e, uw, attn, decay in pending:
            u_s[b, rows, hl[e]] = uw[:, :GDN_HEAD_DIM]
            w_s[b, rows, hl[e]] = uw[:, GDN_HEAD_DIM:].astype(BF16)
            at_s[b, e, rows, :] = (attn * decay).astype(BF16)
        return 0

    lax.fori_loop(0, n_chunks, prepare, 0)

    def scan(n, states):
        rows = pl.ds(pl.multiple_of(n * CHUNK, CHUNK), CHUNK)
        sbs = [st.astype(BF16) for st in states]
        ws = [fdot(w_s[b, rows, hl[e]], sbs[i]) for i, (b, e) in enumerate(units)]
        qs = [fdot(qe_s[b, rows, hl[e]], sbs[i]) for i, (b, e) in enumerate(units)]
        v_new = [(u_s[b, rows, hl[e]] - ws[i]).astype(BF16) for i, (b, e) in enumerate(units)]
        av = [fdot(at_s[b, e, rows, :], v_new[i]) for i, (b, e) in enumerate(units)]
        kv = [fdot(ke_s[b, e, n], v_new[i]) for i, (b, e) in enumerate(units)]
        for i, (b, e) in enumerate(units):
            o_ref[b, rows, hl[e]] = qs[i] + av[i]
        return tuple(states[i] * eg_s[b, e, pl.ds(n, 1), :] + kv[i] for i, (b, e) in enumerate(units))

    init = tuple(state_s[b, e] for b, e in units)
    final = lax.fori_loop(0, n_chunks, scan, init)
    for i, (b, e) in enumerate(units):
        state_s[b, e] = final[i]
        s_ref[b, e] = final[i]


GDN_SEGMENT = 512
GDN_SEQS_PER_STEP = 2


def _gdn_prompt(qkv, col, tri64, batch, seq):
    n = qkv.shape[0]
    n_chunks = seq // CHUNK
    n_seg = seq // GDN_SEGMENT
    seg_chunks = GDN_SEGMENT // CHUNK
    blk = lambda off: pl.BlockSpec((GDN_SEGMENT, GDN_WIDTH), lambda b, s: (b * n_seg + s, off))
    col_spec = pl.BlockSpec((GDN_SEGMENT, LANES), lambda b, s: (b * n_seg + s, 0))
    tri_spec = pl.BlockSpec((CHUNK, CHUNK), lambda b, s: (0, 0))
    mat_spec = pl.BlockSpec((1, GDN_HEADS, seg_chunks, CHUNK, CHUNK), lambda b, s: (b, 0, s, 0, 0))
    params = pltpu.CompilerParams(dimension_semantics=("arbitrary",) * 2, vmem_limit_bytes=VMEM_LIMIT)
    low = pl.pallas_call(
        _gdn_lower_body,
        grid=(batch, n_seg),
        in_specs=[blk(1), col_spec, tri_spec],
        out_specs=mat_spec,
        out_shape=jax.ShapeDtypeStruct((batch, GDN_HEADS, n_chunks, CHUNK, CHUNK), F32),
        compiler_params=params,
        name="gdn_lower",
    )(qkv, col, tri64)
    n_sys = batch * GDN_HEADS * n_chunks
    tinv = _unit_lower_inverse(low.reshape(n_sys // LANES, LANES, CHUNK, CHUNK)).reshape(low.shape)

    nb = GDN_SEQS_PER_STEP if batch % GDN_SEQS_PER_STEP == 0 else 1
    qkv3 = qkv.reshape(batch, seq, 3 * GDN_WIDTH)
    seq_blk = lambda width, off: pl.BlockSpec((nb, GDN_SEGMENT, width), lambda b, s: (b, s, off))
    state_spec = pl.BlockSpec((nb, GDN_HEADS, GDN_HEAD_DIM, GDN_HEAD_DIM), lambda b, s: (b, 0, 0, 0))
    o, s_fin = pl.pallas_call(
        _gdn_prompt_body,
        grid=(batch // nb, n_seg),
        in_specs=[seq_blk(GDN_WIDTH, 0), seq_blk(GDN_WIDTH, 1), seq_blk(GDN_WIDTH, 2), seq_blk(LANES, 0), tri_spec,
                  pl.BlockSpec((nb, GDN_HEADS, seg_chunks, CHUNK, CHUNK), lambda b, s: (b, 0, s, 0, 0))],
        out_specs=[seq_blk(GDN_WIDTH, 0), state_spec],
        out_shape=[jax.ShapeDtypeStruct((batch, seq, GDN_WIDTH), F32),
                   jax.ShapeDtypeStruct((batch, GDN_HEADS, GDN_HEAD_DIM, GDN_HEAD_DIM), F32)],
        scratch_shapes=[pltpu.VMEM((nb, GDN_HEADS, GDN_HEAD_DIM, GDN_HEAD_DIM), F32),
                        pltpu.VMEM((nb, GDN_SEGMENT, GDN_WIDTH), F32), pltpu.VMEM((nb, GDN_SEGMENT, GDN_WIDTH), BF16),
                        pltpu.VMEM((nb, GDN_SEGMENT, GDN_WIDTH), BF16),
                        pltpu.VMEM((nb, GDN_HEADS, seg_chunks, GDN_HEAD_DIM, CHUNK), BF16),
                        pltpu.VMEM((nb, GDN_HEADS, GDN_SEGMENT, CHUNK), BF16),
                        pltpu.VMEM((nb, GDN_HEADS, seg_chunks, LANES), F32)],
        compiler_params=params,
        name="gdn_prompt",
    )(qkv3, qkv3, qkv3, col.reshape(batch, seq, LANES), tri64, tinv)
    return o.reshape(n, GDN_WIDTH), s_fin


def _fox_prompt(q, col, kT, vT, cT, tq):
    n = q.shape[0]
    batch, _, seq = kT.shape
    nq = seq // tq
    pair = 2 * FOX_HEAD_DIM
    return pl.pallas_call(
        functools.partial(_fox_prompt_body, tq=tq),
        grid=(batch, FOX_HEADS // 2, nq),
        in_specs=[
            pl.BlockSpec((tq, pair), lambda b, p, i: (b * nq + i, p)),
            pl.BlockSpec((tq, LANES), lambda b, p, i: (b * nq + i, 0)),
            pl.BlockSpec((1, pair, seq), lambda b, p, i: (b, p, 0)),
            pl.BlockSpec((1, pair, seq), lambda b, p, i: (b, p, 0)),
            pl.BlockSpec((1, FOX_HEADS, seq), lambda b, p, i: (b, 0, 0)),
        ],
        out_specs=pl.BlockSpec((tq, pair), lambda b, p, i: (b * nq + i, p)),
        out_shape=jax.ShapeDtypeStruct((n, FOX_WIDTH), BF16),
        scratch_shapes=[pltpu.VMEM((2, FOX_HEAD_DIM + 16, seq), BF16), pltpu.VMEM((pair, seq), BF16)],
        compiler_params=pltpu.CompilerParams(dimension_semantics=("arbitrary",) * 3, vmem_limit_bytes=VMEM_LIMIT),
        name="fox_prompt",
    )(q, col, kT, vT, cT)


ROW_TILE = 512
FOX_Q_TILE = 512


def kernel(x_prompt, x_sample, cache_k, cache_v, cache_logf, state_gdn, state_conv, page_table,
           ffn1_norm_pre, ffn1_w_gate, ffn1_w_up, ffn1_w_down, ffn1_norm_post,
           mix_norm_pre, w_in, fox_forget_bias, gdn_conv_w, gdn_dt_bias, gdn_a_log, gdn_out_norm,
           w_branch_fox, w_branch_gdn, w_out, mix_norm_post,
           ffn2_norm_pre, ffn2_w_gate, ffn2_w_up, ffn2_w_down, ffn2_norm_post):
    batch, seq, d = x_prompt.shape
    db = x_sample.shape[0]
    assert x_sample.shape[1] == 1, "sample group carries one new token per sequence"
    depth = w_in.shape[0]
    xp = x_prompt.reshape(batch * seq, d)
    xs = x_sample.reshape(db, d)
    row = lambda v: v.reshape(1, -1).astype(F32)
    st_p, st_s = [], []
    for l in range(depth):
        ffn1 = _ffn_weights(ffn1_w_gate[l], ffn1_w_up[l], ffn1_w_down[l])
        ffn2 = _ffn_weights(ffn2_w_gate[l], ffn2_w_up[l], ffn2_w_down[l])
        w = _mixer_weights(w_in[l], fox_forget_bias[l], gdn_conv_w[l], gdn_dt_bias[l], gdn_a_log[l])
        out_w = (row(gdn_out_norm[l]), w_branch_fox[l], w_branch_gdn[l], w_out[l], row(mix_norm_post[l]))

        xp = _ffn(xp, row(ffn1_norm_pre[l]), *ffn1, row(ffn1_norm_post[l]), ROW_TILE)
        (q, kT, vT, lfT, cT, _, col, qkv, conv_p, z, ga, gb) = _proj_prompt(
            xp, row(mix_norm_pre[l]), w, batch, seq, ROW_TILE)
        o_a = _fox_prompt(q, col, kT, vT, cT, FOX_Q_TILE)
        o_g, s_p = _gdn_prompt(qkv, col, w["tri"][:CHUNK, :CHUNK], batch, seq)
        xp = _mix_out(xp, o_a, o_g, z, ga, gb, *out_w, ROW_TILE)
        xp = _ffn(xp, row(ffn2_norm_pre[l]), *ffn2, row(ffn2_norm_post[l]), ROW_TILE)
        to_heads = lambda t: t.reshape(batch, FOX_HEADS, FOX_HEAD_DIM, seq).transpose(0, 3, 1, 2)
        st_p.append((to_heads(kT), to_heads(vT), lfT.transpose(0, 2, 1), s_p, conv_p))

        xs = _ffn(xs, row(ffn1_norm_pre[l]), *ffn1, row(ffn1_norm_post[l]), db)
        buf = state_conv[l].transpose(1, 0, 2)
        bufT = state_conv[l][:, :, :2 * GDN_WIDTH].transpose(1, 2, 0)
        (qTs, kTs, vTs, small, smallT, u, qkT, vg, zs, gas, gbs) = _proj_sample(
            xs, row(mix_norm_pre[l]), w, buf, bufT)
        o_as = _fox_decode(page_table, qTs, kTs, vTs, smallT,
                           cache_k[l].transpose(0, 2, 3, 1), cache_v[l].transpose(0, 2, 3, 1),
                           cache_logf[l].transpose(0, 2, 1))
        o_gs, s_s = _gdn_decode(qkT, vg, smallT, state_gdn[l])
        xs = _mix_out(xs, o_as.astype(BF16), o_gs, zs, gas, gbs, *out_w, db)
        xs = _ffn(xs, row(ffn2_norm_pre[l]), *ffn2, row(ffn2_norm_post[l]), db)
        new_conv = jnp.concatenate([state_conv[l][:, 1:], u[:, None, :]], axis=1)
        st_s.append((kTs.T.reshape(db, 1, FOX_HEADS, FOX_HEAD_DIM), vTs.T.reshape(db, 1, FOX_HEADS, FOX_HEAD_DIM),
                     small[:, :FOX_HEADS].reshape(db, 1, FOX_HEADS), s_s, new_conv))

    stack = lambda states, i: jnp.stack([s[i] for s in states], axis=0)
    return (xp.reshape(batch, seq, d), xs.reshape(db, 1, d),
            *[stack(st_p, i) for i in range(5)], *[stack(st_s, i) for i in range(5)])
```

```python
import functools

import jax
import jax.numpy as jnp
from jax import lax
from jax.experimental import pallas as pl
from jax.experimental.pallas import tpu as pltpu

F32 = jnp.float32
BF16 = jnp.bfloat16
EPS = 1e-6
LOG2E = 1.4426950408889634
LANES = 128
VMEM_LIMIT = 56 * 1024 * 1024

FOX_HEADS = 8
FOX_HEAD_DIM = 64
FOX_WIDTH = FOX_HEADS * FOX_HEAD_DIM
GDN_HEADS = 4
GDN_HEAD_DIM = 128
GDN_WIDTH = GDN_HEADS * GDN_HEAD_DIM
CONV_WIDTH = 4
CONV_CH = 3 * GDN_WIDTH
CHUNK = 64
FF_CHUNK = 256


def _rms(x, g):
    return x * lax.rsqrt(jnp.mean(x * x, axis=-1, keepdims=True) + EPS) * g


def _sigmoid(x):
    return 1.0 / (1.0 + jnp.exp(-x))


def _silu(x):
    return x * _sigmoid(x)


def _const_spec(shape):
    n = len(shape)
    return pl.BlockSpec(shape, lambda *_: (0,) * n, pipeline_mode=pl.Buffered(1))


def _ffn_body(x_ref, gpre_ref, wg_ref, wu_ref, wd_ref, gpost_ref, o_ref):
    x = x_ref[...]
    h = _rms(x, gpre_ref[...]).astype(BF16)
    y = jnp.zeros(x.shape, F32)
    for j in range(wg_ref.shape[1] // FF_CHUNK):
        cols = slice(j * FF_CHUNK, (j + 1) * FF_CHUNK)
        g = jnp.dot(h, wg_ref[:, cols].astype(BF16), preferred_element_type=F32)
        u = jnp.dot(h, wu_ref[:, cols].astype(BF16), preferred_element_type=F32)
        a = (_silu(g) * u).astype(BF16)
        y = y + jnp.dot(a, wd_ref[cols, :].astype(BF16), preferred_element_type=F32)
    o_ref[...] = x + 0.5 * _rms(y, gpost_ref[...])


def _ffn(x, g_pre, wg, wu, wd, g_post, tm):
    n, d = x.shape
    row = pl.BlockSpec((tm, d), lambda i: (i, 0))
    return pl.pallas_call(
        _ffn_body,
        grid=(pl.cdiv(n, tm),),
        in_specs=[row, _const_spec((1, d)), _const_spec(wg.shape), _const_spec(wu.shape),
                  _const_spec(wd.shape), _const_spec((1, d))],
        out_specs=row,
        out_shape=jax.ShapeDtypeStruct((n, d), F32),
        compiler_params=pltpu.CompilerParams(dimension_semantics=("arbitrary",), vmem_limit_bytes=VMEM_LIMIT),
        name="ffn",
    )(x, g_pre, wg, wu, wd, g_post)


def _ffn_weights(w_gate, w_up, w_down):
    assert w_gate.shape[1] % FF_CHUNK == 0
    return w_gate, w_up, w_down


def _split3(x):
    hi = x.astype(BF16)
    r = x - hi.astype(F32)
    mid = r.astype(BF16)
    lo = (r - mid.astype(F32)).astype(BF16)
    return hi, mid, lo


def _dot3(tri_bf16, x):
    hi, mid, lo = _split3(x)
    d = functools.partial(jnp.dot, tri_bf16, preferred_element_type=F32)
    return d(hi) + d(mid) + d(lo)


def _softplus(x):
    return jnp.maximum(x, 0.0) + jnp.log(1.0 + jnp.exp(-jnp.abs(x)))


def _small_cols(small, bias_ref, aneg_ref):
    lane = lax.broadcasted_iota(jnp.int32, small.shape, 1)
    pre = small + bias_ref[...]
    sp = _softplus(pre)
    logf = pre - sp
    g = aneg_ref[...] * sp
    beta = _sigmoid(small)
    gb = jnp.where(lane < FOX_HEADS + GDN_HEADS, g, beta)
    return logf, gb


def _l2n_heads(x):
    outs = []
    for h in range(x.shape[1] // GDN_HEAD_DIM):
        xh = x[:, h * GDN_HEAD_DIM:(h + 1) * GDN_HEAD_DIM]
        outs.append(xh * lax.rsqrt(jnp.sum(xh * xh, axis=-1, keepdims=True) + EPS))
    return jnp.concatenate(outs, axis=-1)


def _proj_prompt_body(x_ref, gpre_ref, wq_ref, wkT_ref, wvT_ref, ws_ref, wc_ref, wz_ref, wga_ref, wgb_ref,
                      bias_ref, aneg_ref, convw_ref, tri_ref,
                      q_ref, kT_ref, vT_ref, lfT_ref, cT_ref, gbT_ref, col_ref, qkv_ref, cst_ref,
                      z_ref, ga_ref, gb_ref, ubuf, ccarry, *, tiles_per_seq):
    i = pl.program_id(0)
    tm = x_ref.shape[0]

    @pl.when(i % tiles_per_seq == 0)
    def _():
        ubuf[0:8, :] = jnp.zeros((8, ubuf.shape[1]), F32)
        ccarry[...] = jnp.zeros(ccarry.shape, F32)

    h = _rms(x_ref[...], gpre_ref[...]).astype(BF16)
    dot = functools.partial(jnp.dot, h, preferred_element_type=F32)
    nt = (((1,), (1,)), ((), ()))

    ubuf[8:8 + tm, :] = dot(wc_ref[...])
    small = dot(ws_ref[...])
    q_ref[...] = (dot(wq_ref[...]) * (LOG2E * FOX_HEAD_DIM ** -0.5)).astype(BF16)
    kT_ref[0] = lax.dot_general(wkT_ref[...], h, nt, preferred_element_type=F32)
    vT_ref[0] = lax.dot_general(wvT_ref[...], h, nt, preferred_element_type=F32)
    z_ref[...] = dot(wz_ref[...]).astype(BF16)
    ga_ref[...] = _sigmoid(dot(wga_ref[...])).astype(BF16)
    gb_ref[...] = _sigmoid(dot(wgb_ref[...])).astype(BF16)

    logf, gb = _small_cols(small, bias_ref, aneg_ref)
    carry = ccarry[...]
    blocks = []
    for r in range(tm // LANES):
        cb = _dot3(tri_ref[...], logf[r * LANES:(r + 1) * LANES, :]) + carry
        carry = cb[LANES - 1:LANES, :]
        blocks.append(cb)
    ccarry[...] = carry
    c = jnp.concatenate(blocks, axis=0)
    lane = lax.broadcasted_iota(jnp.int32, c.shape, 1)
    col = jnp.where(lane < FOX_HEADS, c, gb)
    col_ref[...] = col
    colT = col.T
    lfT_ref[0] = logf.T[0:FOX_HEADS, :]
    cT_ref[0] = colT[0:FOX_HEADS, :]
    gbT_ref[0] = colT[FOX_HEADS:2 * FOX_HEADS, :]

    base = 8 - (CONV_WIDTH - 1)
    conv = ubuf[base:base + tm, :] * convw_ref[0:1, :]
    for k in range(1, CONV_WIDTH):
        conv = conv + ubuf[base + k:base + k + tm, :] * convw_ref[k:k + 1, :]
    cst_ref[0] = ubuf[tm + base:tm + 8, :]
    ubuf[0:8, :] = ubuf[tm:tm + 8, :]
    conv = _silu(conv)
    qkv_ref[:, 0:2 * GDN_WIDTH] = _l2n_heads(conv[:, 0:2 * GDN_WIDTH])
    qkv_ref[:, 2 * GDN_WIDTH:] = conv[:, 2 * GDN_WIDTH:]


def _proj_prompt(x, g_pre, w, batch, seq, tm):
    n, d = x.shape
    tps = seq // tm
    row = lambda width: pl.BlockSpec((tm, width), lambda i: (i, 0))
    seqT = lambda rows: pl.BlockSpec((1, rows, tm), lambda i: (i // tps, 0, i % tps))
    consts = [g_pre, w["q"], w["kT"], w["vT"], w["small"], w["conv"], w["z"], w["ga"], w["gb"],
              w["bias"], w["aneg"], w["convw"], w["tri"]]
    out_shape = [
        jax.ShapeDtypeStruct((n, FOX_WIDTH), BF16),
        jax.ShapeDtypeStruct((batch, FOX_WIDTH, seq), F32),
        jax.ShapeDtypeStruct((batch, FOX_WIDTH, seq), F32),
        jax.ShapeDtypeStruct((batch, FOX_HEADS, seq), F32),
        jax.ShapeDtypeStruct((batch, FOX_HEADS, seq), F32),
        jax.ShapeDtypeStruct((batch, 2 * GDN_HEADS, seq), F32),
        jax.ShapeDtypeStruct((n, LANES), F32),
        jax.ShapeDtypeStruct((n, CONV_CH), F32),
        jax.ShapeDtypeStruct((batch, CONV_WIDTH - 1, CONV_CH), F32),
        jax.ShapeDtypeStruct((n, GDN_WIDTH), BF16),
        jax.ShapeDtypeStruct((n, d), BF16),
        jax.ShapeDtypeStruct((n, d), BF16),
    ]
    out_specs = [row(FOX_WIDTH), seqT(FOX_WIDTH), seqT(FOX_WIDTH), seqT(FOX_HEADS), seqT(FOX_HEADS),
                 seqT(2 * GDN_HEADS), row(LANES), row(CONV_CH),
                 pl.BlockSpec((1, CONV_WIDTH - 1, CONV_CH), lambda i: (i // tps, 0, 0)),
                 row(GDN_WIDTH), row(d), row(d)]
    return pl.pallas_call(
        functools.partial(_proj_prompt_body, tiles_per_seq=tps),
        grid=(n // tm,),
        in_specs=[row(d)] + [_const_spec(a.shape) for a in consts],
        out_specs=out_specs,
        out_shape=out_shape,
        scratch_shapes=[pltpu.VMEM((tm + 8, CONV_CH), F32), pltpu.VMEM((1, LANES), F32)],
        compiler_params=pltpu.CompilerParams(dimension_semantics=("arbitrary",), vmem_limit_bytes=VMEM_LIMIT),
        name="proj_prompt",
    )(x, *consts)


def _mixer_weights(w_in, b_f, conv_w, dt_bias, a_log):
    sizes = (FOX_WIDTH, FOX_WIDTH, FOX_WIDTH, FOX_HEADS, CONV_CH, GDN_HEADS, GDN_HEADS, GDN_WIDTH)
    d = w_in.shape[0]
    offs = [0]
    for s in sizes:
        offs.append(offs[-1] + s)
    part = lambda k: w_in[:, offs[k]:offs[k + 1]]
    ga0 = offs[-1]
    n_small = FOX_HEADS + 2 * GDN_HEADS
    small = jnp.concatenate([part(3), part(5), part(6), jnp.zeros((d, LANES - n_small), w_in.dtype)], axis=1)
    pad = lambda v, lo: jnp.zeros((1, LANES), F32).at[0, lo:lo + v.shape[0]].set(v.astype(F32))
    r = lax.broadcasted_iota(jnp.int32, (LANES, LANES), 0)
    c = lax.broadcasted_iota(jnp.int32, (LANES, LANES), 1)
    return {
        "q": part(0).astype(BF16), "kT": part(1).T.astype(BF16), "vT": part(2).T.astype(BF16),
        "small": small.astype(BF16), "conv": part(4).astype(BF16), "z": part(7).astype(BF16),
        "ga": w_in[:, ga0:ga0 + d].astype(BF16), "gb": w_in[:, ga0 + d:ga0 + 2 * d].astype(BF16),
        "bias": pad(b_f, 0) + pad(dt_bias, FOX_HEADS),
        "aneg": pad(-jnp.exp(a_log.astype(F32)), FOX_HEADS),
        "convw": conv_w.astype(F32),
        "tri": (c <= r).astype(BF16),
    }


NEG_BIG = -1e30


def _split3_f32(x):
    hi, mid, lo = _split3(x)
    return hi.astype(F32), mid.astype(F32), lo.astype(F32)


def _fox_prompt_body(q_ref, col_ref, kT_ref, vT_ref, cT_ref, o_ref, ka_ref, vb_ref, *, tq):
    p = pl.program_id(1)
    qi = pl.program_id(2)
    hd = FOX_HEAD_DIM
    n_aug = 16

    @pl.when(qi == 0)
    def _():
        sub = lax.broadcasted_iota(jnp.int32, (n_aug, ka_ref.shape[2]), 0)
        for e in range(2):
            vb_ref[e, 0:hd, :] = vT_ref[0, e * hd:(e + 1) * hd, :].astype(BF16)
            vb_ref[e, hd:hd + n_aug, :] = jnp.where(sub == 0, 1.0, 0.0).astype(BF16)
            ka_ref[e, 0:hd, :] = kT_ref[0, e * hd:(e + 1) * hd, :].astype(BF16)
            hi, mid, lo = _split3_f32(cT_ref[0, pl.ds(2 * p + e, 1), :] * LOG2E)
            aug = jnp.where(sub < 3, 1.0, jnp.where(sub == 3, -hi, jnp.where(sub == 4, -mid,
                            jnp.where(sub == 5, -lo, 0.0))))
            ka_ref[e, hd:hd + n_aug, :] = aug.astype(BF16)

    nt = (((1,), (1,)), ((), ()))
    col = col_ref[...]
    lane = lax.broadcasted_iota(jnp.int32, col.shape, 1)
    q_pair = q_ref[...].astype(F32)
    qs = []
    for e in range(2):
        cq = jnp.sum(jnp.where(lane == 2 * p + e, col, 0.0), axis=-1, keepdims=True) * LOG2E
        hi, mid, lo = _split3_f32(cq)
        extra = jnp.where(lane == hd, hi, jnp.where(lane == hd + 1, mid, jnp.where(lane == hd + 2, lo,
                          jnp.where(lane < hd + 6, 1.0, 0.0))))
        qh = q_pair if e == 0 else pltpu.roll(q_pair, hd, axis=1)
        qs.append(jnp.where(lane < hd, qh, extra)[:, 0:hd + n_aug].astype(BF16))

    def block(kb, carry):
        ks = pl.ds(pl.multiple_of(kb * tq, tq), tq)
        raw = [jnp.dot(qs[e], ka_ref[e, :, ks], preferred_element_type=F32) for e in range(2)]
        stats = []
        for e in range(2):
            m, _ = carry[e]
            s = raw[e]
            m_new = jnp.maximum(m, jnp.max(s, axis=-1, keepdims=True))
            stats.append((m_new, jnp.exp2(m - m_new), jnp.exp2(s - m_new).astype(BF16)))
        pvs = [lax.dot_general(stats[e][2], vb_ref[e, :, ks], nt, preferred_element_type=F32) for e in range(2)]
        return tuple((stats[e][0], stats[e][1] * carry[e][1] + pvs[e]) for e in range(2))

    def diagonal_block(carry):
        k0 = pl.multiple_of(qi * tq, tq)
        half = tq // 2
        parts = [(e, r0, width) for e in range(2) for (r0, width) in ((0, half), (half, tq))]
        raw = [jnp.dot(qs[e][r0:r0 + half, :], ka_ref[e, :, pl.ds(k0, width)], preferred_element_type=F32)
               for e, r0, width in parts]
        stats = []
        for i, (e, r0, width) in enumerate(parts):
            m = carry[e][0][r0:r0 + half, :]
            r = lax.broadcasted_iota(jnp.int32, raw[i].shape, 0) + r0
            c = lax.broadcasted_iota(jnp.int32, raw[i].shape, 1)
            s = jnp.where(c <= r, raw[i], NEG_BIG)
            m_new = jnp.maximum(m, jnp.max(s, axis=-1, keepdims=True))
            stats.append((m_new, jnp.exp2(m - m_new), jnp.exp2(s - m_new).astype(BF16)))
        pvs = [lax.dot_general(stats[i][2], vb_ref[e, :, pl.ds(k0, width)], nt, preferred_element_type=F32)
               for i, (e, r0, width) in enumerate(parts)]
        accs = [stats[i][1] * carry[e][1][r0:r0 + half, :] + pvs[i] for i, (e, r0, width) in enumerate(parts)]
        return tuple(jnp.concatenate(accs[2 * e:2 * e + 2], axis=0) for e in range(2))

    init = tuple((jnp.full((tq, 1), NEG_BIG, F32), jnp.zeros((tq, hd + n_aug), F32)) for _ in range(2))
    carry = lax.fori_loop(0, qi, block, init)
    accs = diagonal_block(carry)
    o_ref[...] = jnp.concatenate([acc[:, 0:hd] / acc[:, hd:hd + 1] for acc in accs], axis=-1).astype(o_ref.dtype)


def _proj_sample_body(x_ref, gpre_ref, wqT_ref, wkT_ref, wvT_ref, ws_ref, wc_ref, wcT_ref, wz_ref, wga_ref, wgb_ref,
                      bias_ref, aneg_ref, convw_ref, convwT_ref, buf_ref, bufT_ref,
                      qT_ref, kT_ref, vT_ref, small_ref, smallT_ref, u_ref, qkT_ref, vg_ref, z_ref, ga_ref, gb_ref):
    h = _rms(x_ref[...], gpre_ref[...]).astype(BF16)
    dot = functools.partial(jnp.dot, h, preferred_element_type=F32)
    dot_t = lambda w: lax.dot_general(w, h, (((1,), (1,)), ((), ())), preferred_element_type=F32)
    qT_ref[...] = dot_t(wqT_ref[...]) * (FOX_HEAD_DIM ** -0.5)
    kT_ref[...] = dot_t(wkT_ref[...])
    vT_ref[...] = dot_t(wvT_ref[...])
    z_ref[...] = dot(wz_ref[...]).astype(BF16)
    ga_ref[...] = _sigmoid(dot(wga_ref[...])).astype(BF16)
    gb_ref[...] = _sigmoid(dot(wgb_ref[...])).astype(BF16)

    logf, gb = _small_cols(dot(ws_ref[...]), bias_ref, aneg_ref)
    lane = lax.broadcasted_iota(jnp.int32, logf.shape, 1)
    small = jnp.where(lane < FOX_HEADS, logf, gb)
    small_ref[...] = small
    pad = jnp.concatenate([small, jnp.zeros((LANES - small.shape[0], LANES), F32)], axis=0)
    smallT_ref[...] = pad.T[:, 0:small.shape[0]]

    u = dot(wc_ref[...])
    u_ref[...] = u
    vs = slice(2 * GDN_WIDTH, CONV_CH)
    conv_v = u[:, vs] * convw_ref[CONV_WIDTH - 1:CONV_WIDTH, vs]
    conv_qk = dot_t(wcT_ref[...]) * convwT_ref[:, CONV_WIDTH - 1:CONV_WIDTH]
    for k in range(CONV_WIDTH - 1):
        conv_v = conv_v + buf_ref[k][:, vs] * convw_ref[k:k + 1, vs]
        conv_qk = conv_qk + bufT_ref[k] * convwT_ref[:, k:k + 1]
    vg_ref[...] = _silu(conv_v)
    conv_qk = _silu(conv_qk)
    for hd in range(2 * GDN_HEADS):
        rs = slice(hd * GDN_HEAD_DIM, (hd + 1) * GDN_HEAD_DIM)
        xh = conv_qk[rs, :]
        qkT_ref[rs, :] = xh * lax.rsqrt(jnp.sum(xh * xh, axis=0, keepdims=True) + EPS)


def _proj_sample(x, g_pre, w, buf, bufT):
    db, d = x.shape
    ins = [x, g_pre, w["q"].T, w["kT"], w["vT"], w["small"], w["conv"], w["conv"][:, :2 * GDN_WIDTH].T, w["z"],
           w["ga"], w["gb"], w["bias"], w["aneg"], w["convw"], w["convw"][:, :2 * GDN_WIDTH].T, buf, bufT]
    out_shape = [
        jax.ShapeDtypeStruct((FOX_WIDTH, db), F32), jax.ShapeDtypeStruct((FOX_WIDTH, db), F32),
        jax.ShapeDtypeStruct((FOX_WIDTH, db), F32),
        jax.ShapeDtypeStruct((db, LANES), F32), jax.ShapeDtypeStruct((LANES, db), F32),
        jax.ShapeDtypeStruct((db, CONV_CH), F32), jax.ShapeDtypeStruct((2 * GDN_WIDTH, db), F32),
        jax.ShapeDtypeStruct((db, GDN_WIDTH), F32), jax.ShapeDtypeStruct((db, GDN_WIDTH), BF16),
        jax.ShapeDtypeStruct((db, d), BF16), jax.ShapeDtypeStruct((db, d), BF16),
    ]
    return pl.pallas_call(
        _proj_sample_body, out_shape=out_shape,
        compiler_params=pltpu.CompilerParams(vmem_limit_bytes=VMEM_LIMIT),
        name="proj_sample",
    )(*ins)


def _take_lane(x, idx):
    lane = lax.broadcasted_iota(jnp.int32, x.shape, 1)
    return jnp.sum(jnp.where(lane == idx, x, 0.0), axis=-1, keepdims=True)


PAGES_PER_STEP = 8
DECODE_RING = 4


def _fox_decode_body(pt_ref, qT_ref, kTn_ref, vTn_ref, smallT_ref, sfxw_ref, ck_hbm, cv_hbm, clf_hbm, o_ref,
                     kbuf, vbuf, lfbuf, sem, qb_s, acc_s, m_s, l_s, sfx_s, al_s, p_s, *, n_pages):
    g_pages = PAGES_PER_STEP
    n_steps = n_pages // g_pages
    page = kbuf.shape[-1]
    b = pl.program_id(0)
    n_seq = pl.num_programs(0)
    hd = FOX_HEAD_DIM

    def group_copies(seq, j, slot):
        base = seq * n_pages + (n_steps - 1 - j) * g_pages
        out = []
        for g in range(g_pages):
            pg = pt_ref[base + (g_pages - 1 - g)]
            out.append(pltpu.make_async_copy(ck_hbm.at[pg], kbuf.at[slot, g], sem.at[slot]))
            out.append(pltpu.make_async_copy(cv_hbm.at[pg], vbuf.at[slot, g], sem.at[slot]))
            out.append(pltpu.make_async_copy(clf_hbm.at[pg], lfbuf.at[slot, g], sem.at[slot]))
        return out

    def per_head_dot(a, bm):
        return jnp.concatenate(
            [jnp.sum(a[h * hd:(h + 1) * hd, :] * bm[h * hd:(h + 1) * hd, :], axis=0, keepdims=True)
             for h in range(FOX_HEADS)], axis=0)

    ring = kbuf.shape[0]

    @pl.when(b == 0)
    def _():
        for t in range(ring - 1):
            for c in group_copies(t // n_steps, t % n_steps, t):
                c.start()

    qb = jnp.broadcast_to(_take_lane(qT_ref[...], b), (FOX_WIDTH, page))
    kn = jnp.broadcast_to(_take_lane(kTn_ref[...], b), (FOX_WIDTH, page))
    vn = jnp.broadcast_to(_take_lane(vTn_ref[...], b), (FOX_WIDTH, page))
    qb_s[...] = qb
    m_s[...] = per_head_dot(qb, kn)
    l_s[...] = jnp.ones(l_s.shape, F32)
    lane_w = lax.broadcasted_iota(jnp.int32, (FOX_WIDTH, page), 1)
    acc_s[...] = jnp.where(lane_w == 0, vn, 0.0)
    sfx_s[...] = jnp.broadcast_to(_take_lane(smallT_ref[0:FOX_HEADS, :], b), sfx_s.shape)

    def step(j, _):
        t = b * n_steps + j
        slot = t % ring
        nxt = t + ring - 1

        @pl.when(nxt < n_seq * n_steps)
        def _():
            for c in group_copies(nxt // n_steps, nxt % n_steps, nxt % ring):
                c.start()

        for c in group_copies(b, j, slot):
            c.wait()

        lfs = lfbuf[slot].reshape(g_pages * FOX_HEADS, page)
        hi, mid, lo = _split3(lfs)
        wdot = lambda a: jnp.dot(a, sfxw_ref[...], preferred_element_type=F32)
        red = wdot(hi) + wdot(mid) + wdot(lo)
        carry = sfx_s[...]
        scores = []
        for g in range(g_pages):
            hs = slice(g * FOX_HEADS, (g + 1) * FOX_HEADS)
            bias = carry + (red[hs, 0:page] - lfs[hs, :])
            carry = carry + red[hs, page:2 * page]
            kt = kbuf[slot, g].reshape(FOX_WIDTH, page)
            scores.append(per_head_dot(kt, qb_s[...]) + bias)
        sfx_s[...] = carry

        m_old = m_s[...]
        smax = scores[0]
        for s in scores[1:]:
            smax = jnp.maximum(smax, s)
        m_new = jnp.maximum(m_old, jnp.max(smax, axis=-1, keepdims=True))
        al_s[...] = jnp.exp(m_old - m_new)
        psum = None
        for g, s in enumerate(scores):
            p = jnp.exp(s - m_new)
            p_s[g] = p
            psum = p if psum is None else psum + p
        m_s[...] = m_new
        l_s[...] = al_s[...] * l_s[...] + jnp.sum(psum, axis=-1, keepdims=True)
        for h in range(FOX_HEADS):
            rs = slice(h * hd, (h + 1) * hd)
            acc = acc_s[rs, :] * al_s[h:h + 1, :]
            for g in range(g_pages):
                acc = acc + p_s[g, h:h + 1, :] * vbuf[slot, g, h]
            acc_s[rs, :] = acc
        return 0

    lax.fori_loop(0, n_steps, step, 0)

    inv_l = 1.0 / l_s[...]
    for h in range(FOX_HEADS):
        rs = slice(h * hd, (h + 1) * hd)
        o_ref[0, rs, :] = jnp.sum(acc_s[rs, :], axis=-1, keepdims=True) * inv_l[h:h + 1, 0:1]


def _fox_decode(page_table, qT, kTn, vTn, smallT, cache_kT, cache_vT, cache_lfT):
    db, n_pages = page_table.shape
    page = cache_kT.shape[-1]
    g_pages = PAGES_PER_STEP
    ring = DECODE_RING
    assert n_pages % g_pages == 0 and db * (n_pages // g_pages) >= ring - 1
    const = lambda shape: pl.BlockSpec(shape, lambda b, pt: (0,) * len(shape))
    hbm = pl.BlockSpec(memory_space=pl.ANY)
    ii = lax.broadcasted_iota(jnp.int32, (page, 2 * page), 0)
    jj = lax.broadcasted_iota(jnp.int32, (page, 2 * page), 1)
    sfxw = ((ii >= jj) | (jj >= page)).astype(BF16)
    in_specs = [const(qT.shape), const(kTn.shape), const(vTn.shape), const(smallT.shape), const(sfxw.shape),
                hbm, hbm, hbm]
    out = pl.pallas_call(
        functools.partial(_fox_decode_body, n_pages=n_pages),
        grid_spec=pltpu.PrefetchScalarGridSpec(
            num_scalar_prefetch=1, grid=(db,), in_specs=in_specs,
            out_specs=pl.BlockSpec((1, FOX_WIDTH, 1), lambda b, pt: (b, 0, 0)),
            scratch_shapes=[pltpu.VMEM((ring, g_pages, FOX_HEADS, FOX_HEAD_DIM, page), F32),
                            pltpu.VMEM((ring, g_pages, FOX_HEADS, FOX_HEAD_DIM, page), F32),
                            pltpu.VMEM((ring, g_pages, FOX_HEADS, page), F32),
                            pltpu.SemaphoreType.DMA((ring,)),
                            pltpu.VMEM((FOX_WIDTH, page), F32), pltpu.VMEM((FOX_WIDTH, page), F32),
                            pltpu.VMEM((FOX_HEADS, page), F32), pltpu.VMEM((FOX_HEADS, page), F32),
                            pltpu.VMEM((FOX_HEADS, page), F32), pltpu.VMEM((FOX_HEADS, page), F32),
                            pltpu.VMEM((g_pages, FOX_HEADS, page), F32)]),
        out_shape=jax.ShapeDtypeStruct((db, FOX_WIDTH, 1), F32),
        compiler_params=pltpu.CompilerParams(dimension_semantics=("arbitrary",), vmem_limit_bytes=VMEM_LIMIT),
        name="fox_decode",
    )(page_table.reshape(-1), qT, kTn, vTn, smallT, sfxw, cache_kT, cache_vT, cache_lfT)
    return out.reshape(db, FOX_WIDTH)


def _gdn_decode_body(qkT_ref, vg_ref, smallT_ref, s_ref, o_ref, snew_ref):
    b = pl.program_id(0)
    qk = _take_lane(qkT_ref[...], b)
    gbeta = _take_lane(smallT_ref[FOX_HEADS:FOX_HEADS + 2 * GDN_HEADS, :], b)
    for h in range(GDN_HEADS):
        rs = slice(h * GDN_HEAD_DIM, (h + 1) * GDN_HEAD_DIM)
        q = qk[rs, :] * (GDN_HEAD_DIM ** -0.5)
        k = qk[GDN_WIDTH + h * GDN_HEAD_DIM:GDN_WIDTH + (h + 1) * GDN_HEAD_DIM, :]
        v = vg_ref[0, :, rs]
        st = s_ref[0, h] * jnp.exp(gbeta[h:h + 1, :])
        delta = (v - jnp.sum(k * st, axis=0, keepdims=True)) * gbeta[GDN_HEADS + h:GDN_HEADS + h + 1, :]
        st = st + k * delta
        snew_ref[0, h] = st
        o_ref[0, :, rs] = jnp.sum(q * st, axis=0, keepdims=True)


def _gdn_decode(qkT, vg, smallT, state):
    db = vg.shape[0]
    const = lambda shape: pl.BlockSpec(shape, lambda b: (0,) * len(shape))
    st_spec = pl.BlockSpec((1, GDN_HEADS, GDN_HEAD_DIM, GDN_HEAD_DIM), lambda b: (b, 0, 0, 0))
    row_spec = pl.BlockSpec((1, 1, GDN_WIDTH), lambda b: (b, 0, 0))
    o, s_new = pl.pallas_call(
        _gdn_decode_body,
        grid=(db,),
        in_specs=[const(qkT.shape), row_spec, const(smallT.shape), st_spec],
        out_specs=[row_spec, st_spec],
        out_shape=[jax.ShapeDtypeStruct((db, 1, GDN_WIDTH), F32), jax.ShapeDtypeStruct(state.shape, F32)],
        compiler_params=pltpu.CompilerParams(dimension_semantics=("arbitrary",)),
        name="gdn_decode",
    )(qkT, vg.reshape(db, 1, GDN_WIDTH), smallT, state)
    return o.reshape(db, GDN_WIDTH), s_new


def _mix_out_body(x_ref, oa_ref, og_ref, z_ref, ga_ref, gb_ref, ggdn_ref, wbf_ref, wbg_ref, wout_ref, gpost_ref,
                  o_ref):
    og = og_ref[...]
    z = z_ref[...].astype(F32)
    parts = []
    for h in range(GDN_HEADS):
        ls = slice(h * GDN_HEAD_DIM, (h + 1) * GDN_HEAD_DIM)
        parts.append(_rms(og[:, ls], ggdn_ref[...]) * _silu(z[:, ls]))
    ogn = jnp.concatenate(parts, axis=-1).astype(BF16)
    y_a = jnp.dot(oa_ref[...], wbf_ref[...].astype(BF16), preferred_element_type=F32)
    y_g = jnp.dot(ogn, wbg_ref[...].astype(BF16), preferred_element_type=F32)
    y = (ga_ref[...].astype(F32) * y_a + gb_ref[...].astype(F32) * y_g).astype(BF16)
    y = jnp.dot(y, wout_ref[...].astype(BF16), preferred_element_type=F32)
    o_ref[...] = x_ref[...] + _rms(y, gpost_ref[...])


def _mix_out(x, o_a, o_g, z, ga, gb, g_gdn, wbf, wbg, wout, g_post, tm):
    n, d = x.shape
    row = lambda width: pl.BlockSpec((tm, width), lambda i: (i, 0))
    consts = [g_gdn, wbf, wbg, wout, g_post]
    return pl.pallas_call(
        _mix_out_body,
        grid=(pl.cdiv(n, tm),),
        in_specs=[row(d), row(FOX_WIDTH), row(GDN_WIDTH), row(GDN_WIDTH), row(d), row(d)]
        + [_const_spec(a.shape) for a in consts],
        out_specs=row(d),
        out_shape=jax.ShapeDtypeStruct((n, d), F32),
        compiler_params=pltpu.CompilerParams(dimension_semantics=("arbitrary",), vmem_limit_bytes=VMEM_LIMIT),
        name="mix_out",
    )(x, o_a, o_g, z, ga, gb, *consts)


def _chunk_gates(col_ref, tri_ref, rows, heads, lane, rr, cc):
    colc = col_ref[rows, :]
    cs = _dot3(tri_ref[...], colc)
    keep = cc <= rr
    out = []
    for head in heads:
        gc = jnp.sum(jnp.where(lane == FOX_HEADS + head, cs, 0.0), axis=-1, keepdims=True)
        beta = jnp.sum(jnp.where(lane == FOX_HEADS + GDN_HEADS + head, colc, 0.0), axis=-1, keepdims=True)
        cmat = jnp.broadcast_to(gc, (CHUNK, CHUNK))
        decay = jnp.where(keep, jnp.exp(jnp.where(keep, cmat - cmat.T, 0.0)), 0.0)
        out.append((gc, beta, decay))
    return out


def _gdn_lower_body(k_ref, col_ref, tri_ref, l_ref):
    n_chunks = k_ref.shape[0] // CHUNK
    nt = (((1,), (1,)), ((), ()))
    rr = lax.broadcasted_iota(jnp.int32, (CHUNK, CHUNK), 0)
    cc = lax.broadcasted_iota(jnp.int32, (CHUNK, CHUNK), 1)
    lane = lax.broadcasted_iota(jnp.int32, (CHUNK, LANES), 1)

    def chunk_pair(i, _):
        work = []
        for c in range(2):
            n = 2 * i + c
            rows = pl.ds(pl.multiple_of(n * CHUNK, CHUNK), CHUNK)
            work.append((n, rows, _chunk_gates(col_ref, tri_ref, rows, range(GDN_HEADS), lane, rr, cc)))
        prods = []
        for n, rows, gates in work:
            for e, (_, beta, decay) in enumerate(gates):
                k = k_ref[rows, e * GDN_HEAD_DIM:(e + 1) * GDN_HEAD_DIM]
                kk = lax.dot_general((k * beta).astype(BF16), k.astype(BF16), nt, preferred_element_type=F32)
                prods.append((n, e, kk, decay))
        for n, e, kk, decay in prods:
            l_ref[0, e, n] = jnp.where(cc < rr, kk * decay, 0.0)
        return 0

    lax.fori_loop(0, n_chunks // 2, chunk_pair, 0)


def _unit_lower_inverse_body(l_ref, t_ref, lt_s, tt_s):
    for i in range(CHUNK):
        lt_s[i * CHUNK:(i + 1) * CHUNK, :] = l_ref[0, :, i, :].T
    sub = lax.broadcasted_iota(jnp.int32, (8, LANES), 0)
    for i in range(CHUNK):
        for kg in range(i // 8 + 1):
            acc = jnp.where(sub + 8 * kg == i, 1.0, 0.0)
            for j in range(8 * kg, i):
                acc = acc - lt_s[i * CHUNK + j:i * CHUNK + j + 1, :] * tt_s[j * CHUNK + 8 * kg:j * CHUNK + 8 * kg + 8, :]
            tt_s[i * CHUNK + 8 * kg:i * CHUNK + 8 * kg + 8, :] = acc
        for kg in range(i // 8 + 1, CHUNK // 8):
            tt_s[i * CHUNK + 8 * kg:i * CHUNK + 8 * kg + 8, :] = jnp.zeros((8, LANES), F32)
    for i in range(CHUNK):
        t_ref[0, :, i, :] = tt_s[i * CHUNK:(i + 1) * CHUNK, :].T


def _unit_lower_inverse(low):
    groups = low.shape[0]
    spec = pl.BlockSpec((1, LANES, CHUNK, CHUNK), lambda g: (g, 0, 0, 0))
    return pl.pallas_call(
        _unit_lower_inverse_body,
        grid=(groups,),
        in_specs=[spec], out_specs=spec,
        out_shape=jax.ShapeDtypeStruct(low.shape, F32),
        scratch_shapes=[pltpu.VMEM((CHUNK * CHUNK, LANES), F32), pltpu.VMEM((CHUNK * CHUNK, LANES), F32)],
        compiler_params=pltpu.CompilerParams(dimension_semantics=("arbitrary",), vmem_limit_bytes=VMEM_LIMIT),
        name="unit_lower_inverse",
    )(low)


def _gdn_prompt_body(q_ref, k_ref, v_ref, col_ref, tri_ref, t_ref, o_ref, s_ref, state_s,
                     u_s, w_s, qe_s, ke_s, at_s, eg_s):
    seg = pl.program_id(1)
    n_seqs = q_ref.shape[0]
    n_chunks = q_ref.shape[1] // CHUNK
    nt = (((1,), (1,)), ((), ()))
    rr = lax.broadcasted_iota(jnp.int32, (CHUNK, CHUNK), 0)
    cc = lax.broadcasted_iota(jnp.int32, (CHUNK, CHUNK), 1)
    lane = lax.broadcasted_iota(jnp.int32, (CHUNK, LANES), 1)

    @pl.when(seg == 0)
    def _():
        state_s[...] = jnp.zeros(state_s.shape, F32)

    heads = range(GDN_HEADS)
    units = [(b, e) for b in range(n_seqs) for e in heads]
    hl = [slice(e * GDN_HEAD_DIM, (e + 1) * GDN_HEAD_DIM) for e in heads]
    fdot = functools.partial(jnp.dot, preferred_element_type=F32)

    def prepare(n, _):
        rows = pl.ds(pl.multiple_of(n * CHUNK, CHUNK), CHUNK)
        gates = [_chunk_gates(col_ref.at[b], tri_ref, rows, heads, lane, rr, cc) for b in range(n_seqs)]
        pending = []
        for b, e in units:
            gc, beta, decay = gates[b][e]
            q = q_ref[b, rows, hl[e]] * (GDN_HEAD_DIM ** -0.5)
            k = k_ref[b, rows, hl[e]]
            v = v_ref[b, rows, hl[e]]
            egc = jnp.exp(gc)
            g_last = gc[CHUNK - 1:CHUNK, :]
            tinv = t_ref[b, e, n]
            t_hi = tinv.astype(BF16)
            t_lo = (tinv - t_hi.astype(F32)).astype(BF16)
            rhs = jnp.concatenate([v * beta, k * (beta * egc)], axis=1)
            r_hi = rhs.astype(BF16)
            r_lo = (rhs - r_hi.astype(F32)).astype(BF16)
            uw = fdot(jnp.concatenate([t_hi, t_lo, t_hi], axis=1), jnp.concatenate([r_hi, r_hi, r_lo], axis=0))
            attn = lax.dot_general(q.astype(BF16), k.astype(BF16), nt, preferred_element_type=F32)
            pending.append((b, e, uw, attn, decay))
            qe_s[b, rows, hl[e]] = (q * egc).astype(BF16)
            ke_s[b, e, n] = (k * jnp.exp(g_last - gc)).T.astype(BF16)
            eg_s[b, e, pl.ds(n, 1), :] = jnp.broadcast_to(jnp.exp(g_last), (1, LANES))
        for b, e, uw, attn, decay in pending:
            u_s[b, rows, hl[e]] = uw[:, :GDN_HEAD_DIM]
            w_s[b, rows, hl[e]] = uw[:, GDN_HEAD_DIM:].astype(BF16)
            at_s[b, e, rows, :] = (attn * decay).astype(BF16)
        return 0

    lax.fori_loop(0, n_chunks, prepare, 0)

    def scan(n, states):
        rows = pl.ds(pl.multiple_of(n * CHUNK, CHUNK), CHUNK)
        sbs = [st.astype(BF16) for st in states]
        ws = [fdot(w_s[b, rows, hl[e]], sbs[i]) for i, (b, e) in enumerate(units)]
        qs = [fdot(qe_s[b, rows, hl[e]], sbs[i]) for i, (b, e) in enumerate(units)]
        v_new = [(u_s[b, rows, hl[e]] - ws[i]).astype(BF16) for i, (b, e) in enumerate(units)]
        av = [fdot(at_s[b, e, rows, :], v_new[i]) for i, (b, e) in enumerate(units)]
        kv = [fdot(ke_s[b, e, n], v_new[i]) for i, (b, e) in enumerate(units)]
        for i, (b, e) in enumerate(units):
            o_ref[b, rows, hl[e]] = qs[i] + av[i]
        return tuple(states[i] * eg_s[b, e, pl.ds(n, 1), :] + kv[i] for i, (b, e) in enumerate(units))

    init = tuple(state_s[b, e] for b, e in units)
    final = lax.fori_loop(0, n_chunks, scan, init)
    for i, (b, e) in enumerate(units):
        state_s[b, e] = final[i]
        s_ref[b, e] = final[i]


GDN_SEGMENT = 512
GDN_SEQS_PER_STEP = 2


def _gdn_prompt(qkv, col, tri64, batch, seq):
    n = qkv.shape[0]
    n_chunks = seq // CHUNK
    n_seg = seq // GDN_SEGMENT
    seg_chunks = GDN_SEGMENT // CHUNK
    blk = lambda off: pl.BlockSpec((GDN_SEGMENT, GDN_WIDTH), lambda b, s: (b * n_seg + s, off))
    col_spec = pl.BlockSpec((GDN_SEGMENT, LANES), lambda b, s: (b * n_seg + s, 0))
    tri_spec = pl.BlockSpec((CHUNK, CHUNK), lambda b, s: (0, 0))
    mat_spec = pl.BlockSpec((1, GDN_HEADS, seg_chunks, CHUNK, CHUNK), lambda b, s: (b, 0, s, 0, 0))
    params = pltpu.CompilerParams(dimension_semantics=("arbitrary",) * 2, vmem_limit_bytes=VMEM_LIMIT)
    low = pl.pallas_call(
        _gdn_lower_body,
        grid=(batch, n_seg),
        in_specs=[blk(1), col_spec, tri_spec],
        out_specs=mat_spec,
        out_shape=jax.ShapeDtypeStruct((batch, GDN_HEADS, n_chunks, CHUNK, CHUNK), F32),
        compiler_params=params,
        name="gdn_lower",
    )(qkv, col, tri64)
    n_sys = batch * GDN_HEADS * n_chunks
    tinv = _unit_lower_inverse(low.reshape(n_sys // LANES, LANES, CHUNK, CHUNK)).reshape(low.shape)

    nb = GDN_SEQS_PER_STEP if batch % GDN_SEQS_PER_STEP == 0 else 1
    qkv3 = qkv.reshape(batch, seq, 3 * GDN_WIDTH)
    seq_blk = lambda width, off: pl.BlockSpec((nb, GDN_SEGMENT, width), lambda b, s: (b, s, off))
    state_spec = pl.BlockSpec((nb, GDN_HEADS, GDN_HEAD_DIM, GDN_HEAD_DIM), lambda b, s: (b, 0, 0, 0))
    o, s_fin = pl.pallas_call(
        _gdn_prompt_body,
        grid=(batch // nb, n_seg),
        in_specs=[seq_blk(GDN_WIDTH, 0), seq_blk(GDN_WIDTH, 1), seq_blk(GDN_WIDTH, 2), seq_blk(LANES, 0), tri_spec,
                  pl.BlockSpec((nb, GDN_HEADS, seg_chunks, CHUNK, CHUNK), lambda b, s: (b, 0, s, 0, 0))],
        out_specs=[seq_blk(GDN_WIDTH, 0), state_spec],
        out_shape=[jax.ShapeDtypeStruct((batch, seq, GDN_WIDTH), F32),
                   jax.ShapeDtypeStruct((batch, GDN_HEADS, GDN_HEAD_DIM, GDN_HEAD_DIM), F32)],
        scratch_shapes=[pltpu.VMEM((nb, GDN_HEADS, GDN_HEAD_DIM, GDN_HEAD_DIM), F32),
                        pltpu.VMEM((nb, GDN_SEGMENT, GDN_WIDTH), F32), pltpu.VMEM((nb, GDN_SEGMENT, GDN_WIDTH), BF16),
                        pltpu.VMEM((nb, GDN_SEGMENT, GDN_WIDTH), BF16),
                        pltpu.VMEM((nb, GDN_HEADS, seg_chunks, GDN_HEAD_DIM, CHUNK), BF16),
                        pltpu.VMEM((nb, GDN_HEADS, GDN_SEGMENT, CHUNK), BF16),
                        pltpu.VMEM((nb, GDN_HEADS, seg_chunks, LANES), F32)],
        compiler_params=params,
        name="gdn_prompt",
    )(qkv3, qkv3, qkv3, col.reshape(batch, seq, LANES), tri64, tinv)
    return o.reshape(n, GDN_WIDTH), s_fin


def _fox_prompt(q, col, kT, vT, cT, tq):
    n = q.shape[0]
    batch, _, seq = kT.shape
    nq = seq // tq
    pair = 2 * FOX_HEAD_DIM
    return pl.pallas_call(
        functools.partial(_fox_prompt_body, tq=tq),
        grid=(batch, FOX_HEADS // 2, nq),
        in_specs=[
            pl.BlockSpec((tq, pair), lambda b, p, i: (b * nq + i, p)),
            pl.BlockSpec((tq, LANES), lambda b, p, i: (b * nq + i, 0)),
            pl.BlockSpec((1, pair, seq), lambda b, p, i: (b, p, 0)),
            pl.BlockSpec((1, pair, seq), lambda b, p, i: (b, p, 0)),
            pl.BlockSpec((1, FOX_HEADS, seq), lambda b, p, i: (b, 0, 0)),
        ],
        out_specs=pl.BlockSpec((tq, pair), lambda b, p, i: (b * nq + i, p)),
        out_shape=jax.ShapeDtypeStruct((n, FOX_WIDTH), BF16),
        scratch_shapes=[pltpu.VMEM((2, FOX_HEAD_DIM + 16, seq), BF16), pltpu.VMEM((2, FOX_HEAD_DIM + 16, seq), BF16)],
        compiler_params=pltpu.CompilerParams(dimension_semantics=("arbitrary",) * 3, vmem_limit_bytes=VMEM_LIMIT),
        name="fox_prompt",
    )(q, col, kT, vT, cT)


ROW_TILE = 512
FOX_Q_TILE = 512


def kernel(x_prompt, x_sample, cache_k, cache_v, cache_logf, state_gdn, state_conv, page_table,
           ffn1_norm_pre, ffn1_w_gate, ffn1_w_up, ffn1_w_down, ffn1_norm_post,
           mix_norm_pre, w_in, fox_forget_bias, gdn_conv_w, gdn_dt_bias, gdn_a_log, gdn_out_norm,
           w_branch_fox, w_branch_gdn, w_out, mix_norm_post,
           ffn2_norm_pre, ffn2_w_gate, ffn2_w_up, ffn2_w_down, ffn2_norm_post):
    batch, seq, d = x_prompt.shape
    db = x_sample.shape[0]
    assert x_sample.shape[1] == 1, "sample group carries one new token per sequence"
    depth = w_in.shape[0]
    xp = x_prompt.reshape(batch * seq, d)
    xs = x_sample.reshape(db, d)
    row = lambda v: v.reshape(1, -1).astype(F32)
    st_p, st_s = [], []
    for l in range(depth):
        ffn1 = _ffn_weights(ffn1_w_gate[l], ffn1_w_up[l], ffn1_w_down[l])
        ffn2 = _ffn_weights(ffn2_w_gate[l], ffn2_w_up[l], ffn2_w_down[l])
        w = _mixer_weights(w_in[l], fox_forget_bias[l], gdn_conv_w[l], gdn_dt_bias[l], gdn_a_log[l])
        out_w = (row(gdn_out_norm[l]), w_branch_fox[l], w_branch_gdn[l], w_out[l], row(mix_norm_post[l]))

        xp = _ffn(xp, row(ffn1_norm_pre[l]), *ffn1, row(ffn1_norm_post[l]), ROW_TILE)
        (q, kT, vT, lfT, cT, _, col, qkv, conv_p, z, ga, gb) = _proj_prompt(
            xp, row(mix_norm_pre[l]), w, batch, seq, ROW_TILE)
        o_a = _fox_prompt(q, col, kT, vT, cT, FOX_Q_TILE)
        o_g, s_p = _gdn_prompt(qkv, col, w["tri"][:CHUNK, :CHUNK], batch, seq)
        xp = _mix_out(xp, o_a, o_g, z, ga, gb, *out_w, ROW_TILE)
        xp = _ffn(xp, row(ffn2_norm_pre[l]), *ffn2, row(ffn2_norm_post[l]), ROW_TILE)
        to_heads = lambda t: t.reshape(batch, FOX_HEADS, FOX_HEAD_DIM, seq).transpose(0, 3, 1, 2)
        st_p.append((to_heads(kT), to_heads(vT), lfT.transpose(0, 2, 1), s_p, conv_p))

        xs = _ffn(xs, row(ffn1_norm_pre[l]), *ffn1, row(ffn1_norm_post[l]), db)
        buf = state_conv[l].transpose(1, 0, 2)
        bufT = state_conv[l][:, :, :2 * GDN_WIDTH].transpose(1, 2, 0)
        (qTs, kTs, vTs, small, smallT, u, qkT, vg, zs, gas, gbs) = _proj_sample(
            xs, row(mix_norm_pre[l]), w, buf, bufT)
        o_as = _fox_decode(page_table, qTs, kTs, vTs, smallT,
                           cache_k[l].transpose(0, 2, 3, 1), cache_v[l].transpose(0, 2, 3, 1),
                           cache_logf[l].transpose(0, 2, 1))
        o_gs, s_s = _gdn_decode(qkT, vg, smallT, state_gdn[l])
        xs = _mix_out(xs, o_as.astype(BF16), o_gs, zs, gas, gbs, *out_w, db)
        xs = _ffn(xs, row(ffn2_norm_pre[l]), *ffn2, row(ffn2_norm_post[l]), db)
        new_conv = jnp.concatenate([state_conv[l][:, 1:], u[:, None, :]], axis=1)
        st_s.append((kTs.T.reshape(db, 1, FOX_HEADS, FOX_HEAD_DIM), vTs.T.reshape(db, 1, FOX_HEADS, FOX_HEAD_DIM),
                     small[:, :FOX_HEADS].reshape(db, 1, FOX_HEADS), s_s, new_conv))

    stack = lambda states, i: jnp.stack([s[i] for s in states], axis=0)
    return (xp.reshape(batch, seq, d), xs.reshape(db, 1, d),
            *[stack(st_p, i) for i in range(5)], *[stack(st_s, i) for i in range(5)])
```

```python
import functools

import jax
import jax.numpy as jnp
from jax import lax
from jax.experimental import pallas as pl
from jax.experimental.pallas import tpu as pltpu

F32 = jnp.float32
BF16 = jnp.bfloat16
EPS = 1e-6
LOG2E = 1.4426950408889634
LANES = 128
VMEM_LIMIT = 56 * 1024 * 1024

FOX_HEADS = 8
FOX_HEAD_DIM = 64
FOX_WIDTH = FOX_HEADS * FOX_HEAD_DIM
GDN_HEADS = 4
GDN_HEAD_DIM = 128
GDN_WIDTH = GDN_HEADS * GDN_HEAD_DIM
CONV_WIDTH = 4
CONV_CH = 3 * GDN_WIDTH
CHUNK = 64
FF_CHUNK = 256


def _rms(x, g):
    return x * lax.rsqrt(jnp.mean(x * x, axis=-1, keepdims=True) + EPS) * g


def _sigmoid(x):
    return 1.0 / (1.0 + jnp.exp(-x))


def _silu(x):
    return x * _sigmoid(x)


def _const_spec(shape):
    n = len(shape)
    return pl.BlockSpec(shape, lambda *_: (0,) * n, pipeline_mode=pl.Buffered(1))


def _ffn_body(x_ref, gpre_ref, wg_ref, wu_ref, wd_ref, gpost_ref, o_ref):
    x = x_ref[...]
    h = _rms(x, gpre_ref[...]).astype(BF16)
    y = jnp.zeros(x.shape, F32)
    for j in range(wg_ref.shape[1] // FF_CHUNK):
        cols = slice(j * FF_CHUNK, (j + 1) * FF_CHUNK)
        g = jnp.dot(h, wg_ref[:, cols].astype(BF16), preferred_element_type=F32)
        u = jnp.dot(h, wu_ref[:, cols].astype(BF16), preferred_element_type=F32)
        a = (_silu(g) * u).astype(BF16)
        y = y + jnp.dot(a, wd_ref[cols, :].astype(BF16), preferred_element_type=F32)
    o_ref[...] = x + 0.5 * _rms(y, gpost_ref[...])


def _ffn(x, g_pre, wg, wu, wd, g_post, tm):
    n, d = x.shape
    row = pl.BlockSpec((tm, d), lambda i: (i, 0))
    return pl.pallas_call(
        _ffn_body,
        grid=(pl.cdiv(n, tm),),
        in_specs=[row, _const_spec((1, d)), _const_spec(wg.shape), _const_spec(wu.shape),
                  _const_spec(wd.shape), _const_spec((1, d))],
        out_specs=row,
        out_shape=jax.ShapeDtypeStruct((n, d), F32),
        compiler_params=pltpu.CompilerParams(dimension_semantics=("arbitrary",), vmem_limit_bytes=VMEM_LIMIT),
        name="ffn",
    )(x, g_pre, wg, wu, wd, g_post)


def _ffn_weights(w_gate, w_up, w_down):
    assert w_gate.shape[1] % FF_CHUNK == 0
    return w_gate, w_up, w_down


def _split3(x):
    hi = x.astype(BF16)
    r = x - hi.astype(F32)
    mid = r.astype(BF16)
    lo = (r - mid.astype(F32)).astype(BF16)
    return hi, mid, lo


def _dot3(tri_bf16, x):
    hi, mid, lo = _split3(x)
    d = functools.partial(jnp.dot, tri_bf16, preferred_element_type=F32)
    return d(hi) + d(mid) + d(lo)


def _softplus(x):
    return jnp.maximum(x, 0.0) + jnp.log(1.0 + jnp.exp(-jnp.abs(x)))


def _small_cols(small, bias_ref, aneg_ref):
    lane = lax.broadcasted_iota(jnp.int32, small.shape, 1)
    pre = small + bias_ref[...]
    sp = _softplus(pre)
    logf = pre - sp
    g = aneg_ref[...] * sp
    beta = _sigmoid(small)
    gb = jnp.where(lane < FOX_HEADS + GDN_HEADS, g, beta)
    return logf, gb


def _l2n_heads(x):
    outs = []
    for h in range(x.shape[1] // GDN_HEAD_DIM):
        xh = x[:, h * GDN_HEAD_DIM:(h + 1) * GDN_HEAD_DIM]
        outs.append(xh * lax.rsqrt(jnp.sum(xh * xh, axis=-1, keepdims=True) + EPS))
    return jnp.concatenate(outs, axis=-1)


def _proj_prompt_body(x_ref, gpre_ref, wq_ref, wkT_ref, wvT_ref, ws_ref, wc_ref, wz_ref, wga_ref, wgb_ref,
                      bias_ref, aneg_ref, convw_ref, tri_ref,
                      q_ref, kT_ref, vT_ref, lfT_ref, cT_ref, gbT_ref, col_ref, qkv_ref, cst_ref,
                      z_ref, ga_ref, gb_ref, ubuf, ccarry, *, tiles_per_seq):
    i = pl.program_id(0)
    tm = x_ref.shape[0]

    @pl.when(i % tiles_per_seq == 0)
    def _():
        ubuf[0:8, :] = jnp.zeros((8, ubuf.shape[1]), F32)
        ccarry[...] = jnp.zeros(ccarry.shape, F32)

    h = _rms(x_ref[...], gpre_ref[...]).astype(BF16)
    dot = functools.partial(jnp.dot, h, preferred_element_type=F32)
    nt = (((1,), (1,)), ((), ()))

    ubuf[8:8 + tm, :] = dot(wc_ref[...])
    small = dot(ws_ref[...])
    q_ref[...] = (dot(wq_ref[...]) * (LOG2E * FOX_HEAD_DIM ** -0.5)).astype(BF16)
    kT_ref[0] = lax.dot_general(wkT_ref[...], h, nt, preferred_element_type=F32)
    vT_ref[0] = lax.dot_general(wvT_ref[...], h, nt, preferred_element_type=F32)
    z_ref[...] = dot(wz_ref[...]).astype(BF16)
    ga_ref[...] = _sigmoid(dot(wga_ref[...])).astype(BF16)
    gb_ref[...] = _sigmoid(dot(wgb_ref[...])).astype(BF16)

    logf, gb = _small_cols(small, bias_ref, aneg_ref)
    carry = ccarry[...]
    blocks = []
    for r in range(tm // LANES):
        cb = _dot3(tri_ref[...], logf[r * LANES:(r + 1) * LANES, :]) + carry
        carry = cb[LANES - 1:LANES, :]
        blocks.append(cb)
    ccarry[...] = carry
    c = jnp.concatenate(blocks, axis=0)
    lane = lax.broadcasted_iota(jnp.int32, c.shape, 1)
    col = jnp.where(lane < FOX_HEADS, c, gb)
    col_ref[...] = col
    colT = col.T
    lfT_ref[0] = logf.T[0:FOX_HEADS, :]
    cT_ref[0] = colT[0:FOX_HEADS, :]
    gbT_ref[0] = colT[FOX_HEADS:2 * FOX_HEADS, :]

    base = 8 - (CONV_WIDTH - 1)
    conv = ubuf[base:base + tm, :] * convw_ref[0:1, :]
    for k in range(1, CONV_WIDTH):
        conv = conv + ubuf[base + k:base + k + tm, :] * convw_ref[k:k + 1, :]
    cst_ref[0] = ubuf[tm + base:tm + 8, :]
    ubuf[0:8, :] = ubuf[tm:tm + 8, :]
    conv = _silu(conv)
    qkv_ref[:, 0:2 * GDN_WIDTH] = _l2n_heads(conv[:, 0:2 * GDN_WIDTH])
    qkv_ref[:, 2 * GDN_WIDTH:] = conv[:, 2 * GDN_WIDTH:]


def _proj_prompt(x, g_pre, w, batch, seq, tm):
    n, d = x.shape
    tps = seq // tm
    row = lambda width: pl.BlockSpec((tm, width), lambda i: (i, 0))
    seqT = lambda rows: pl.BlockSpec((1, rows, tm), lambda i: (i // tps, 0, i % tps))
    consts = [g_pre, w["q"], w["kT"], w["vT"], w["small"], w["conv"], w["z"], w["ga"], w["gb"],
              w["bias"], w["aneg"], w["convw"], w["tri"]]
    out_shape = [
        jax.ShapeDtypeStruct((n, FOX_WIDTH), BF16),
        jax.ShapeDtypeStruct((batch, FOX_WIDTH, seq), F32),
        jax.ShapeDtypeStruct((batch, FOX_WIDTH, seq), F32),
        jax.ShapeDtypeStruct((batch, FOX_HEADS, seq), F32),
        jax.ShapeDtypeStruct((batch, FOX_HEADS, seq), F32),
        jax.ShapeDtypeStruct((batch, 2 * GDN_HEADS, seq), F32),
        jax.ShapeDtypeStruct((n, LANES), F32),
        jax.ShapeDtypeStruct((n, CONV_CH), F32),
        jax.ShapeDtypeStruct((batch, CONV_WIDTH - 1, CONV_CH), F32),
        jax.ShapeDtypeStruct((n, GDN_WIDTH), BF16),
        jax.ShapeDtypeStruct((n, d), BF16),
        jax.ShapeDtypeStruct((n, d), BF16),
    ]
    out_specs = [row(FOX_WIDTH), seqT(FOX_WIDTH), seqT(FOX_WIDTH), seqT(FOX_HEADS), seqT(FOX_HEADS),
                 seqT(2 * GDN_HEADS), row(LANES), row(CONV_CH),
                 pl.BlockSpec((1, CONV_WIDTH - 1, CONV_CH), lambda i: (i // tps, 0, 0)),
                 row(GDN_WIDTH), row(d), row(d)]
    return pl.pallas_call(
        functools.partial(_proj_prompt_body, tiles_per_seq=tps),
        grid=(n // tm,),
        in_specs=[row(d)] + [_const_spec(a.shape) for a in consts],
        out_specs=out_specs,
        out_shape=out_shape,
        scratch_shapes=[pltpu.VMEM((tm + 8, CONV_CH), F32), pltpu.VMEM((1, LANES), F32)],
        compiler_params=pltpu.CompilerParams(dimension_semantics=("arbitrary",), vmem_limit_bytes=VMEM_LIMIT),
        name="proj_prompt",
    )(x, *consts)


def _mixer_weights(w_in, b_f, conv_w, dt_bias, a_log):
    sizes = (FOX_WIDTH, FOX_WIDTH, FOX_WIDTH, FOX_HEADS, CONV_CH, GDN_HEADS, GDN_HEADS, GDN_WIDTH)
    d = w_in.shape[0]
    offs = [0]
    for s in sizes:
        offs.append(offs[-1] + s)
    part = lambda k: w_in[:, offs[k]:offs[k + 1]]
    ga0 = offs[-1]
    n_small = FOX_HEADS + 2 * GDN_HEADS
    small = jnp.concatenate([part(3), part(5), part(6), jnp.zeros((d, LANES - n_small), w_in.dtype)], axis=1)
    pad = lambda v, lo: jnp.zeros((1, LANES), F32).at[0, lo:lo + v.shape[0]].set(v.astype(F32))
    r = lax.broadcasted_iota(jnp.int32, (LANES, LANES), 0)
    c = lax.broadcasted_iota(jnp.int32, (LANES, LANES), 1)
    return {
        "q": part(0).astype(BF16), "kT": part(1).T.astype(BF16), "vT": part(2).T.astype(BF16),
        "small": small.astype(BF16), "conv": part(4).astype(BF16), "z": part(7).astype(BF16),
        "ga": w_in[:, ga0:ga0 + d].astype(BF16), "gb": w_in[:, ga0 + d:ga0 + 2 * d].astype(BF16),
        "bias": pad(b_f, 0) + pad(dt_bias, FOX_HEADS),
        "aneg": pad(-jnp.exp(a_log.astype(F32)), FOX_HEADS),
        "convw": conv_w.astype(F32),
        "tri": (c <= r).astype(BF16),
    }


NEG_BIG = -1e30


def _split3_f32(x):
    hi, mid, lo = _split3(x)
    return hi.astype(F32), mid.astype(F32), lo.astype(F32)


def _fox_prompt_body(q_ref, col_ref, kT_ref, vT_ref, cT_ref, o_ref, ka_ref, vb_ref, *, tq):
    p = pl.program_id(1)
    qi = pl.program_id(2)
    hd = FOX_HEAD_DIM
    n_aug = 16

    @pl.when(qi == 0)
    def _():
        sub = lax.broadcasted_iota(jnp.int32, (n_aug, ka_ref.shape[2]), 0)
        for e in range(2):
            vb_ref[e, 0:hd, :] = vT_ref[0, e * hd:(e + 1) * hd, :].astype(BF16)
            vb_ref[e, hd:hd + n_aug, :] = jnp.where(sub == 0, 1.0, 0.0).astype(BF16)
            ka_ref[e, 0:hd, :] = kT_ref[0, e * hd:(e + 1) * hd, :].astype(BF16)
            hi, mid, lo = _split3_f32(cT_ref[0, pl.ds(2 * p + e, 1), :] * LOG2E)
            aug = jnp.where(sub < 3, 1.0, jnp.where(sub == 3, -hi, jnp.where(sub == 4, -mid,
                            jnp.where(sub == 5, -lo, 0.0))))
            ka_ref[e, hd:hd + n_aug, :] = aug.astype(BF16)

    nt = (((1,), (1,)), ((), ()))
    col = col_ref[...]
    lane = lax.broadcasted_iota(jnp.int32, col.shape, 1)
    q_pair = q_ref[...].astype(F32)
    qs = []
    for e in range(2):
        cq = jnp.sum(jnp.where(lane == 2 * p + e, col, 0.0), axis=-1, keepdims=True) * LOG2E
        hi, mid, lo = _split3_f32(cq)
        extra = jnp.where(lane == hd, hi, jnp.where(lane == hd + 1, mid, jnp.where(lane == hd + 2, lo,
                          jnp.where(lane < hd + 6, 1.0, 0.0))))
        qh = q_pair if e == 0 else pltpu.roll(q_pair, hd, axis=1)
        qs.append(jnp.where(lane < hd, qh, extra)[:, 0:hd + n_aug].astype(BF16))

    def block(kb, carry):
        ks = pl.ds(pl.multiple_of(kb * tq, tq), tq)
        raw = [jnp.dot(qs[e], ka_ref[e, :, ks], preferred_element_type=F32) for e in range(2)]
        stats = []
        for e in range(2):
            m, _ = carry[e]
            s = raw[e]
            m_new = jnp.maximum(m, jnp.max(s, axis=-1, keepdims=True))
            stats.append((m_new, jnp.exp2(m - m_new), jnp.exp2(s - m_new).astype(BF16)))
        pvs = [lax.dot_general(stats[e][2], vb_ref[e, :, ks], nt, preferred_element_type=F32) for e in range(2)]
        return tuple((stats[e][0], stats[e][1] * carry[e][1] + pvs[e]) for e in range(2))

    def diagonal_block(carry):
        k0 = pl.multiple_of(qi * tq, tq)
        half = tq // 2
        parts = [(e, r0, width) for e in range(2) for (r0, width) in ((0, half), (half, tq))]
        raw = [jnp.dot(qs[e][r0:r0 + half, :], ka_ref[e, :, pl.ds(k0, width)], preferred_element_type=F32)
               for e, r0, width in parts]
        stats = []
        for i, (e, r0, width) in enumerate(parts):
            m = carry[e][0][r0:r0 + half, :]
            r = lax.broadcasted_iota(jnp.int32, raw[i].shape, 0) + r0
            c = lax.broadcasted_iota(jnp.int32, raw[i].shape, 1)
            s = jnp.where(c <= r, raw[i], NEG_BIG)
            m_new = jnp.maximum(m, jnp.max(s, axis=-1, keepdims=True))
            stats.append((m_new, jnp.exp2(m - m_new), jnp.exp2(s - m_new).astype(BF16)))
        pvs = [lax.dot_general(stats[i][2], vb_ref[e, :, pl.ds(k0, width)], nt, preferred_element_type=F32)
               for i, (e, r0, width) in enumerate(parts)]
        accs = [stats[i][1] * carry[e][1][r0:r0 + half, :] + pvs[i] for i, (e, r0, width) in enumerate(parts)]
        return tuple(jnp.concatenate(accs[2 * e:2 * e + 2], axis=0) for e in range(2))

    init = tuple((jnp.full((tq, 1), NEG_BIG, F32), jnp.zeros((tq, hd + n_aug), F32)) for _ in range(2))
    carry = lax.fori_loop(0, qi, block, init)
    accs = diagonal_block(carry)
    o_ref[...] = jnp.concatenate([acc[:, 0:hd] / acc[:, hd:hd + 1] for acc in accs], axis=-1).astype(o_ref.dtype)


def _proj_sample_body(x_ref, gpre_ref, wqT_ref, wkT_ref, wvT_ref, ws_ref, wc_ref, wcT_ref, wz_ref, wga_ref, wgb_ref,
                      bias_ref, aneg_ref, convw_ref, convwT_ref, buf_ref, bufT_ref,
                      qT_ref, kT_ref, vT_ref, small_ref, smallT_ref, u_ref, qkT_ref, vg_ref, z_ref, ga_ref, gb_ref):
    h = _rms(x_ref[...], gpre_ref[...]).astype(BF16)
    dot = functools.partial(jnp.dot, h, preferred_element_type=F32)
    dot_t = lambda w: lax.dot_general(w, h, (((1,), (1,)), ((), ())), preferred_element_type=F32)
    qT_ref[...] = dot_t(wqT_ref[...]) * (FOX_HEAD_DIM ** -0.5)
    kT_ref[...] = dot_t(wkT_ref[...])
    vT_ref[...] = dot_t(wvT_ref[...])
    z_ref[...] = dot(wz_ref[...]).astype(BF16)
    ga_ref[...] = _sigmoid(dot(wga_ref[...])).astype(BF16)
    gb_ref[...] = _sigmoid(dot(wgb_ref[...])).astype(BF16)

    logf, gb = _small_cols(dot(ws_ref[...]), bias_ref, aneg_ref)
    lane = lax.broadcasted_iota(jnp.int32, logf.shape, 1)
    small = jnp.where(lane < FOX_HEADS, logf, gb)
    small_ref[...] = small
    pad = jnp.concatenate([small, jnp.zeros((LANES - small.shape[0], LANES), F32)], axis=0)
    smallT_ref[...] = pad.T[:, 0:small.shape[0]]

    u = dot(wc_ref[...])
    u_ref[...] = u
    vs = slice(2 * GDN_WIDTH, CONV_CH)
    conv_v = u[:, vs] * convw_ref[CONV_WIDTH - 1:CONV_WIDTH, vs]
    conv_qk = dot_t(wcT_ref[...]) * convwT_ref[:, CONV_WIDTH - 1:CONV_WIDTH]
    for k in range(CONV_WIDTH - 1):
        conv_v = conv_v + buf_ref[k][:, vs] * convw_ref[k:k + 1, vs]
        conv_qk = conv_qk + bufT_ref[k] * convwT_ref[:, k:k + 1]
    vg_ref[...] = _silu(conv_v)
    conv_qk = _silu(conv_qk)
    for hd in range(2 * GDN_HEADS):
        rs = slice(hd * GDN_HEAD_DIM, (hd + 1) * GDN_HEAD_DIM)
        xh = conv_qk[rs, :]
        qkT_ref[rs, :] = xh * lax.rsqrt(jnp.sum(xh * xh, axis=0, keepdims=True) + EPS)


def _proj_sample(x, g_pre, w, buf, bufT):
    db, d = x.shape
    ins = [x, g_pre, w["q"].T, w["kT"], w["vT"], w["small"], w["conv"], w["conv"][:, :2 * GDN_WIDTH].T, w["z"],
           w["ga"], w["gb"], w["bias"], w["aneg"], w["convw"], w["convw"][:, :2 * GDN_WIDTH].T, buf, bufT]
    out_shape = [
        jax.ShapeDtypeStruct((FOX_WIDTH, db), F32), jax.ShapeDtypeStruct((FOX_WIDTH, db), F32),
        jax.ShapeDtypeStruct((FOX_WIDTH, db), F32),
        jax.ShapeDtypeStruct((db, LANES), F32), jax.ShapeDtypeStruct((LANES, db), F32),
        jax.ShapeDtypeStruct((db, CONV_CH), F32), jax.ShapeDtypeStruct((2 * GDN_WIDTH, db), F32),
        jax.ShapeDtypeStruct((db, GDN_WIDTH), F32), jax.ShapeDtypeStruct((db, GDN_WIDTH), BF16),
        jax.ShapeDtypeStruct((db, d), BF16), jax.ShapeDtypeStruct((db, d), BF16),
    ]
    return pl.pallas_call(
        _proj_sample_body, out_shape=out_shape,
        compiler_params=pltpu.CompilerParams(vmem_limit_bytes=VMEM_LIMIT),
        name="proj_sample",
    )(*ins)


def _take_lane(x, idx):
    lane = lax.broadcasted_iota(jnp.int32, x.shape, 1)
    return jnp.sum(jnp.where(lane == idx, x, 0.0), axis=-1, keepdims=True)


PAGES_PER_STEP = 8
DECODE_RING = 5


def _fox_decode_body(pt_ref, qT_ref, kTn_ref, vTn_ref, smallT_ref, sfxw_ref, ck_hbm, cv_hbm, clf_hbm, o_ref,
                     kbuf, vbuf, lfbuf, sem, qb_s, acc_s, m_s, l_s, sfx_s, al_s, p_s, *, n_pages):
    g_pages = PAGES_PER_STEP
    n_steps = n_pages // g_pages
    page = kbuf.shape[-1]
    b = pl.program_id(0)
    n_seq = pl.num_programs(0)
    hd = FOX_HEAD_DIM

    def group_copies(seq, j, slot):
        base = seq * n_pages + (n_steps - 1 - j) * g_pages
        out = []
        for g in range(g_pages):
            pg = pt_ref[base + (g_pages - 1 - g)]
            out.append(pltpu.make_async_copy(ck_hbm.at[pg], kbuf.at[slot, g], sem.at[slot]))
            out.append(pltpu.make_async_copy(cv_hbm.at[pg], vbuf.at[slot, g], sem.at[slot]))
            out.append(pltpu.make_async_copy(clf_hbm.at[pg], lfbuf.at[slot, g], sem.at[slot]))
        return out

    def per_head_dot(a, bm):
        return jnp.concatenate(
            [jnp.sum(a[h * hd:(h + 1) * hd, :] * bm[h * hd:(h + 1) * hd, :], axis=0, keepdims=True)
             for h in range(FOX_HEADS)], axis=0)

    ring = kbuf.shape[0]

    @pl.when(b == 0)
    def _():
        for t in range(ring - 1):
            for c in group_copies(t // n_steps, t % n_steps, t):
                c.start()

    qb = jnp.broadcast_to(_take_lane(qT_ref[...], b), (FOX_WIDTH, page))
    kn = jnp.broadcast_to(_take_lane(kTn_ref[...], b), (FOX_WIDTH, page))
    vn = jnp.broadcast_to(_take_lane(vTn_ref[...], b), (FOX_WIDTH, page))
    qb_s[...] = qb
    m_s[...] = per_head_dot(qb, kn)
    l_s[...] = jnp.ones(l_s.shape, F32)
    lane_w = lax.broadcasted_iota(jnp.int32, (FOX_WIDTH, page), 1)
    acc_s[...] = jnp.where(lane_w == 0, vn, 0.0)
    sfx_s[...] = jnp.broadcast_to(_take_lane(smallT_ref[0:FOX_HEADS, :], b), sfx_s.shape)

    def step(j, _):
        t = b * n_steps + j
        slot = t % ring
        nxt = t + ring - 1

        @pl.when(nxt < n_seq * n_steps)
        def _():
            for c in group_copies(nxt // n_steps, nxt % n_steps, nxt % ring):
                c.start()

        for c in group_copies(b, j, slot):
            c.wait()

        lfs = lfbuf[slot].reshape(g_pages * FOX_HEADS, page)
        hi, mid, lo = _split3(lfs)
        wdot = lambda a: jnp.dot(a, sfxw_ref[...], preferred_element_type=F32)
        red = wdot(hi) + wdot(mid) + wdot(lo)
        carry = sfx_s[...]
        scores = []
        for g in range(g_pages):
            hs = slice(g * FOX_HEADS, (g + 1) * FOX_HEADS)
            bias = carry + (red[hs, 0:page] - lfs[hs, :])
            carry = carry + red[hs, page:2 * page]
            kt = kbuf[slot, g].reshape(FOX_WIDTH, page)
            scores.append(per_head_dot(kt, qb_s[...]) + bias)
        sfx_s[...] = carry

        m_old = m_s[...]
        smax = scores[0]
        for s in scores[1:]:
            smax = jnp.maximum(smax, s)
        m_new = jnp.maximum(m_old, jnp.max(smax, axis=-1, keepdims=True))
        al_s[...] = jnp.exp(m_old - m_new)
        psum = None
        for g, s in enumerate(scores):
            p = jnp.exp(s - m_new)
            p_s[g] = p
            psum = p if psum is None else psum + p
        m_s[...] = m_new
        l_s[...] = al_s[...] * l_s[...] + jnp.sum(psum, axis=-1, keepdims=True)
        for h in range(FOX_HEADS):
            rs = slice(h * hd, (h + 1) * hd)
            acc = acc_s[rs, :] * al_s[h:h + 1, :]
            for g in range(g_pages):
                acc = acc + p_s[g, h:h + 1, :] * vbuf[slot, g, h]
            acc_s[rs, :] = acc
        return 0

    lax.fori_loop(0, n_steps, step, 0)

    inv_l = 1.0 / l_s[...]
    for h in range(FOX_HEADS):
        rs = slice(h * hd, (h + 1) * hd)
        o_ref[0, rs, :] = jnp.sum(acc_s[rs, :], axis=-1, keepdims=True) * inv_l[h:h + 1, 0:1]


def _fox_decode(page_table, qT, kTn, vTn, smallT, cache_kT, cache_vT, cache_lfT):
    db, n_pages = page_table.shape
    page = cache_kT.shape[-1]
    g_pages = PAGES_PER_STEP
    ring = DECODE_RING
    assert n_pages % g_pages == 0 and db * (n_pages // g_pages) >= ring - 1
    const = lambda shape: pl.BlockSpec(shape, lambda b, pt: (0,) * len(shape))
    hbm = pl.BlockSpec(memory_space=pl.ANY)
    ii = lax.broadcasted_iota(jnp.int32, (page, 2 * page), 0)
    jj = lax.broadcasted_iota(jnp.int32, (page, 2 * page), 1)
    sfxw = ((ii >= jj) | (jj >= page)).astype(BF16)
    in_specs = [const(qT.shape), const(kTn.shape), const(vTn.shape), const(smallT.shape), const(sfxw.shape),
                hbm, hbm, hbm]
    out = pl.pallas_call(
        functools.partial(_fox_decode_body, n_pages=n_pages),
        grid_spec=pltpu.PrefetchScalarGridSpec(
            num_scalar_prefetch=1, grid=(db,), in_specs=in_specs,
            out_specs=pl.BlockSpec((1, FOX_WIDTH, 1), lambda b, pt: (b, 0, 0)),
            scratch_shapes=[pltpu.VMEM((ring, g_pages, FOX_HEADS, FOX_HEAD_DIM, page), F32),
                            pltpu.VMEM((ring, g_pages, FOX_HEADS, FOX_HEAD_DIM, page), F32),
                            pltpu.VMEM((ring, g_pages, FOX_HEADS, page), F32),
                            pltpu.SemaphoreType.DMA((ring,)),
                            pltpu.VMEM((FOX_WIDTH, page), F32), pltpu.VMEM((FOX_WIDTH, page), F32),
                            pltpu.VMEM((FOX_HEADS, page), F32), pltpu.VMEM((FOX_HEADS, page), F32),
                            pltpu.VMEM((FOX_HEADS, page), F32), pltpu.VMEM((FOX_HEADS, page), F32),
                            pltpu.VMEM((g_pages, FOX_HEADS, page), F32)]),
        out_shape=jax.ShapeDtypeStruct((db, FOX_WIDTH, 1), F32),
        compiler_params=pltpu.CompilerParams(dimension_semantics=("arbitrary",), vmem_limit_bytes=VMEM_LIMIT),
        name="fox_decode",
    )(page_table.reshape(-1), qT, kTn, vTn, smallT, sfxw, cache_kT, cache_vT, cache_lfT)
    return out.reshape(db, FOX_WIDTH)


def _gdn_decode_body(qkT_ref, vg_ref, smallT_ref, s_ref, o_ref, snew_ref):
    n_seq = s_ref.shape[0]
    for s in range(n_seq):
        b = pl.program_id(0) * n_seq + s
        qk = _take_lane(qkT_ref[...], b)
        gbeta = _take_lane(smallT_ref[FOX_HEADS:FOX_HEADS + 2 * GDN_HEADS, :], b)
        for h in range(GDN_HEADS):
            rs = slice(h * GDN_HEAD_DIM, (h + 1) * GDN_HEAD_DIM)
            q = qk[rs, :] * (GDN_HEAD_DIM ** -0.5)
            k = qk[GDN_WIDTH + h * GDN_HEAD_DIM:GDN_WIDTH + (h + 1) * GDN_HEAD_DIM, :]
            v = vg_ref[s, :, rs]
            st = s_ref[s, h] * jnp.exp(gbeta[h:h + 1, :])
            delta = (v - jnp.sum(k * st, axis=0, keepdims=True)) * gbeta[GDN_HEADS + h:GDN_HEADS + h + 1, :]
            st = st + k * delta
            snew_ref[s, h] = st
            o_ref[s, :, rs] = jnp.sum(q * st, axis=0, keepdims=True)


GDN_DECODE_SEQS = 4


def _gdn_decode(qkT, vg, smallT, state):
    db = vg.shape[0]
    ns = GDN_DECODE_SEQS if db % GDN_DECODE_SEQS == 0 else 1
    const = lambda shape: pl.BlockSpec(shape, lambda b: (0,) * len(shape))
    st_spec = pl.BlockSpec((ns, GDN_HEADS, GDN_HEAD_DIM, GDN_HEAD_DIM), lambda b: (b, 0, 0, 0))
    row_spec = pl.BlockSpec((ns, 1, GDN_WIDTH), lambda b: (b, 0, 0))
    o, s_new = pl.pallas_call(
        _gdn_decode_body,
        grid=(db // ns,),
        in_specs=[const(qkT.shape), row_spec, const(smallT.shape), st_spec],
        out_specs=[row_spec, st_spec],
        out_shape=[jax.ShapeDtypeStruct((db, 1, GDN_WIDTH), F32), jax.ShapeDtypeStruct(state.shape, F32)],
        compiler_params=pltpu.CompilerParams(dimension_semantics=("arbitrary",)),
        name="gdn_decode",
    )(qkT, vg.reshape(db, 1, GDN_WIDTH), smallT, state)
    return o.reshape(db, GDN_WIDTH), s_new


def _mix_out_body(x_ref, oa_ref, og_ref, z_ref, ga_ref, gb_ref, ggdn_ref, wbf_ref, wbg_ref, wout_ref, gpost_ref,
                  o_ref):
    og = og_ref[...]
    z = z_ref[...].astype(F32)
    parts = []
    for h in range(GDN_HEADS):
        ls = slice(h * GDN_HEAD_DIM, (h + 1) * GDN_HEAD_DIM)
        parts.append(_rms(og[:, ls], ggdn_ref[...]) * _silu(z[:, ls]))
    ogn = jnp.concatenate(parts, axis=-1).astype(BF16)
    y_a = jnp.dot(oa_ref[...], wbf_ref[...].astype(BF16), preferred_element_type=F32)
    y_g = jnp.dot(ogn, wbg_ref[...].astype(BF16), preferred_element_type=F32)
    y = (ga_ref[...].astype(F32) * y_a + gb_ref[...].astype(F32) * y_g).astype(BF16)
    y = jnp.dot(y, wout_ref[...].astype(BF16), preferred_element_type=F32)
    o_ref[...] = x_ref[...] + _rms(y, gpost_ref[...])


def _mix_out(x, o_a, o_g, z, ga, gb, g_gdn, wbf, wbg, wout, g_post, tm):
    n, d = x.shape
    row = lambda width: pl.BlockSpec((tm, width), lambda i: (i, 0))
    consts = [g_gdn, wbf, wbg, wout, g_post]
    return pl.pallas_call(
        _mix_out_body,
        grid=(pl.cdiv(n, tm),),
        in_specs=[row(d), row(FOX_WIDTH), row(GDN_WIDTH), row(GDN_WIDTH), row(d), row(d)]
        + [_const_spec(a.shape) for a in consts],
        out_specs=row(d),
        out_shape=jax.ShapeDtypeStruct((n, d), F32),
        compiler_params=pltpu.CompilerParams(dimension_semantics=("arbitrary",), vmem_limit_bytes=VMEM_LIMIT),
        name="mix_out",
    )(x, o_a, o_g, z, ga, gb, *consts)


def _chunk_gates(col_ref, tri_ref, rows, heads, lane, rr, cc):
    colc = col_ref[rows, :]
    cs = _dot3(tri_ref[...], colc)
    keep = cc <= rr
    out = []
    for head in heads:
        gc = jnp.sum(jnp.where(lane == FOX_HEADS + head, cs, 0.0), axis=-1, keepdims=True)
        beta = jnp.sum(jnp.where(lane == FOX_HEADS + GDN_HEADS + head, colc, 0.0), axis=-1, keepdims=True)
        cmat = jnp.broadcast_to(gc, (CHUNK, CHUNK))
        decay = jnp.where(keep, jnp.exp(jnp.where(keep, cmat - cmat.T, 0.0)), 0.0)
        out.append((gc, beta, decay))
    return out


def _gdn_lower_body(k_ref, col_ref, tri_ref, l_ref):
    n_chunks = k_ref.shape[0] // CHUNK
    nt = (((1,), (1,)), ((), ()))
    rr = lax.broadcasted_iota(jnp.int32, (CHUNK, CHUNK), 0)
    cc = lax.broadcasted_iota(jnp.int32, (CHUNK, CHUNK), 1)
    lane = lax.broadcasted_iota(jnp.int32, (CHUNK, LANES), 1)

    def chunk_pair(i, _):
        work = []
        for c in range(2):
            n = 2 * i + c
            rows = pl.ds(pl.multiple_of(n * CHUNK, CHUNK), CHUNK)
            work.append((n, rows, _chunk_gates(col_ref, tri_ref, rows, range(GDN_HEADS), lane, rr, cc)))
        prods = []
        for n, rows, gates in work:
            for e, (_, beta, decay) in enumerate(gates):
                k = k_ref[rows, e * GDN_HEAD_DIM:(e + 1) * GDN_HEAD_DIM]
                kk = lax.dot_general((k * beta).astype(BF16), k.astype(BF16), nt, preferred_element_type=F32)
                prods.append((n, e, kk, decay))
        for n, e, kk, decay in prods:
            l_ref[0, e, n] = jnp.where(cc < rr, kk * decay, 0.0)
        return 0

    lax.fori_loop(0, n_chunks // 2, chunk_pair, 0)


def _unit_lower_inverse_body(l_ref, t_ref, lt_s, tt_s):
    for i in range(CHUNK):
        lt_s[i * CHUNK:(i + 1) * CHUNK, :] = l_ref[0, :, i, :].T
    sub = lax.broadcasted_iota(jnp.int32, (8, LANES), 0)
    for i in range(CHUNK):
        for kg in range(i // 8 + 1):
            acc = jnp.where(sub + 8 * kg == i, 1.0, 0.0)
            for j in range(8 * kg, i):
                acc = acc - lt_s[i * CHUNK + j:i * CHUNK + j + 1, :] * tt_s[j * CHUNK + 8 * kg:j * CHUNK + 8 * kg + 8, :]
            tt_s[i * CHUNK + 8 * kg:i * CHUNK + 8 * kg + 8, :] = acc
        for kg in range(i // 8 + 1, CHUNK // 8):
            tt_s[i * CHUNK + 8 * kg:i * CHUNK + 8 * kg + 8, :] = jnp.zeros((8, LANES), F32)
    for i in range(CHUNK):
        t_ref[0, :, i, :] = tt_s[i * CHUNK:(i + 1) * CHUNK, :].T


def _unit_lower_inverse(low):
    groups = low.shape[0]
    spec = pl.BlockSpec((1, LANES, CHUNK, CHUNK), lambda g: (g, 0, 0, 0))
    return pl.pallas_call(
        _unit_lower_inverse_body,
        grid=(groups,),
        in_specs=[spec], out_specs=spec,
        out_shape=jax.ShapeDtypeStruct(low.shape, F32),
        scratch_shapes=[pltpu.VMEM((CHUNK * CHUNK, LANES), F32), pltpu.VMEM((CHUNK * CHUNK, LANES), F32)],
        compiler_params=pltpu.CompilerParams(dimension_semantics=("arbitrary",), vmem_limit_bytes=VMEM_LIMIT),
        name="unit_lower_inverse",
    )(low)


def _gdn_prompt_body(q_ref, k_ref, v_ref, col_ref, tri_ref, t_ref, o_ref, s_ref, state_s,
                     u_s, w_s, qe_s, ke_s, at_s, eg_s):
    seg = pl.program_id(1)
    n_seqs = q_ref.shape[0]
    n_chunks = q_ref.shape[1] // CHUNK
    nt = (((1,), (1,)), ((), ()))
    rr = lax.broadcasted_iota(jnp.int32, (CHUNK, CHUNK), 0)
    cc = lax.broadcasted_iota(jnp.int32, (CHUNK, CHUNK), 1)
    lane = lax.broadcasted_iota(jnp.int32, (CHUNK, LANES), 1)

    @pl.when(seg == 0)
    def _():
        state_s[...] = jnp.zeros(state_s.shape, F32)

    heads = range(GDN_HEADS)
    units = [(b, e) for b in range(n_seqs) for e in heads]
    hl = [slice(e * GDN_HEAD_DIM, (e + 1) * GDN_HEAD_DIM) for e in heads]
    fdot = functools.partial(jnp.dot, preferred_element_type=F32)

    def prepare(n, _):
        rows = pl.ds(pl.multiple_of(n * CHUNK, CHUNK), CHUNK)
        gates = [_chunk_gates(col_ref.at[b], tri_ref, rows, heads, lane, rr, cc) for b in range(n_seqs)]
        pending = []
        for b, e in units:
            gc, beta, decay = gates[b][e]
            q = q_ref[b, rows, hl[e]] * (GDN_HEAD_DIM ** -0.5)
            k = k_ref[b, rows, hl[e]]
            v = v_ref[b, rows, hl[e]]
            egc = jnp.exp(gc)
            g_last = gc[CHUNK - 1:CHUNK, :]
            tinv = t_ref[b, e, n]
            t_hi = tinv.astype(BF16)
            t_lo = (tinv - t_hi.astype(F32)).astype(BF16)
            rhs = jnp.concatenate([v * beta, k * (beta * egc)], axis=1)
            r_hi = rhs.astype(BF16)
            r_lo = (rhs - r_hi.astype(F32)).astype(BF16)
            uw = fdot(jnp.concatenate([t_hi, t_lo, t_hi], axis=1), jnp.concatenate([r_hi, r_hi, r_lo], axis=0))
            attn = lax.dot_general(q.astype(BF16), k.astype(BF16), nt, preferred_element_type=F32)
            pending.append((b, e, uw, attn, decay))
            qe_s[b, rows, hl[e]] = (q * egc).astype(BF16)
            ke_s[b, e, n] = (k * jnp.exp(g_last - gc)).T.astype(BF16)
            eg_s[b, e, pl.ds(n, 1), :] = jnp.broadcast_to(jnp.exp(g_last), (1, LANES))
        for b, e, uw, attn, decay in pending:
            u_s[b, rows, hl[e]] = uw[:, :GDN_HEAD_DIM]
            w_s[b, rows, hl[e]] = uw[:, GDN_HEAD_DIM:].astype(BF16)
            at_s[b, e, rows, :] = (attn * decay).astype(BF16)
        return 0

    lax.fori_loop(0, n_chunks, prepare, 0)

    def scan(n, states):
        rows = pl.ds(pl.multiple_of(n * CHUNK, CHUNK), CHUNK)
        sbs = [st.astype(BF16) for st in states]
        ws = [fdot(w_s[b, rows, hl[e]], sbs[i]) for i, (b, e) in enumerate(units)]
        qs = [fdot(qe_s[b, rows, hl[e]], sbs[i]) for i, (b, e) in enumerate(units)]
        v_new = [(u_s[b, rows, hl[e]] - ws[i]).astype(BF16) for i, (b, e) in enumerate(units)]
        av = [fdot(at_s[b, e, rows, :], v_new[i]) for i, (b, e) in enumerate(units)]
        kv = [fdot(ke_s[b, e, n], v_new[i]) for i, (b, e) in enumerate(units)]
        for i, (b, e) in enumerate(units):
            o_ref[b, rows, hl[e]] = qs[i] + av[i]
        return tuple(states[i] * eg_s[b, e, pl.ds(n, 1), :] + kv[i] for i, (b, e) in enumerate(units))

    init = tuple(state_s[b, e] for b, e in units)
    final = lax.fori_loop(0, n_chunks, scan, init)
    for i, (b, e) in enumerate(units):
        state_s[b, e] = final[i]
        s_ref[b, e] = final[i]


GDN_SEGMENT = 512
GDN_SEQS_PER_STEP = 2


def _gdn_prompt(qkv, col, tri64, batch, seq):
    n = qkv.shape[0]
    n_chunks = seq // CHUNK
    n_seg = seq // GDN_SEGMENT
    seg_chunks = GDN_SEGMENT // CHUNK
    blk = lambda off: pl.BlockSpec((GDN_SEGMENT, GDN_WIDTH), lambda b, s: (b * n_seg + s, off))
    col_spec = pl.BlockSpec((GDN_SEGMENT, LANES), lambda b, s: (b * n_seg + s, 0))
    tri_spec = pl.BlockSpec((CHUNK, CHUNK), lambda b, s: (0, 0))
    mat_spec = pl.BlockSpec((1, GDN_HEADS, seg_chunks, CHUNK, CHUNK), lambda b, s: (b, 0, s, 0, 0))
    params = pltpu.CompilerParams(dimension_semantics=("arbitrary",) * 2, vmem_limit_bytes=VMEM_LIMIT)
    low = pl.pallas_call(
        _gdn_lower_body,
        grid=(batch, n_seg),
        in_specs=[blk(1), col_spec, tri_spec],
        out_specs=mat_spec,
        out_shape=jax.ShapeDtypeStruct((batch, GDN_HEADS, n_chunks, CHUNK, CHUNK), F32),
        compiler_params=params,
        name="gdn_lower",
    )(qkv, col, tri64)
    n_sys = batch * GDN_HEADS * n_chunks
    tinv = _unit_lower_inverse(low.reshape(n_sys // LANES, LANES, CHUNK, CHUNK)).reshape(low.shape)

    nb = GDN_SEQS_PER_STEP if batch % GDN_SEQS_PER_STEP == 0 else 1
    qkv3 = qkv.reshape(batch, seq, 3 * GDN_WIDTH)
    seq_blk = lambda width, off: pl.BlockSpec((nb, GDN_SEGMENT, width), lambda b, s: (b, s, off))
    state_spec = pl.BlockSpec((nb, GDN_HEADS, GDN_HEAD_DIM, GDN_HEAD_DIM), lambda b, s: (b, 0, 0, 0))
    o, s_fin = pl.pallas_call(
        _gdn_prompt_body,
        grid=(batch // nb, n_seg),
        in_specs=[seq_blk(GDN_WIDTH, 0), seq_blk(GDN_WIDTH, 1), seq_blk(GDN_WIDTH, 2), seq_blk(LANES, 0), tri_spec,
                  pl.BlockSpec((nb, GDN_HEADS, seg_chunks, CHUNK, CHUNK), lambda b, s: (b, 0, s, 0, 0))],
        out_specs=[seq_blk(GDN_WIDTH, 0), state_spec],
        out_shape=[jax.ShapeDtypeStruct((batch, seq, GDN_WIDTH), F32),
                   jax.ShapeDtypeStruct((batch, GDN_HEADS, GDN_HEAD_DIM, GDN_HEAD_DIM), F32)],
        scratch_shapes=[pltpu.VMEM((nb, GDN_HEADS, GDN_HEAD_DIM, GDN_HEAD_DIM), F32),
                        pltpu.VMEM((nb, GDN_SEGMENT, GDN_WIDTH), F32), pltpu.VMEM((nb, GDN_SEGMENT, GDN_WIDTH), BF16),
                        pltpu.VMEM((nb, GDN_SEGMENT, GDN_WIDTH), BF16),
                        pltpu.VMEM((nb, GDN_HEADS, seg_chunks, GDN_HEAD_DIM, CHUNK), BF16),
                        pltpu.VMEM((nb, GDN_HEADS, GDN_SEGMENT, CHUNK), BF16),
                        pltpu.VMEM((nb, GDN_HEADS, seg_chunks, LANES), F32)],
        compiler_params=params,
        name="gdn_prompt",
    )(qkv3, qkv3, qkv3, col.reshape(batch, seq, LANES), tri64, tinv)
    return o.reshape(n, GDN_WIDTH), s_fin


def _fox_prompt(q, col, kT, vT, cT, tq):
    n = q.shape[0]
    batch, _, seq = kT.shape
    nq = seq // tq
    pair = 2 * FOX_HEAD_DIM
    return pl.pallas_call(
        functools.partial(_fox_prompt_body, tq=tq),
        grid=(batch, FOX_HEADS // 2, nq),
        in_specs=[
            pl.BlockSpec((tq, pair), lambda b, p, i: (b * nq + i, p)),
            pl.BlockSpec((tq, LANES), lambda b, p, i: (b * nq + i, 0)),
            pl.BlockSpec((1, pair, seq), lambda b, p, i: (b, p, 0)),
            pl.BlockSpec((1, pair, seq), lambda b, p, i: (b, p, 0)),
            pl.BlockSpec((1, FOX_HEADS, seq), lambda b, p, i: (b, 0, 0)),
        ],
        out_specs=pl.BlockSpec((tq, pair), lambda b, p, i: (b * nq + i, p)),
        out_shape=jax.ShapeDtypeStruct((n, FOX_WIDTH), BF16),
        scratch_shapes=[pltpu.VMEM((2, FOX_HEAD_DIM + 16, seq), BF16), pltpu.VMEM((2, FOX_HEAD_DIM + 16, seq), BF16)],
        compiler_params=pltpu.CompilerParams(dimension_semantics=("arbitrary",) * 3, vmem_limit_bytes=VMEM_LIMIT),
        name="fox_prompt",
    )(q, col, kT, vT, cT)


ROW_TILE = 512
PROJ_TILE = 512
FOX_Q_TILE = 512


def kernel(x_prompt, x_sample, cache_k, cache_v, cache_logf, state_gdn, state_conv, page_table,
           ffn1_norm_pre, ffn1_w_gate, ffn1_w_up, ffn1_w_down, ffn1_norm_post,
           mix_norm_pre, w_in, fox_forget_bias, gdn_conv_w, gdn_dt_bias, gdn_a_log, gdn_out_norm,
           w_branch_fox, w_branch_gdn, w_out, mix_norm_post,
           ffn2_norm_pre, ffn2_w_gate, ffn2_w_up, ffn2_w_down, ffn2_norm_post):
    batch, seq, d = x_prompt.shape
    db = x_sample.shape[0]
    assert x_sample.shape[1] == 1, "sample group carries one new token per sequence"
    depth = w_in.shape[0]
    xp = x_prompt.reshape(batch * seq, d)
    xs = x_sample.reshape(db, d)
    row = lambda v: v.reshape(1, -1).astype(F32)
    st_p, st_s = [], []
    for l in range(depth):
        ffn1 = _ffn_weights(ffn1_w_gate[l], ffn1_w_up[l], ffn1_w_down[l])
        ffn2 = _ffn_weights(ffn2_w_gate[l], ffn2_w_up[l], ffn2_w_down[l])
        w = _mixer_weights(w_in[l], fox_forget_bias[l], gdn_conv_w[l], gdn_dt_bias[l], gdn_a_log[l])
        out_w = (row(gdn_out_norm[l]), w_branch_fox[l], w_branch_gdn[l], w_out[l], row(mix_norm_post[l]))

        xp = _ffn(xp, row(ffn1_norm_pre[l]), *ffn1, row(ffn1_norm_post[l]), ROW_TILE)
        (q, kT, vT, lfT, cT, _, col, qkv, conv_p, z, ga, gb) = _proj_prompt(
            xp, row(mix_norm_pre[l]), w, batch, seq, PROJ_TILE)
        o_a = _fox_prompt(q, col, kT, vT, cT, FOX_Q_TILE)
        o_g, s_p = _gdn_prompt(qkv, col, w["tri"][:CHUNK, :CHUNK], batch, seq)
        xp = _mix_out(xp, o_a, o_g, z, ga, gb, *out_w, ROW_TILE)
        xp = _ffn(xp, row(ffn2_norm_pre[l]), *ffn2, row(ffn2_norm_post[l]), ROW_TILE)
        to_heads = lambda t: t.reshape(batch, FOX_HEADS, FOX_HEAD_DIM, seq).transpose(0, 3, 1, 2)
        st_p.append((to_heads(kT), to_heads(vT), lfT.transpose(0, 2, 1), s_p, conv_p))

        xs = _ffn(xs, row(ffn1_norm_pre[l]), *ffn1, row(ffn1_norm_post[l]), db)
        buf = state_conv[l].transpose(1, 0, 2)
        bufT = state_conv[l][:, :, :2 * GDN_WIDTH].transpose(1, 2, 0)
        (qTs, kTs, vTs, small, smallT, u, qkT, vg, zs, gas, gbs) = _proj_sample(
            xs, row(mix_norm_pre[l]), w, buf, bufT)
        o_as = _fox_decode(page_table, qTs, kTs, vTs, smallT,
                           cache_k[l].transpose(0, 2, 3, 1), cache_v[l].transpose(0, 2, 3, 1),
                           cache_logf[l].transpose(0, 2, 1))
        o_gs, s_s = _gdn_decode(qkT, vg, smallT, state_gdn[l])
        xs = _mix_out(xs, o_as.astype(BF16), o_gs, zs, gas, gbs, *out_w, db)
        xs = _ffn(xs, row(ffn2_norm_pre[l]), *ffn2, row(ffn2_norm_post[l]), db)
        new_conv = jnp.concatenate([state_conv[l][:, 1:], u[:, None, :]], axis=1)
        st_s.append((kTs.T.reshape(db, 1, FOX_HEADS, FOX_HEAD_DIM), vTs.T.reshape(db, 1, FOX_HEADS, FOX_HEAD_DIM),
                     small[:, :FOX_HEADS].reshape(db, 1, FOX_HEADS), s_s, new_conv))

    stack = lambda states, i: jnp.stack([s[i] for s in states], axis=0)
    return (xp.reshape(batch, seq, d), xs.reshape(db, 1, d),
            *[stack(st_p, i) for i in range(5)], *[stack(st_s, i) for i in range(5)])
```

```python
import functools

import jax
import jax.numpy as jnp
from jax import lax
from jax.experimental import pallas as pl
from jax.experimental.pallas import tpu as pltpu

F32 = jnp.float32
BF16 = jnp.bfloat16
EPS = 1e-6
LOG2E = 1.4426950408889634
LANES = 128
VMEM_LIMIT = 56 * 1024 * 1024

FOX_HEADS = 8
FOX_HEAD_DIM = 64
FOX_WIDTH = FOX_HEADS * FOX_HEAD_DIM
GDN_HEADS = 4
GDN_HEAD_DIM = 128
GDN_WIDTH = GDN_HEADS * GDN_HEAD_DIM
CONV_WIDTH = 4
CONV_CH = 3 * GDN_WIDTH
CHUNK = 64
FF_CHUNK = 256


def _rms(x, g):
    return x * lax.rsqrt(jnp.mean(x * x, axis=-1, keepdims=True) + EPS) * g


def _sigmoid(x):
    return 1.0 / (1.0 + jnp.exp(-x))


def _silu(x):
    return x * _sigmoid(x)


def _const_spec(shape):
    n = len(shape)
    return pl.BlockSpec(shape, lambda *_: (0,) * n, pipeline_mode=pl.Buffered(1))


def _ffn_body(x_ref, xs_ref, gpre_ref, wg_ref, wu_ref, wd_ref, gpost_ref, o_ref, os_ref):
    def half_step(x):
        h = _rms(x, gpre_ref[...]).astype(BF16)
        y = jnp.zeros(x.shape, F32)
        for j in range(wg_ref.shape[1] // FF_CHUNK):
            cols = slice(j * FF_CHUNK, (j + 1) * FF_CHUNK)
            g = jnp.dot(h, wg_ref[:, cols].astype(BF16), preferred_element_type=F32)
            u = jnp.dot(h, wu_ref[:, cols].astype(BF16), preferred_element_type=F32)
            a = (_silu(g) * u).astype(BF16)
            y = y + jnp.dot(a, wd_ref[cols, :].astype(BF16), preferred_element_type=F32)
        return x + 0.5 * _rms(y, gpost_ref[...])

    last = pl.num_programs(0) - 1

    @pl.when(pl.program_id(0) < last)
    def _():
        o_ref[...] = half_step(x_ref[...])

    @pl.when(pl.program_id(0) == last)
    def _():
        os_ref[...] = half_step(xs_ref[...])


def _ffn(x, xs, g_pre, wg, wu, wd, g_post, tm):
    n, d = x.shape
    assert n % tm == 0
    n_tiles = n // tm
    row = pl.BlockSpec((tm, d), lambda i: (jnp.minimum(i, n_tiles - 1), 0))
    side = pl.BlockSpec(xs.shape, lambda i: (0, 0))
    return pl.pallas_call(
        _ffn_body,
        grid=(n_tiles + 1,),
        in_specs=[row, side, _const_spec((1, d)), _const_spec(wg.shape), _const_spec(wu.shape),
                  _const_spec(wd.shape), _const_spec((1, d))],
        out_specs=[row, side],
        out_shape=[jax.ShapeDtypeStruct((n, d), F32), jax.ShapeDtypeStruct(xs.shape, F32)],
        compiler_params=pltpu.CompilerParams(dimension_semantics=("arbitrary",), vmem_limit_bytes=VMEM_LIMIT),
        name="ffn",
    )(x, xs, g_pre, wg, wu, wd, g_post)


def _ffn_weights(w_gate, w_up, w_down):
    assert w_gate.shape[1] % FF_CHUNK == 0
    return w_gate, w_up, w_down


def _split3(x):
    hi = x.astype(BF16)
    r = x - hi.astype(F32)
    mid = r.astype(BF16)
    lo = (r - mid.astype(F32)).astype(BF16)
    return hi, mid, lo


def _dot3(tri_bf16, x):
    hi, mid, lo = _split3(x)
    d = functools.partial(jnp.dot, tri_bf16, preferred_element_type=F32)
    return d(hi) + d(mid) + d(lo)


def _softplus(x):
    return jnp.maximum(x, 0.0) + jnp.log(1.0 + jnp.exp(-jnp.abs(x)))


def _small_cols(small, bias_ref, aneg_ref):
    lane = lax.broadcasted_iota(jnp.int32, small.shape, 1)
    pre = small + bias_ref[...]
    sp = _softplus(pre)
    logf = pre - sp
    g = aneg_ref[...] * sp
    beta = _sigmoid(small)
    gb = jnp.where(lane < FOX_HEADS + GDN_HEADS, g, beta)
    return logf, gb


def _l2n_heads(x):
    outs = []
    for h in range(x.shape[1] // GDN_HEAD_DIM):
        xh = x[:, h * GDN_HEAD_DIM:(h + 1) * GDN_HEAD_DIM]
        outs.append(xh * lax.rsqrt(jnp.sum(xh * xh, axis=-1, keepdims=True) + EPS))
    return jnp.concatenate(outs, axis=-1)


def _proj_prompt_body(x_ref, gpre_ref, wq_ref, wkT_ref, wvT_ref, ws_ref, wc_ref, wz_ref, wga_ref, wgb_ref,
                      bias_ref, aneg_ref, convw_ref, tri_ref,
                      q_ref, kT_ref, vT_ref, lfT_ref, cT_ref, gbT_ref, col_ref, qkv_ref, cst_ref,
                      z_ref, ga_ref, gb_ref, ubuf, ccarry, *, tiles_per_seq):
    i = pl.program_id(0)
    tm = x_ref.shape[0]

    @pl.when(i % tiles_per_seq == 0)
    def _():
        ubuf[0:8, :] = jnp.zeros((8, ubuf.shape[1]), F32)
        ccarry[...] = jnp.zeros(ccarry.shape, F32)

    h = _rms(x_ref[...], gpre_ref[...]).astype(BF16)
    dot = functools.partial(jnp.dot, h, preferred_element_type=F32)
    nt = (((1,), (1,)), ((), ()))

    ubuf[8:8 + tm, :] = dot(wc_ref[...])
    small = dot(ws_ref[...])
    q_ref[...] = (dot(wq_ref[...]) * (LOG2E * FOX_HEAD_DIM ** -0.5)).astype(BF16)
    kT_ref[0] = lax.dot_general(wkT_ref[...], h, nt, preferred_element_type=F32)
    vT_ref[0] = lax.dot_general(wvT_ref[...], h, nt, preferred_element_type=F32)
    z_ref[...] = dot(wz_ref[...]).astype(BF16)
    ga_ref[...] = _sigmoid(dot(wga_ref[...])).astype(BF16)
    gb_ref[...] = _sigmoid(dot(wgb_ref[...])).astype(BF16)

    logf, gb = _small_cols(small, bias_ref, aneg_ref)
    carry = ccarry[...]
    blocks = []
    for r in range(tm // LANES):
        cb = _dot3(tri_ref[...], logf[r * LANES:(r + 1) * LANES, :]) + carry
        carry = cb[LANES - 1:LANES, :]
        blocks.append(cb)
    ccarry[...] = carry
    c = jnp.concatenate(blocks, axis=0)
    lane = lax.broadcasted_iota(jnp.int32, c.shape, 1)
    col = jnp.where(lane < FOX_HEADS, c, gb)
    col_ref[...] = col
    colT = col.T
    lfT_ref[0] = logf.T[0:FOX_HEADS, :]
    cT_ref[0] = colT[0:FOX_HEADS, :]
    gbT_ref[0] = colT[FOX_HEADS:2 * FOX_HEADS, :]

    base = 8 - (CONV_WIDTH - 1)
    conv = ubuf[base:base + tm, :] * convw_ref[0:1, :]
    for k in range(1, CONV_WIDTH):
        conv = conv + ubuf[base + k:base + k + tm, :] * convw_ref[k:k + 1, :]
    cst_ref[0] = ubuf[tm + base:tm + 8, :]
    ubuf[0:8, :] = ubuf[tm:tm + 8, :]
    conv = _silu(conv)
    qkv_ref[:, 0:2 * GDN_WIDTH] = _l2n_heads(conv[:, 0:2 * GDN_WIDTH])
    qkv_ref[:, 2 * GDN_WIDTH:] = conv[:, 2 * GDN_WIDTH:]


def _proj_prompt(x, g_pre, w, batch, seq, tm):
    n, d = x.shape
    tps = seq // tm
    row = lambda width: pl.BlockSpec((tm, width), lambda i: (i, 0))
    seqT = lambda rows: pl.BlockSpec((1, rows, tm), lambda i: (i // tps, 0, i % tps))
    consts = [g_pre, w["q"], w["kT"], w["vT"], w["small"], w["conv"], w["z"], w["ga"], w["gb"],
              w["bias"], w["aneg"], w["convw"], w["tri"]]
    out_shape = [
        jax.ShapeDtypeStruct((n, FOX_WIDTH), BF16),
        jax.ShapeDtypeStruct((batch, FOX_WIDTH, seq), F32),
        jax.ShapeDtypeStruct((batch, FOX_WIDTH, seq), F32),
        jax.ShapeDtypeStruct((batch, FOX_HEADS, seq), F32),
        jax.ShapeDtypeStruct((batch, FOX_HEADS, seq), F32),
        jax.ShapeDtypeStruct((batch, 2 * GDN_HEADS, seq), F32),
        jax.ShapeDtypeStruct((n, LANES), F32),
        jax.ShapeDtypeStruct((n, CONV_CH), F32),
        jax.ShapeDtypeStruct((batch, CONV_WIDTH - 1, CONV_CH), F32),
        jax.ShapeDtypeStruct((n, GDN_WIDTH), BF16),
        jax.ShapeDtypeStruct((n, d), BF16),
        jax.ShapeDtypeStruct((n, d), BF16),
    ]
    out_specs = [row(FOX_WIDTH), seqT(FOX_WIDTH), seqT(FOX_WIDTH), seqT(FOX_HEADS), seqT(FOX_HEADS),
                 seqT(2 * GDN_HEADS), row(LANES), row(CONV_CH),
                 pl.BlockSpec((1, CONV_WIDTH - 1, CONV_CH), lambda i: (i // tps, 0, 0)),
                 row(GDN_WIDTH), row(d), row(d)]
    return pl.pallas_call(
        functools.partial(_proj_prompt_body, tiles_per_seq=tps),
        grid=(n // tm,),
        in_specs=[row(d)] + [_const_spec(a.shape) for a in consts],
        out_specs=out_specs,
        out_shape=out_shape,
        scratch_shapes=[pltpu.VMEM((tm + 8, CONV_CH), F32), pltpu.VMEM((1, LANES), F32)],
        compiler_params=pltpu.CompilerParams(dimension_semantics=("arbitrary",), vmem_limit_bytes=VMEM_LIMIT),
        name="proj_prompt",
    )(x, *consts)


def _mixer_weights(w_in, b_f, conv_w, dt_bias, a_log):
    sizes = (FOX_WIDTH, FOX_WIDTH, FOX_WIDTH, FOX_HEADS, CONV_CH, GDN_HEADS, GDN_HEADS, GDN_WIDTH)
    d = w_in.shape[0]
    offs = [0]
    for s in sizes:
        offs.append(offs[-1] + s)
    part = lambda k: w_in[:, offs[k]:offs[k + 1]]
    ga0 = offs[-1]
    n_small = FOX_HEADS + 2 * GDN_HEADS
    small = jnp.concatenate([part(3), part(5), part(6), jnp.zeros((d, LANES - n_small), w_in.dtype)], axis=1)
    pad = lambda v, lo: jnp.zeros((1, LANES), F32).at[0, lo:lo + v.shape[0]].set(v.astype(F32))
    r = lax.broadcasted_iota(jnp.int32, (LANES, LANES), 0)
    c = lax.broadcasted_iota(jnp.int32, (LANES, LANES), 1)
    return {
        "q": part(0).astype(BF16), "kT": part(1).T.astype(BF16), "vT": part(2).T.astype(BF16),
        "small": small.astype(BF16), "conv": part(4).astype(BF16), "z": part(7).astype(BF16),
        "ga": w_in[:, ga0:ga0 + d].astype(BF16), "gb": w_in[:, ga0 + d:ga0 + 2 * d].astype(BF16),
        "bias": pad(b_f, 0) + pad(dt_bias, FOX_HEADS),
        "aneg": pad(-jnp.exp(a_log.astype(F32)), FOX_HEADS),
        "convw": conv_w.astype(F32),
        "tri": (c <= r).astype(BF16),
    }


NEG_BIG = -1e30


def _split3_f32(x):
    hi, mid, lo = _split3(x)
    return hi.astype(F32), mid.astype(F32), lo.astype(F32)


def _fox_prompt_body(q_ref, col_ref, kT_ref, vT_ref, cT_ref, o_ref, ka_ref, vb_ref, *, tq):
    p = pl.program_id(1)
    qi = pl.program_id(2)
    hd = FOX_HEAD_DIM
    n_aug = 16

    @pl.when(qi == 0)
    def _():
        sub = lax.broadcasted_iota(jnp.int32, (n_aug, ka_ref.shape[2]), 0)
        for e in range(2):
            vb_ref[e, 0:hd, :] = vT_ref[0, e * hd:(e + 1) * hd, :].astype(BF16)
            vb_ref[e, hd:hd + n_aug, :] = jnp.where(sub == 0, 1.0, 0.0).astype(BF16)
            ka_ref[e, 0:hd, :] = kT_ref[0, e * hd:(e + 1) * hd, :].astype(BF16)
            hi, mid, lo = _split3_f32(cT_ref[0, pl.ds(2 * p + e, 1), :] * LOG2E)
            aug = jnp.where(sub < 3, 1.0, jnp.where(sub == 3, -hi, jnp.where(sub == 4, -mid,
                            jnp.where(sub == 5, -lo, 0.0))))
            ka_ref[e, hd:hd + n_aug, :] = aug.astype(BF16)

    nt = (((1,), (1,)), ((), ()))
    col = col_ref[...]
    lane = lax.broadcasted_iota(jnp.int32, col.shape, 1)
    q_pair = q_ref[...].astype(F32)
    qs = []
    for e in range(2):
        cq = jnp.sum(jnp.where(lane == 2 * p + e, col, 0.0), axis=-1, keepdims=True) * LOG2E
        hi, mid, lo = _split3_f32(cq)
        extra = jnp.where(lane == hd, hi, jnp.where(lane == hd + 1, mid, jnp.where(lane == hd + 2, lo,
                          jnp.where(lane < hd + 6, 1.0, 0.0))))
        qh = q_pair if e == 0 else pltpu.roll(q_pair, hd, axis=1)
        qs.append(jnp.where(lane < hd, qh, extra)[:, 0:hd + n_aug].astype(BF16))

    def block(kb, carry):
        ks = pl.ds(pl.multiple_of(kb * tq, tq), tq)
        raw = [jnp.dot(qs[e], ka_ref[e, :, ks], preferred_element_type=F32) for e in range(2)]
        stats = []
        for e in range(2):
            m, _ = carry[e]
            s = raw[e]
            m_new = jnp.maximum(m, jnp.max(s, axis=-1, keepdims=True))
            stats.append((m_new, jnp.exp2(m - m_new), jnp.exp2(s - m_new).astype(BF16)))
        pvs = [lax.dot_general(stats[e][2], vb_ref[e, :, ks], nt, preferred_element_type=F32) for e in range(2)]
        return tuple((stats[e][0], stats[e][1] * carry[e][1] + pvs[e]) for e in range(2))

    def diagonal_block(carry):
        k0 = pl.multiple_of(qi * tq, tq)
        half = tq // 2
        parts = [(e, r0, width) for e in range(2) for (r0, width) in ((0, half), (half, tq))]
        raw = [jnp.dot(qs[e][r0:r0 + half, :], ka_ref[e, :, pl.ds(k0, width)], preferred_element_type=F32)
               for e, r0, width in parts]
        stats = []
        for i, (e, r0, width) in enumerate(parts):
            m = carry[e][0][r0:r0 + half, :]
            r = lax.broadcasted_iota(jnp.int32, raw[i].shape, 0) + r0
            c = lax.broadcasted_iota(jnp.int32, raw[i].shape, 1)
            s = jnp.where(c <= r, raw[i], NEG_BIG)
            m_new = jnp.maximum(m, jnp.max(s, axis=-1, keepdims=True))
            stats.append((m_new, jnp.exp2(m - m_new), jnp.exp2(s - m_new).astype(BF16)))
        pvs = [lax.dot_general(stats[i][2], vb_ref[e, :, pl.ds(k0, width)], nt, preferred_element_type=F32)
               for i, (e, r0, width) in enumerate(parts)]
        accs = [stats[i][1] * carry[e][1][r0:r0 + half, :] + pvs[i] for i, (e, r0, width) in enumerate(parts)]
        return tuple(jnp.concatenate(accs[2 * e:2 * e + 2], axis=0) for e in range(2))

    init = tuple((jnp.full((tq, 1), NEG_BIG, F32), jnp.zeros((tq, hd + n_aug), F32)) for _ in range(2))
    carry = lax.fori_loop(0, qi, block, init)
    accs = diagonal_block(carry)
    o_ref[...] = jnp.concatenate([acc[:, 0:hd] / acc[:, hd:hd + 1] for acc in accs], axis=-1).astype(o_ref.dtype)


def _proj_sample_body(x_ref, gpre_ref, wqT_ref, wkT_ref, wvT_ref, ws_ref, wc_ref, wcT_ref, wz_ref, wga_ref, wgb_ref,
                      bias_ref, aneg_ref, convw_ref, convwT_ref, buf_ref, bufT_ref,
                      qT_ref, kT_ref, vT_ref, small_ref, smallT_ref, u_ref, qkT_ref, vg_ref, z_ref, ga_ref, gb_ref):
    h = _rms(x_ref[...], gpre_ref[...]).astype(BF16)
    dot = functools.partial(jnp.dot, h, preferred_element_type=F32)
    dot_t = lambda w: lax.dot_general(w, h, (((1,), (1,)), ((), ())), preferred_element_type=F32)
    qT_ref[...] = dot_t(wqT_ref[...]) * (FOX_HEAD_DIM ** -0.5)
    kT_ref[...] = dot_t(wkT_ref[...])
    vT_ref[...] = dot_t(wvT_ref[...])
    z_ref[...] = dot(wz_ref[...]).astype(BF16)
    ga_ref[...] = _sigmoid(dot(wga_ref[...])).astype(BF16)
    gb_ref[...] = _sigmoid(dot(wgb_ref[...])).astype(BF16)

    logf, gb = _small_cols(dot(ws_ref[...]), bias_ref, aneg_ref)
    lane = lax.broadcasted_iota(jnp.int32, logf.shape, 1)
    small = jnp.where(lane < FOX_HEADS, logf, gb)
    small_ref[...] = small
    pad = jnp.concatenate([small, jnp.zeros((LANES - small.shape[0], LANES), F32)], axis=0)
    smallT_ref[...] = pad.T[:, 0:small.shape[0]]

    u = dot(wc_ref[...])
    u_ref[...] = u
    vs = slice(2 * GDN_WIDTH, CONV_CH)
    conv_v = u[:, vs] * convw_ref[CONV_WIDTH - 1:CONV_WIDTH, vs]
    conv_qk = dot_t(wcT_ref[...]) * convwT_ref[:, CONV_WIDTH - 1:CONV_WIDTH]
    for k in range(CONV_WIDTH - 1):
        conv_v = conv_v + buf_ref[k][:, vs] * convw_ref[k:k + 1, vs]
        conv_qk = conv_qk + bufT_ref[k] * convwT_ref[:, k:k + 1]
    vg_ref[...] = _silu(conv_v)
    conv_qk = _silu(conv_qk)
    for hd in range(2 * GDN_HEADS):
        rs = slice(hd * GDN_HEAD_DIM, (hd + 1) * GDN_HEAD_DIM)
        xh = conv_qk[rs, :]
        qkT_ref[rs, :] = xh * lax.rsqrt(jnp.sum(xh * xh, axis=0, keepdims=True) + EPS)


def _proj_sample(x, g_pre, w, buf, bufT):
    db, d = x.shape
    ins = [x, g_pre, w["q"].T, w["kT"], w["vT"], w["small"], w["conv"], w["conv"][:, :2 * GDN_WIDTH].T, w["z"],
           w["ga"], w["gb"], w["bias"], w["aneg"], w["convw"], w["convw"][:, :2 * GDN_WIDTH].T, buf, bufT]
    out_shape = [
        jax.ShapeDtypeStruct((FOX_WIDTH, db), F32), jax.ShapeDtypeStruct((FOX_WIDTH, db), F32),
        jax.ShapeDtypeStruct((FOX_WIDTH, db), F32),
        jax.ShapeDtypeStruct((db, LANES), F32), jax.ShapeDtypeStruct((LANES, db), F32),
        jax.ShapeDtypeStruct((db, CONV_CH), F32), jax.ShapeDtypeStruct((2 * GDN_WIDTH, db), F32),
        jax.ShapeDtypeStruct((db, GDN_WIDTH), F32), jax.ShapeDtypeStruct((db, GDN_WIDTH), BF16),
        jax.ShapeDtypeStruct((db, d), BF16), jax.ShapeDtypeStruct((db, d), BF16),
    ]
    return pl.pallas_call(
        _proj_sample_body, out_shape=out_shape,
        compiler_params=pltpu.CompilerParams(vmem_limit_bytes=VMEM_LIMIT),
        name="proj_sample",
    )(*ins)


def _take_lane(x, idx):
    lane = lax.broadcasted_iota(jnp.int32, x.shape, 1)
    return jnp.sum(jnp.where(lane == idx, x, 0.0), axis=-1, keepdims=True)


PAGES_PER_STEP = 8
DECODE_RING = 5


def _fox_decode_body(pt_ref, qT_ref, kTn_ref, vTn_ref, smallT_ref, sfxw_ref, ck_hbm, cv_hbm, clf_hbm, o_ref,
                     kbuf, vbuf, lfbuf, sem, qb_s, acc_s, m_s, l_s, sfx_s, al_s, p_s, *, n_pages):
    g_pages = PAGES_PER_STEP
    n_steps = n_pages // g_pages
    page = kbuf.shape[-1]
    b = pl.program_id(0)
    n_seq = pl.num_programs(0)
    hd = FOX_HEAD_DIM

    def group_copies(seq, j, slot):
        base = seq * n_pages + (n_steps - 1 - j) * g_pages
        out = []
        for g in range(g_pages):
            pg = pt_ref[base + (g_pages - 1 - g)]
            out.append(pltpu.make_async_copy(ck_hbm.at[pg], kbuf.at[slot, g], sem.at[slot]))
            out.append(pltpu.make_async_copy(cv_hbm.at[pg], vbuf.at[slot, g], sem.at[slot]))
            out.append(pltpu.make_async_copy(clf_hbm.at[pg], lfbuf.at[slot, g], sem.at[slot]))
        return out

    def per_head_dot(a, bm):
        return jnp.concatenate(
            [jnp.sum(a[h * hd:(h + 1) * hd, :] * bm[h * hd:(h + 1) * hd, :], axis=0, keepdims=True)
             for h in range(FOX_HEADS)], axis=0)

    ring = kbuf.shape[0]

    @pl.when(b == 0)
    def _():
        for t in range(ring - 1):
            for c in group_copies(t // n_steps, t % n_steps, t):
                c.start()

    qb = jnp.broadcast_to(_take_lane(qT_ref[...], b), (FOX_WIDTH, page))
    kn = jnp.broadcast_to(_take_lane(kTn_ref[...], b), (FOX_WIDTH, page))
    vn = jnp.broadcast_to(_take_lane(vTn_ref[...], b), (FOX_WIDTH, page))
    qb_s[...] = qb
    m_s[...] = per_head_dot(qb, kn)
    l_s[...] = jnp.ones(l_s.shape, F32)
    lane_w = lax.broadcasted_iota(jnp.int32, (FOX_WIDTH, page), 1)
    acc_s[...] = jnp.where(lane_w == 0, vn, 0.0)
    sfx_s[...] = jnp.broadcast_to(_take_lane(smallT_ref[0:FOX_HEADS, :], b), sfx_s.shape)

    def step(j, _):
        t = b * n_steps + j
        slot = t % ring
        nxt = t + ring - 1

        @pl.when(nxt < n_seq * n_steps)
        def _():
            for c in group_copies(nxt // n_steps, nxt % n_steps, nxt % ring):
                c.start()

        for c in group_copies(b, j, slot):
            c.wait()

        lfs = lfbuf[slot].reshape(g_pages * FOX_HEADS, page)
        hi, mid, lo = _split3(lfs)
        wdot = lambda a: jnp.dot(a, sfxw_ref[...], preferred_element_type=F32)
        red = wdot(hi) + wdot(mid) + wdot(lo)
        carry = sfx_s[...]
        scores = []
        for g in range(g_pages):
            hs = slice(g * FOX_HEADS, (g + 1) * FOX_HEADS)
            bias = carry + (red[hs, 0:page] - lfs[hs, :])
            carry = carry + red[hs, page:2 * page]
            kt = kbuf[slot, g].reshape(FOX_WIDTH, page)
            scores.append(per_head_dot(kt, qb_s[...]) + bias)
        sfx_s[...] = carry

        m_old = m_s[...]
        smax = scores[0]
        for s in scores[1:]:
            smax = jnp.maximum(smax, s)
        m_new = jnp.maximum(m_old, jnp.max(smax, axis=-1, keepdims=True))
        al_s[...] = jnp.exp(m_old - m_new)
        psum = None
        for g, s in enumerate(scores):
            p = jnp.exp(s - m_new)
            p_s[g] = p
            psum = p if psum is None else psum + p
        m_s[...] = m_new
        l_s[...] = al_s[...] * l_s[...] + jnp.sum(psum, axis=-1, keepdims=True)
        for h in range(FOX_HEADS):
            rs = slice(h * hd, (h + 1) * hd)
            acc = acc_s[rs, :] * al_s[h:h + 1, :]
            for g in range(g_pages):
                acc = acc + p_s[g, h:h + 1, :] * vbuf[slot, g, h]
            acc_s[rs, :] = acc
        return 0

    lax.fori_loop(0, n_steps, step, 0)

    inv_l = 1.0 / l_s[...]
    for h in range(FOX_HEADS):
        rs = slice(h * hd, (h + 1) * hd)
        o_ref[0, rs, :] = jnp.sum(acc_s[rs, :], axis=-1, keepdims=True) * inv_l[h:h + 1, 0:1]


def _fox_decode(page_table, qT, kTn, vTn, smallT, cache_kT, cache_vT, cache_lfT):
    db, n_pages = page_table.shape
    page = cache_kT.shape[-1]
    g_pages = PAGES_PER_STEP
    ring = DECODE_RING
    assert n_pages % g_pages == 0 and db * (n_pages // g_pages) >= ring - 1
    const = lambda shape: pl.BlockSpec(shape, lambda b, pt: (0,) * len(shape))
    hbm = pl.BlockSpec(memory_space=pl.ANY)
    ii = lax.broadcasted_iota(jnp.int32, (page, 2 * page), 0)
    jj = lax.broadcasted_iota(jnp.int32, (page, 2 * page), 1)
    sfxw = ((ii >= jj) | (jj >= page)).astype(BF16)
    in_specs = [const(qT.shape), const(kTn.shape), const(vTn.shape), const(smallT.shape), const(sfxw.shape),
                hbm, hbm, hbm]
    out = pl.pallas_call(
        functools.partial(_fox_decode_body, n_pages=n_pages),
        grid_spec=pltpu.PrefetchScalarGridSpec(
            num_scalar_prefetch=1, grid=(db,), in_specs=in_specs,
            out_specs=pl.BlockSpec((1, FOX_WIDTH, 1), lambda b, pt: (b, 0, 0)),
            scratch_shapes=[pltpu.VMEM((ring, g_pages, FOX_HEADS, FOX_HEAD_DIM, page), F32),
                            pltpu.VMEM((ring, g_pages, FOX_HEADS, FOX_HEAD_DIM, page), F32),
                            pltpu.VMEM((ring, g_pages, FOX_HEADS, page), F32),
                            pltpu.SemaphoreType.DMA((ring,)),
                            pltpu.VMEM((FOX_WIDTH, page), F32), pltpu.VMEM((FOX_WIDTH, page), F32),
                            pltpu.VMEM((FOX_HEADS, page), F32), pltpu.VMEM((FOX_HEADS, page), F32),
                            pltpu.VMEM((FOX_HEADS, page), F32), pltpu.VMEM((FOX_HEADS, page), F32),
                            pltpu.VMEM((g_pages, FOX_HEADS, page), F32)]),
        out_shape=jax.ShapeDtypeStruct((db, FOX_WIDTH, 1), F32),
        compiler_params=pltpu.CompilerParams(dimension_semantics=("arbitrary",), vmem_limit_bytes=VMEM_LIMIT),
        name="fox_decode",
    )(page_table.reshape(-1), qT, kTn, vTn, smallT, sfxw, cache_kT, cache_vT, cache_lfT)
    return out.reshape(db, FOX_WIDTH)


def _gdn_decode_body(qkT_ref, vg_ref, smallT_ref, s_ref, z_ref, ggdn_ref, o_ref, snew_ref):
    n_seq = s_ref.shape[0]
    for s in range(n_seq):
        b = pl.program_id(0) * n_seq + s
        qk = _take_lane(qkT_ref[...], b)
        gbeta = _take_lane(smallT_ref[FOX_HEADS:FOX_HEADS + 2 * GDN_HEADS, :], b)
        for h in range(GDN_HEADS):
            rs = slice(h * GDN_HEAD_DIM, (h + 1) * GDN_HEAD_DIM)
            q = qk[rs, :] * (GDN_HEAD_DIM ** -0.5)
            k = qk[GDN_WIDTH + h * GDN_HEAD_DIM:GDN_WIDTH + (h + 1) * GDN_HEAD_DIM, :]
            v = vg_ref[s, :, rs]
            st = s_ref[s, h] * jnp.exp(gbeta[h:h + 1, :])
            delta = (v - jnp.sum(k * st, axis=0, keepdims=True)) * gbeta[GDN_HEADS + h:GDN_HEADS + h + 1, :]
            st = st + k * delta
            snew_ref[s, h] = st
            o = jnp.sum(q * st, axis=0, keepdims=True)
            o_ref[s, :, rs] = _rms(o, ggdn_ref[...]) * _silu(z_ref[s, :, rs])


GDN_DECODE_SEQS = 4


def _gdn_decode(qkT, vg, smallT, state, z, g_gdn):
    db = vg.shape[0]
    ns = GDN_DECODE_SEQS if db % GDN_DECODE_SEQS == 0 else 1
    const = lambda shape: pl.BlockSpec(shape, lambda b: (0,) * len(shape))
    st_spec = pl.BlockSpec((ns, GDN_HEADS, GDN_HEAD_DIM, GDN_HEAD_DIM), lambda b: (b, 0, 0, 0))
    row_spec = pl.BlockSpec((ns, 1, GDN_WIDTH), lambda b: (b, 0, 0))
    o, s_new = pl.pallas_call(
        _gdn_decode_body,
        grid=(db // ns,),
        in_specs=[const(qkT.shape), row_spec, const(smallT.shape), st_spec, row_spec, const(g_gdn.shape)],
        out_specs=[row_spec, st_spec],
        out_shape=[jax.ShapeDtypeStruct((db, 1, GDN_WIDTH), F32), jax.ShapeDtypeStruct(state.shape, F32)],
        compiler_params=pltpu.CompilerParams(dimension_semantics=("arbitrary",)),
        name="gdn_decode",
    )(qkT, vg.reshape(db, 1, GDN_WIDTH), smallT, state, z.astype(F32).reshape(db, 1, GDN_WIDTH), g_gdn)
    return o.reshape(db, GDN_WIDTH), s_new


def _mix_out_body(x_ref, oa_ref, og_ref, ga_ref, gb_ref, wbf_ref, wbg_ref, wout_ref, gpost_ref, o_ref):
    y_a = jnp.dot(oa_ref[...], wbf_ref[...].astype(BF16), preferred_element_type=F32)
    y_g = jnp.dot(og_ref[...], wbg_ref[...].astype(BF16), preferred_element_type=F32)
    y = (ga_ref[...].astype(F32) * y_a + gb_ref[...].astype(F32) * y_g).astype(BF16)
    y = jnp.dot(y, wout_ref[...].astype(BF16), preferred_element_type=F32)
    o_ref[...] = x_ref[...] + _rms(y, gpost_ref[...])


def _mix_out(x, o_a, o_g, ga, gb, wbf, wbg, wout, g_post, tm):
    n, d = x.shape
    row = lambda width: pl.BlockSpec((tm, width), lambda i: (i, 0))
    consts = [wbf, wbg, wout, g_post]
    return pl.pallas_call(
        _mix_out_body,
        grid=(pl.cdiv(n, tm),),
        in_specs=[row(d), row(FOX_WIDTH), row(GDN_WIDTH), row(d), row(d)]
        + [_const_spec(a.shape) for a in consts],
        out_specs=row(d),
        out_shape=jax.ShapeDtypeStruct((n, d), F32),
        compiler_params=pltpu.CompilerParams(dimension_semantics=("arbitrary",), vmem_limit_bytes=VMEM_LIMIT),
        name="mix_out",
    )(x, o_a, o_g, ga, gb, *consts)


def _chunk_gates(col_ref, tri_ref, rows, heads, lane, rr, cc):
    colc = col_ref[rows, :]
    cs = _dot3(tri_ref[...], colc)
    keep = cc <= rr
    out = []
    for head in heads:
        gc = jnp.sum(jnp.where(lane == FOX_HEADS + head, cs, 0.0), axis=-1, keepdims=True)
        beta = jnp.sum(jnp.where(lane == FOX_HEADS + GDN_HEADS + head, colc, 0.0), axis=-1, keepdims=True)
        cmat = jnp.broadcast_to(gc, (CHUNK, CHUNK))
        decay = jnp.where(keep, jnp.exp(jnp.where(keep, cmat - cmat.T, 0.0)), 0.0)
        out.append((gc, beta, decay))
    return out


def _gdn_lower_body(k_ref, col_ref, tri_ref, l_ref):
    n_chunks = k_ref.shape[0] // CHUNK
    nt = (((1,), (1,)), ((), ()))
    rr = lax.broadcasted_iota(jnp.int32, (CHUNK, CHUNK), 0)
    cc = lax.broadcasted_iota(jnp.int32, (CHUNK, CHUNK), 1)
    lane = lax.broadcasted_iota(jnp.int32, (CHUNK, LANES), 1)

    def chunk_pair(i, _):
        work = []
        for c in range(2):
            n = 2 * i + c
            rows = pl.ds(pl.multiple_of(n * CHUNK, CHUNK), CHUNK)
            work.append((n, rows, _chunk_gates(col_ref, tri_ref, rows, range(GDN_HEADS), lane, rr, cc)))
        prods = []
        for n, rows, gates in work:
            for e, (_, beta, decay) in enumerate(gates):
                k = k_ref[rows, e * GDN_HEAD_DIM:(e + 1) * GDN_HEAD_DIM]
                kk = lax.dot_general((k * beta).astype(BF16), k.astype(BF16), nt, preferred_element_type=F32)
                prods.append((n, e, kk, decay))
        for n, e, kk, decay in prods:
            l_ref[0, e, n] = jnp.where(cc < rr, kk * decay, 0.0)
        return 0

    lax.fori_loop(0, n_chunks // 2, chunk_pair, 0)


def _unit_lower_inverse_body(l_ref, t_ref, lt_s, tt_s):
    for i in range(CHUNK):
        lt_s[i * CHUNK:(i + 1) * CHUNK, :] = l_ref[0, :, i, :].T
    sub = lax.broadcasted_iota(jnp.int32, (8, LANES), 0)
    for i in range(CHUNK):
        for kg in range(i // 8 + 1):
            acc = jnp.where(sub + 8 * kg == i, 1.0, 0.0)
            for j in range(8 * kg, i):
                acc = acc - lt_s[i * CHUNK + j:i * CHUNK + j + 1, :] * tt_s[j * CHUNK + 8 * kg:j * CHUNK + 8 * kg + 8, :]
            tt_s[i * CHUNK + 8 * kg:i * CHUNK + 8 * kg + 8, :] = acc
        for kg in range(i // 8 + 1, CHUNK // 8):
            tt_s[i * CHUNK + 8 * kg:i * CHUNK + 8 * kg + 8, :] = jnp.zeros((8, LANES), F32)
    for i in range(CHUNK):
        t_ref[0, :, i, :] = tt_s[i * CHUNK:(i + 1) * CHUNK, :].T


def _unit_lower_inverse(low):
    groups = low.shape[0]
    spec = pl.BlockSpec((1, LANES, CHUNK, CHUNK), lambda g: (g, 0, 0, 0))
    return pl.pallas_call(
        _unit_lower_inverse_body,
        grid=(groups,),
        in_specs=[spec], out_specs=spec,
        out_shape=jax.ShapeDtypeStruct(low.shape, F32),
        scratch_shapes=[pltpu.VMEM((CHUNK * CHUNK, LANES), F32), pltpu.VMEM((CHUNK * CHUNK, LANES), F32)],
        compiler_params=pltpu.CompilerParams(dimension_semantics=("arbitrary",), vmem_limit_bytes=VMEM_LIMIT),
        name="unit_lower_inverse",
    )(low)


def _gdn_prompt_body(q_ref, k_ref, v_ref, col_ref, tri_ref, t_ref, z_ref, ggdn_ref, o_ref, s_ref, state_s,
                     u_s, w_s, qe_s, ke_s, at_s, eg_s):
    seg = pl.program_id(1)
    n_seqs = q_ref.shape[0]
    n_chunks = q_ref.shape[1] // CHUNK
    nt = (((1,), (1,)), ((), ()))
    rr = lax.broadcasted_iota(jnp.int32, (CHUNK, CHUNK), 0)
    cc = lax.broadcasted_iota(jnp.int32, (CHUNK, CHUNK), 1)
    lane = lax.broadcasted_iota(jnp.int32, (CHUNK, LANES), 1)

    @pl.when(seg == 0)
    def _():
        state_s[...] = jnp.zeros(state_s.shape, F32)

    heads = range(GDN_HEADS)
    units = [(b, e) for b in range(n_seqs) for e in heads]
    hl = [slice(e * GDN_HEAD_DIM, (e + 1) * GDN_HEAD_DIM) for e in heads]
    fdot = functools.partial(jnp.dot, preferred_element_type=F32)

    def prepare(n, _):
        rows = pl.ds(pl.multiple_of(n * CHUNK, CHUNK), CHUNK)
        gates = [_chunk_gates(col_ref.at[b], tri_ref, rows, heads, lane, rr, cc) for b in range(n_seqs)]
        pending = []
        for b, e in units:
            gc, beta, decay = gates[b][e]
            q = q_ref[b, rows, hl[e]] * (GDN_HEAD_DIM ** -0.5)
            k = k_ref[b, rows, hl[e]]
            v = v_ref[b, rows, hl[e]]
            egc = jnp.exp(gc)
            g_last = gc[CHUNK - 1:CHUNK, :]
            tinv = t_ref[b, e, n]
            t_hi = tinv.astype(BF16)
            t_lo = (tinv - t_hi.astype(F32)).astype(BF16)
            rhs = jnp.concatenate([v * beta, k * (beta * egc)], axis=1)
            r_hi = rhs.astype(BF16)
            r_lo = (rhs - r_hi.astype(F32)).astype(BF16)
            uw = fdot(jnp.concatenate([t_hi, t_lo, t_hi], axis=1), jnp.concatenate([r_hi, r_hi, r_lo], axis=0))
            attn = lax.dot_general(q.astype(BF16), k.astype(BF16), nt, preferred_element_type=F32)
            pending.append((b, e, uw, attn, decay))
            qe_s[b, rows, hl[e]] = (q * egc).astype(BF16)
            ke_s[b, e, n] = (k * jnp.exp(g_last - gc)).T.astype(BF16)
            eg_s[b, e, pl.ds(n, 1), :] = jnp.broadcast_to(jnp.exp(g_last), (1, LANES))
        for b, e, uw, attn, decay in pending:
            u_s[b, rows, hl[e]] = uw[:, :GDN_HEAD_DIM]
            w_s[b, rows, hl[e]] = uw[:, GDN_HEAD_DIM:].astype(BF16)
            at_s[b, e, rows, :] = (attn * decay).astype(BF16)
        return 0

    lax.fori_loop(0, n_chunks, prepare, 0)

    def scan(n, states):
        rows = pl.ds(pl.multiple_of(n * CHUNK, CHUNK), CHUNK)
        sbs = [st.astype(BF16) for st in states]
        ws = [fdot(w_s[b, rows, hl[e]], sbs[i]) for i, (b, e) in enumerate(units)]
        qs = [fdot(qe_s[b, rows, hl[e]], sbs[i]) for i, (b, e) in enumerate(units)]
        v_new = [(u_s[b, rows, hl[e]] - ws[i]).astype(BF16) for i, (b, e) in enumerate(units)]
        av = [fdot(at_s[b, e, rows, :], v_new[i]) for i, (b, e) in enumerate(units)]
        kv = [fdot(ke_s[b, e, n], v_new[i]) for i, (b, e) in enumerate(units)]
        for i, (b, e) in enumerate(units):
            gate = _silu(z_ref[b, rows, hl[e]].astype(F32))
            o_ref[b, rows, hl[e]] = (_rms(qs[i] + av[i], ggdn_ref[...]) * gate).astype(o_ref.dtype)
        return tuple(states[i] * eg_s[b, e, pl.ds(n, 1), :] + kv[i] for i, (b, e) in enumerate(units))

    init = tuple(state_s[b, e] for b, e in units)
    final = lax.fori_loop(0, n_chunks, scan, init)
    for i, (b, e) in enumerate(units):
        state_s[b, e] = final[i]
        s_ref[b, e] = final[i]


GDN_SEGMENT = 512
GDN_SEQS_PER_STEP = 2


def _gdn_prompt(qkv, col, tri64, z, g_gdn, batch, seq):
    n = qkv.shape[0]
    n_chunks = seq // CHUNK
    n_seg = seq // GDN_SEGMENT
    seg_chunks = GDN_SEGMENT // CHUNK
    blk = lambda off: pl.BlockSpec((GDN_SEGMENT, GDN_WIDTH), lambda b, s: (b * n_seg + s, off))
    col_spec = pl.BlockSpec((GDN_SEGMENT, LANES), lambda b, s: (b * n_seg + s, 0))
    tri_spec = pl.BlockSpec((CHUNK, CHUNK), lambda b, s: (0, 0))
    mat_spec = pl.BlockSpec((1, GDN_HEADS, seg_chunks, CHUNK, CHUNK), lambda b, s: (b, 0, s, 0, 0))
    params = pltpu.CompilerParams(dimension_semantics=("arbitrary",) * 2, vmem_limit_bytes=VMEM_LIMIT)
    low = pl.pallas_call(
        _gdn_lower_body,
        grid=(batch, n_seg),
        in_specs=[blk(1), col_spec, tri_spec],
        out_specs=mat_spec,
        out_shape=jax.ShapeDtypeStruct((batch, GDN_HEADS, n_chunks, CHUNK, CHUNK), F32),
        compiler_params=params,
        name="gdn_lower",
    )(qkv, col, tri64)
    n_sys = batch * GDN_HEADS * n_chunks
    tinv = _unit_lower_inverse(low.reshape(n_sys // LANES, LANES, CHUNK, CHUNK)).reshape(low.shape)

    nb = GDN_SEQS_PER_STEP if batch % GDN_SEQS_PER_STEP == 0 else 1
    qkv3 = qkv.reshape(batch, seq, 3 * GDN_WIDTH)
    seq_blk = lambda width, off: pl.BlockSpec((nb, GDN_SEGMENT, width), lambda b, s: (b, s, off))
    state_spec = pl.BlockSpec((nb, GDN_HEADS, GDN_HEAD_DIM, GDN_HEAD_DIM), lambda b, s: (b, 0, 0, 0))
    o, s_fin = pl.pallas_call(
        _gdn_prompt_body,
        grid=(batch // nb, n_seg),
        in_specs=[seq_blk(GDN_WIDTH, 0), seq_blk(GDN_WIDTH, 1), seq_blk(GDN_WIDTH, 2), seq_blk(LANES, 0), tri_spec,
                  pl.BlockSpec((nb, GDN_HEADS, seg_chunks, CHUNK, CHUNK), lambda b, s: (b, 0, s, 0, 0)),
                  seq_blk(GDN_WIDTH, 0), pl.BlockSpec((1, GDN_HEAD_DIM), lambda b, s: (0, 0))],
        out_specs=[seq_blk(GDN_WIDTH, 0), state_spec],
        out_shape=[jax.ShapeDtypeStruct((batch, seq, GDN_WIDTH), BF16),
                   jax.ShapeDtypeStruct((batch, GDN_HEADS, GDN_HEAD_DIM, GDN_HEAD_DIM), F32)],
        scratch_shapes=[pltpu.VMEM((nb, GDN_HEADS, GDN_HEAD_DIM, GDN_HEAD_DIM), F32),
                        pltpu.VMEM((nb, GDN_SEGMENT, GDN_WIDTH), F32), pltpu.VMEM((nb, GDN_SEGMENT, GDN_WIDTH), BF16),
                        pltpu.VMEM((nb, GDN_SEGMENT, GDN_WIDTH), BF16),
                        pltpu.VMEM((nb, GDN_HEADS, seg_chunks, GDN_HEAD_DIM, CHUNK), BF16),
                        pltpu.VMEM((nb, GDN_HEADS, GDN_SEGMENT, CHUNK), BF16),
                        pltpu.VMEM((nb, GDN_HEADS, seg_chunks, LANES), F32)],
        compiler_params=params,
        name="gdn_prompt",
    )(qkv3, qkv3, qkv3, col.reshape(batch, seq, LANES), tri64, tinv, z.reshape(batch, seq, GDN_WIDTH), g_gdn)
    return o.reshape(n, GDN_WIDTH), s_fin


def _fox_prompt(q, col, kT, vT, cT, tq):
    n = q.shape[0]
    batch, _, seq = kT.shape
    nq = seq // tq
    pair = 2 * FOX_HEAD_DIM
    return pl.pallas_call(
        functools.partial(_fox_prompt_body, tq=tq),
        grid=(batch, FOX_HEADS // 2, nq),
        in_specs=[
            pl.BlockSpec((tq, pair), lambda b, p, i: (b * nq + i, p)),
            pl.BlockSpec((tq, LANES), lambda b, p, i: (b * nq + i, 0)),
            pl.BlockSpec((1, pair, seq), lambda b, p, i: (b, p, 0)),
            pl.BlockSpec((1, pair, seq), lambda b, p, i: (b, p, 0)),
            pl.BlockSpec((1, FOX_HEADS, seq), lambda b, p, i: (b, 0, 0)),
        ],
        out_specs=pl.BlockSpec((tq, pair), lambda b, p, i: (b * nq + i, p)),
        out_shape=jax.ShapeDtypeStruct((n, FOX_WIDTH), BF16),
        scratch_shapes=[pltpu.VMEM((2, FOX_HEAD_DIM + 16, seq), BF16), pltpu.VMEM((2, FOX_HEAD_DIM + 16, seq), BF16)],
        compiler_params=pltpu.CompilerParams(dimension_semantics=("arbitrary",) * 3, vmem_limit_bytes=VMEM_LIMIT),
        name="fox_prompt",
    )(q, col, kT, vT, cT)


ROW_TILE = 512
PROJ_TILE = 512
FOX_Q_TILE = 512


def kernel(x_prompt, x_sample, cache_k, cache_v, cache_logf, state_gdn, state_conv, page_table,
           ffn1_norm_pre, ffn1_w_gate, ffn1_w_up, ffn1_w_down, ffn1_norm_post,
           mix_norm_pre, w_in, fox_forget_bias, gdn_conv_w, gdn_dt_bias, gdn_a_log, gdn_out_norm,
           w_branch_fox, w_branch_gdn, w_out, mix_norm_post,
           ffn2_norm_pre, ffn2_w_gate, ffn2_w_up, ffn2_w_down, ffn2_norm_post):
    batch, seq, d = x_prompt.shape
    db = x_sample.shape[0]
    assert x_sample.shape[1] == 1, "sample group carries one new token per sequence"
    depth = w_in.shape[0]
    xp = x_prompt.reshape(batch * seq, d)
    xs = x_sample.reshape(db, d)
    row = lambda v: v.reshape(1, -1).astype(F32)
    st_p, st_s = [], []
    for l in range(depth):
        ffn1 = _ffn_weights(ffn1_w_gate[l], ffn1_w_up[l], ffn1_w_down[l])
        ffn2 = _ffn_weights(ffn2_w_gate[l], ffn2_w_up[l], ffn2_w_down[l])
        w = _mixer_weights(w_in[l], fox_forget_bias[l], gdn_conv_w[l], gdn_dt_bias[l], gdn_a_log[l])
        g_gdn = row(gdn_out_norm[l])
        out_w = (w_branch_fox[l], w_branch_gdn[l], w_out[l], row(mix_norm_post[l]))

        xp, xs = _ffn(xp, xs, row(ffn1_norm_pre[l]), *ffn1, row(ffn1_norm_post[l]), ROW_TILE)

        (q, kT, vT, lfT, cT, _, col, qkv, conv_p, z, ga, gb) = _proj_prompt(
            xp, row(mix_norm_pre[l]), w, batch, seq, PROJ_TILE)
        o_a = _fox_prompt(q, col, kT, vT, cT, FOX_Q_TILE)
        o_g, s_p = _gdn_prompt(qkv, col, w["tri"][:CHUNK, :CHUNK], z, g_gdn, batch, seq)
        xp = _mix_out(xp, o_a, o_g, ga, gb, *out_w, ROW_TILE)
        to_heads = lambda t: t.reshape(batch, FOX_HEADS, FOX_HEAD_DIM, seq).transpose(0, 3, 1, 2)
        st_p.append((to_heads(kT), to_heads(vT), lfT.transpose(0, 2, 1), s_p, conv_p))

        buf = state_conv[l].transpose(1, 0, 2)
        bufT = state_conv[l][:, :, :2 * GDN_WIDTH].transpose(1, 2, 0)
        (qTs, kTs, vTs, small, smallT, u, qkT, vg, zs, gas, gbs) = _proj_sample(
            xs, row(mix_norm_pre[l]), w, buf, bufT)
        o_as = _fox_decode(page_table, qTs, kTs, vTs, smallT,
                           cache_k[l].transpose(0, 2, 3, 1), cache_v[l].transpose(0, 2, 3, 1),
                           cache_logf[l].transpose(0, 2, 1))
        o_gs, s_s = _gdn_decode(qkT, vg, smallT, state_gdn[l], zs, g_gdn)
        xs = _mix_out(xs, o_as.astype(BF16), o_gs.astype(BF16), gas, gbs, *out_w, db)
        xp, xs = _ffn(xp, xs, row(ffn2_norm_pre[l]), *ffn2, row(ffn2_norm_post[l]), ROW_TILE)
        new_conv = jnp.concatenate([state_conv[l][:, 1:], u[:, None, :]], axis=1)
        st_s.append((kTs.T.reshape(db, 1, FOX_HEADS, FOX_HEAD_DIM), vTs.T.reshape(db, 1, FOX_HEADS, FOX_HEAD_DIM),
                     small[:, :FOX_HEADS].reshape(db, 1, FOX_HEADS), s_s, new_conv))

    stack = lambda states, i: jnp.stack([s[i] for s in states], axis=0)
    return (xp.reshape(batch, seq, d), xs.reshape(db, 1, d),
            *[stack(st_p, i) for i in range(5)], *[stack(st_s, i) for i in range(5)])
```
